```python
import jax, jax.numpy as jnp
from jax import lax
import numpy as np

D_MODEL = 1024
BATCH = 8
SEQ = 4096
DEPTH = 2

N_EVEN = (DEPTH + 1) // 2
N_ODD = DEPTH // 2

DN_ALPHA = (2.0 * DEPTH) ** 0.25
DN_BETA = (8.0 * DEPTH) ** -0.25

GDN_HEADS = 4
GDN_HEAD_DIM = 128
GDN_WIDTH = GDN_HEADS * GDN_HEAD_DIM
GDN_CONV = 4
GDN_CHUNK = 64
SC_WIDTH = 512
SC_CONV = 3
EVEN_SPLITS = (3 * GDN_WIDTH, GDN_WIDTH, GDN_HEADS, GDN_HEADS, SC_WIDTH, SC_WIDTH, SC_WIDTH)
EVEN_IN = sum(EVEN_SPLITS)
EVEN_MIX = GDN_WIDTH + SC_WIDTH

MLA_HEADS = 8
Q_LORA = 384
KV_LORA = 256
QK_NOPE = 128
QK_ROPE = 64
V_DIM = 128
ODD_SPLITS = (Q_LORA, KV_LORA, QK_ROPE)
ODD_IN = sum(ODD_SPLITS)
ODD_MIX = MLA_HEADS * V_DIM
ROPE_THETA = 10000.0
Q_BLOCK = 128

N_EXPERTS = 32
N_GROUPS = 4
EXPERTS_PER_GROUP = N_EXPERTS // N_GROUPS
TOP_K = 2
D_EXPERT = 512
MOE_BLOCK = 128

NORM_EPS = 1e-6
LN_EPS = 1e-5

kernel_name = 'hybrid_gdn_shortconv_mla_grouped_moe'


def split_cols(x, sizes):
    idx = np.cumsum(sizes)[:-1].tolist()
    return jnp.split(x, idx, axis=-1)


def layer_norm(x, g, b):
    xf = x.astype(jnp.float32)
    mu = xf.mean(-1, keepdims=True)
    var = jnp.square(xf - mu).mean(-1, keepdims=True)
    return ((xf - mu) * lax.rsqrt(var + LN_EPS) * g.astype(jnp.float32) + b.astype(jnp.float32)).astype(x.dtype)


def rms_norm(x, w):
    xf = x.astype(jnp.float32)
    y = xf * lax.rsqrt(jnp.mean(xf * xf, -1, keepdims=True) + NORM_EPS) * w.astype(jnp.float32)
    return y.astype(x.dtype)


def l2_norm(x):
    xf = x.astype(jnp.float32)
    return xf * lax.rsqrt(jnp.sum(xf * xf, -1, keepdims=True) + NORM_EPS)


def causal_dwconv(x, w):
    k = w.shape[0]
    return lax.conv_general_dilated(
        x, w[:, None, :].astype(x.dtype), window_strides=(1,), padding=[(k - 1, 0)],
        dimension_numbers=('NWC', 'WIO', 'NWC'), feature_group_count=x.shape[-1])


def chunk_gated_delta_rule(q, k, v, g, beta):
    bsz, s, h, dk = q.shape
    dv = v.shape[-1]
    c = GDN_CHUNK
    n = s // c

    def to_chunks(t):
        t = jnp.moveaxis(t, 2, 1)
        return t.reshape(bsz, h, n, c, *t.shape[3:])

    q = to_chunks(q * dk ** -0.5)
    k = to_chunks(k)
    v = to_chunks(v.astype(jnp.float32))
    gc = jnp.cumsum(to_chunks(g), -1)
    beta = to_chunks(beta)
    causal = jnp.tril(jnp.ones((c, c), bool))
    strict = jnp.tril(jnp.ones((c, c), bool), -1)
    decay = jnp.exp(jnp.where(causal, gc[..., :, None] - gc[..., None, :], -jnp.inf))
    kb = k * beta[..., None]
    lower = jnp.where(strict, jnp.einsum('bhnid,bhnjd->bhnij', kb, k) * decay, 0.0)
    rhs = jnp.concatenate([v * beta[..., None], kb * jnp.exp(gc)[..., None]], -1)
    sol = lax.linalg.triangular_solve(lower, rhs, left_side=True, lower=True, unit_diagonal=True)
    u, w = sol[..., :dv], sol[..., dv:]
    intra = jnp.where(causal, jnp.einsum('bhnid,bhnjd->bhnij', q, k) * decay, 0.0)
    q_dec = q * jnp.exp(gc)[..., None]
    k_dec = k * jnp.exp(gc[..., -1:] - gc)[..., None]
    g_last = jnp.exp(gc[..., -1])

    def step(state, inp):
        qd, kd, uc, wc, ic, gl = inp
        v_new = uc - jnp.einsum('bhck,bhkv->bhcv', wc, state)
        o = jnp.einsum('bhck,bhkv->bhcv', qd, state) + jnp.einsum('bhcj,bhjv->bhcv', ic, v_new)
        state = state * gl[..., None, None] + jnp.einsum('bhck,bhcv->bhkv', kd, v_new)
        return state, o

    xs = tuple(jnp.moveaxis(t, 2, 0) for t in (q_dec, k_dec, u, w, intra, g_last))
    _, o = lax.scan(step, jnp.zeros((bsz, h, dk, dv), jnp.float32), xs)
    o = jnp.moveaxis(o, 0, 2).reshape(bsz, h, s, dv)
    return jnp.moveaxis(o, 1, 2)


def even_mixer(h, w_in, gdn_conv_w, a_log, dt_bias, gdn_norm_w, sc_conv_w, w_out):
    bsz, s, _ = h.shape
    qkv, z, b, a, sc_b, sc_c, sc_h = split_cols(h @ w_in, EVEN_SPLITS)
    qkv = jax.nn.silu(causal_dwconv(qkv, gdn_conv_w))
    q, k, v = [t.reshape(bsz, s, GDN_HEADS, GDN_HEAD_DIM) for t in jnp.split(qkv, 3, -1)]
    q, k = l2_norm(q), l2_norm(k)
    beta = jax.nn.sigmoid(b.astype(jnp.float32))
    g = -jnp.exp(a_log.astype(jnp.float32)) * jax.nn.softplus(
        a.astype(jnp.float32) + dt_bias.astype(jnp.float32))
    o = chunk_gated_delta_rule(q, k, v, g, beta)
    o = rms_norm(o, gdn_norm_w) * jax.nn.silu(z.astype(jnp.float32)).reshape(bsz, s, GDN_HEADS, GDN_HEAD_DIM)
    y_a = o.reshape(bsz, s, GDN_WIDTH).astype(h.dtype)
    y_b = sc_b * causal_dwconv(sc_c * sc_h, sc_conv_w)
    return jnp.concatenate([y_a, y_b], -1) @ w_out


def rope_tables(positions, dim):
    inv = ROPE_THETA ** (-jnp.arange(0, dim, 2, dtype=jnp.float32) / dim)
    ang = positions.astype(jnp.float32)[..., None] * inv
    return jnp.cos(ang), jnp.sin(ang)


def apply_rope(x, cos, sin):
    x1, x2 = jnp.split(x.astype(jnp.float32), 2, -1)
    return jnp.concatenate([x1 * cos - x2 * sin, x2 * cos + x1 * sin], -1).astype(x.dtype)


def blocked_causal_attention(q_nope, q_rope, k_nope, k_rope, v):
    bsz, s, h, dn = q_nope.shape
    nb = s // Q_BLOCK
    scale = (QK_NOPE + QK_ROPE) ** -0.5
    qn = jnp.moveaxis(q_nope.reshape(bsz, nb, Q_BLOCK, h, dn), 1, 0)
    qr = jnp.moveaxis(q_rope.reshape(bsz, nb, Q_BLOCK, h, QK_ROPE), 1, 0)
    kpos = jnp.arange(s)

    def block(args):
        qnb, qrb, i = args
        sc = (jnp.einsum('bqhd,bkhd->bhqk', qnb, k_nope) + jnp.einsum('bqhd,bkd->bhqk', qrb, k_rope))
        sc = sc.astype(jnp.float32) * scale
        qpos = i * Q_BLOCK + jnp.arange(Q_BLOCK)
        sc = jnp.where(kpos[None, :] <= qpos[:, None], sc, -jnp.inf)
        p = jax.nn.softmax(sc, -1)
        return jnp.einsum('bhqk,bkhd->bqhd', p.astype(v.dtype), v)

    o = lax.map(block, (qn, qr, jnp.arange(nb)))
    return jnp.moveaxis(o, 0, 1).reshape(bsz, s, h * v.shape[-1])


def mla_mixer(h, positions, w_in, q_a_norm, w_q_b, kv_a_norm, w_kv_b, w_out):
    bsz, s, _ = h.shape
    qa, kva, k_rope = split_cols(h @ w_in, ODD_SPLITS)
    q = (rms_norm(qa, q_a_norm) @ w_q_b).reshape(bsz, s, MLA_HEADS, QK_NOPE + QK_ROPE)
    kv = (rms_norm(kva, kv_a_norm) @ w_kv_b).reshape(bsz, s, MLA_HEADS, QK_NOPE + V_DIM)
    q_nope, q_rope = q[..., :QK_NOPE], q[..., QK_NOPE:]
    k_nope, v = kv[..., :QK_NOPE], kv[..., QK_NOPE:]
    cos, sin = rope_tables(positions, QK_ROPE)
    q_rope = apply_rope(q_rope, cos[:, :, None], sin[:, :, None])
    k_rope = apply_rope(k_rope, cos, sin)
    return blocked_causal_attention(q_nope, q_rope, k_nope, k_rope, v) @ w_out


def moe_ffn(h, router_w, router_bias, w_gate, w_up, w_down):
    bsz, s, d = h.shape
    t = bsz * s
    hf = h.reshape(t, d)
    scores = jax.nn.sigmoid((hf @ router_w).astype(jnp.float32))
    biased = (scores + router_bias.astype(jnp.float32)).reshape(t, N_GROUPS, EXPERTS_PER_GROUP)
    group_score = lax.top_k(biased, 2)[0].sum(-1)
    gsel = jnp.argmax(group_score, -1)
    in_group = jnp.take_along_axis(biased, gsel[:, None, None], axis=1)[:, 0]
    _, local = lax.top_k(in_group, TOP_K)
    eidx = gsel[:, None] * EXPERTS_PER_GROUP + local
    wsel = jnp.take_along_axis(scores, eidx, -1)
    wsel = wsel / wsel.sum(-1, keepdims=True)
    n_assign = t * TOP_K
    e_flat = eidx.reshape(n_assign)
    order = jnp.argsort(e_flat)
    e_sorted = e_flat[order]
    tok_sorted = order // TOP_K
    counts = jnp.bincount(e_flat, length=N_EXPERTS)
    padded = (counts + MOE_BLOCK - 1) // MOE_BLOCK * MOE_BLOCK
    start = jnp.cumsum(counts) - counts
    pstart = jnp.cumsum(padded) - padded
    dest = pstart[e_sorted] + jnp.arange(n_assign) - start[e_sorted]
    n_blocks = -(-n_assign // MOE_BLOCK) + N_EXPERTS
    rows = n_blocks * MOE_BLOCK
    xs = jnp.zeros((rows, d), h.dtype).at[dest].set(hf[tok_sorted])
    block_expert = jnp.minimum(
        jnp.searchsorted(jnp.cumsum(padded), jnp.arange(n_blocks) * MOE_BLOCK, side='right'), N_EXPERTS - 1)

    def expert_block(args):
        xb, e = args
        return (jax.nn.silu(xb @ w_gate[e]) * (xb @ w_up[e])) @ w_down[e]

    ys = lax.map(expert_block, (xs.reshape(n_blocks, MOE_BLOCK, d), block_expert)).reshape(rows, d)
    contrib = ys[dest] * wsel.reshape(n_assign)[order][:, None].astype(h.dtype)
    return jax.ops.segment_sum(contrib, tok_sorted, num_segments=t).reshape(bsz, s, d)


def setup_inputs(seed: int = 0) -> dict:
    key = jax.random.key(seed)
    ks = iter(jax.random.split(key, 32))
    f32 = jnp.float32

    def nrm(shape, scale):
        return jax.random.normal(next(ks), shape, f32) * scale

    x = nrm((BATCH, SEQ, D_MODEL), 1.0)
    c = nrm((BATCH, D_MODEL), 1.0)
    positions = (jax.random.randint(next(ks), (BATCH, 1), 0, 1024) + jnp.arange(SEQ)[None, :]).astype(jnp.int32)
    ada_w = nrm((DEPTH, 2, D_MODEL, 3 * D_MODEL), 0.2 * D_MODEL ** -0.5)
    ada_b = nrm((DEPTH, 2, 3 * D_MODEL), 0.02)
    ln_g = 1.0 + nrm((DEPTH, 2, D_MODEL), 0.02)
    ln_b = nrm((DEPTH, 2, D_MODEL), 0.02)
    w_in_e = nrm((N_EVEN, D_MODEL, EVEN_IN), D_MODEL ** -0.5)
    gdn_conv_w = nrm((N_EVEN, GDN_CONV, 3 * GDN_WIDTH), GDN_CONV ** -0.5)
    gdn_a_log = jnp.log(jax.random.uniform(next(ks), (N_EVEN, GDN_HEADS), f32, 1.0, 16.0))
    dt = jnp.exp(jax.random.uniform(next(ks), (N_EVEN, GDN_HEADS), f32, float(np.log(1e-3)), float(np.log(1e-1))))
    gdn_dt_bias = dt + jnp.log(-jnp.expm1(-dt))
    gdn_norm_w = 1.0 + nrm((N_EVEN, GDN_HEAD_DIM), 0.02)
    sc_conv_w = nrm((N_EVEN, SC_CONV, SC_WIDTH), SC_CONV ** -0.5)
    w_out_e = nrm((N_EVEN, EVEN_MIX, D_MODEL), DN_BETA * EVEN_MIX ** -0.5)
    w_in_o = nrm((N_ODD, D_MODEL, ODD_IN), D_MODEL ** -0.5)
    q_a_norm = 1.0 + nrm((N_ODD, Q_LORA), 0.02)
    w_q_b = nrm((N_ODD, Q_LORA, MLA_HEADS * (QK_NOPE + QK_ROPE)), Q_LORA ** -0.5)
    kv_a_norm = 1.0 + nrm((N_ODD, KV_LORA), 0.02)
    w_kv_b = nrm((N_ODD, KV_LORA, MLA_HEADS * (QK_NOPE + V_DIM)), KV_LORA ** -0.5)
    w_out_o = nrm((N_ODD, ODD_MIX, D_MODEL), DN_BETA * ODD_MIX ** -0.5)
    router_w = nrm((D_MODEL, N_EXPERTS), D_MODEL ** -0.5)
    router_bias = nrm((N_EXPERTS,), 0.01)
    w_gate = nrm((DEPTH, N_EXPERTS, D_MODEL, D_EXPERT), D_MODEL ** -0.5)
    w_up = nrm((DEPTH, N_EXPERTS, D_MODEL, D_EXPERT), D_MODEL ** -0.5)
    w_down = nrm((DEPTH, N_EXPERTS, D_EXPERT, D_MODEL), DN_BETA * D_EXPERT ** -0.5)
    return {'x': x, 'c': c, 'positions': positions, 'ada_w': ada_w, 'ada_b': ada_b,
            'ln_g': ln_g, 'ln_b': ln_b, 'w_in_e': w_in_e, 'gdn_conv_w': gdn_conv_w,
            'gdn_a_log': gdn_a_log, 'gdn_dt_bias': gdn_dt_bias, 'gdn_norm_w': gdn_norm_w,
            'sc_conv_w': sc_conv_w, 'w_out_e': w_out_e, 'w_in_o': w_in_o, 'q_a_norm': q_a_norm,
            'w_q_b': w_q_b, 'kv_a_norm': kv_a_norm, 'w_kv_b': w_kv_b, 'w_out_o': w_out_o,
            'router_w': router_w, 'router_bias': router_bias, 'w_gate': w_gate, 'w_up': w_up,
            'w_down': w_down}


def reference(x, c, positions, ada_w, ada_b, ln_g, ln_b, w_in_e, gdn_conv_w, gdn_a_log,
              gdn_dt_bias, gdn_norm_w, sc_conv_w, w_out_e, w_in_o, q_a_norm, w_q_b, kv_a_norm,
              w_kv_b, w_out_o, router_w, router_bias, w_gate, w_up, w_down):
    cond = jax.nn.silu(c)
    for i in range(DEPTH):
        j = i // 2
        mod = jnp.einsum('bd,sde->sbe', cond, ada_w[i]) + ada_b[i][:, None, :]
        shift, scale, gate = jnp.split(mod[0][:, None, :], 3, -1)
        hmix = x * (1.0 + scale) + shift
        if i % 2 == 0:
            y = even_mixer(hmix, w_in_e[j], gdn_conv_w[j], gdn_a_log[j], gdn_dt_bias[j],
                           gdn_norm_w[j], sc_conv_w[j], w_out_e[j])
        else:
            y = mla_mixer(hmix, positions, w_in_o[j], q_a_norm[j], w_q_b[j], kv_a_norm[j],
                          w_kv_b[j], w_out_o[j])
        x = layer_norm(DN_ALPHA * x + (1.0 + gate) * y, ln_g[i, 0], ln_b[i, 0])
        shift, scale, gate = jnp.split(mod[1][:, None, :], 3, -1)
        y = moe_ffn(x * (1.0 + scale) + shift, router_w, router_bias, w_gate[i], w_up[i], w_down[i])
        x = layer_norm(DN_ALPHA * x + (1.0 + gate) * y, ln_g[i, 1], ln_b[i, 1])
    return x
```

```python
import functools

import numpy as np
import jax
import jax.numpy as jnp
from jax import lax
from jax.experimental import pallas as pl
from jax.experimental.pallas import tpu as pltpu

F32 = jnp.float32
BF16 = jnp.bfloat16
HIGHEST = lax.Precision.HIGHEST

D_MODEL = 1024
DEPTH = 2
DN_ALPHA = (2.0 * DEPTH) ** 0.25

GDN_HEADS = 4
GDN_HEAD_DIM = 128
GDN_WIDTH = GDN_HEADS * GDN_HEAD_DIM
GDN_CONV = 4
GDN_CHUNK = 64
SC_WIDTH = 512
SC_CONV = 3

MLA_HEADS = 8
Q_LORA = 384
KV_LORA = 256
QK_NOPE = 128
QK_ROPE = 64
V_DIM = 128
ROPE_THETA = 10000.0

N_EXPERTS = 32
N_GROUPS = 4
EXPERTS_PER_GROUP = N_EXPERTS // N_GROUPS
D_EXPERT = 512

NORM_EPS = 1e-6
LN_EPS = 1e-5

LANES = 128
HALO = 16
TOKEN_TILE = 512
GDN_TILE = 256
ATTN_TILE = 512
MOE_BLOCK = 256
VMEM_LIMIT = 48 * 1024 * 1024


def _cparams(*sem):
    return pltpu.CompilerParams(dimension_semantics=sem, vmem_limit_bytes=VMEM_LIMIT)


def _sigmoid(x):
    return 1.0 / (1.0 + jnp.exp(-x))


def _silu(x):
    return x * _sigmoid(x)


def _dot(a, b):
    return jnp.dot(a, b, preferred_element_type=F32)


def _dot_nt(a, b, precision=None):
    return lax.dot_general(a, b, (((1,), (1,)), ((), ())), precision=precision,
                           preferred_element_type=F32)


def _dot_tn(a, b):
    return lax.dot_general(a, b, (((0,), (0,)), ((), ())), preferred_element_type=F32)


def _ada_kernel(c_ref, w_ref, b_ref, o_ref):
    cond = _silu(c_ref[...])
    o_ref[0] = jnp.dot(cond, w_ref[0], precision=HIGHEST, preferred_element_type=F32) + b_ref[0]


def ada_mod(c, ada_w, ada_b):
    nl = ada_w.shape[0] * ada_w.shape[1]
    bsz, d = c.shape
    w = ada_w.reshape(nl, d, 3 * d)
    b = ada_b.reshape(nl, 1, 3 * d)
    return pl.pallas_call(
        _ada_kernel,
        grid=(nl, 3),
        in_specs=[pl.BlockSpec((bsz, d), lambda l, j: (0, 0)),
                  pl.BlockSpec((1, d, d), lambda l, j: (l, 0, j)),
                  pl.BlockSpec((1, 1, d), lambda l, j: (l, 0, j))],
        out_specs=pl.BlockSpec((1, bsz, d), lambda l, j: (l, 0, j)),
        out_shape=jax.ShapeDtypeStruct((nl, bsz, 3 * d), F32),
        compiler_params=_cparams("parallel", "parallel"),
        name="ada_mod",
    )(c, w, b)


EVEN_MAIN = 3 * GDN_WIDTH + GDN_WIDTH + 3 * SC_WIDTH


def _even_in_kernel(x_ref, mod_ref, w1_ref, w2_ref, alog_ref, dtb_ref, o1_ref, o2_ref, o2t_ref):
    d = D_MODEL
    mod = mod_ref[0, 0]
    h = (x_ref[0] * (1.0 + mod[:, d:2 * d]) + mod[:, :d]).astype(BF16)
    for j in range(EVEN_MAIN // 512):
        o1_ref[0, :, j * 512:(j + 1) * 512] = _dot(h, w1_ref[:, j * 512:(j + 1) * 512]).astype(BF16)
    r = _dot(h, w2_ref[...])
    lane = lax.broadcasted_iota(jnp.int32, r.shape, 1)
    a = r + dtb_ref[...]
    softplus = jnp.maximum(a, 0.0) + jnp.log(1.0 + jnp.exp(-jnp.abs(a)))
    bg = jnp.where(lane < GDN_HEADS, _sigmoid(r), -jnp.exp(alog_ref[...]) * softplus)
    o2_ref[0] = bg
    o2t_ref[0] = bg.T[:8]


def even_in_proj(x, mod, w_in, a_log, dt_bias):
    bsz, s, d = x.shape
    tm = min(TOKEN_TILE, s)
    q_end = 4 * GDN_WIDTH
    w1 = jnp.concatenate([w_in[:, :q_end], w_in[:, q_end + 2 * GDN_HEADS:]], axis=1).astype(BF16)
    w2 = jnp.pad(w_in[:, q_end:q_end + 2 * GDN_HEADS], ((0, 0), (0, LANES - 2 * GDN_HEADS))).astype(BF16)
    alog = jnp.pad(a_log.astype(F32), (GDN_HEADS, LANES - 2 * GDN_HEADS)).reshape(1, LANES)
    dtb = jnp.pad(dt_bias.astype(F32), (GDN_HEADS, LANES - 2 * GDN_HEADS)).reshape(1, LANES)
    const = lambda b, i: (0, 0)
    return pl.pallas_call(
        _even_in_kernel,
        grid=(bsz, s // tm),
        in_specs=[pl.BlockSpec((1, tm, d), lambda b, i: (b, i, 0)),
                  pl.BlockSpec((1, 1, 1, 3 * d), lambda b, i: (b, 0, 0, 0)),
                  pl.BlockSpec((d, EVEN_MAIN), const),
                  pl.BlockSpec((d, LANES), const),
                  pl.BlockSpec((1, LANES), const),
                  pl.BlockSpec((1, LANES), const)],
        out_specs=[pl.BlockSpec((1, tm, EVEN_MAIN), lambda b, i: (b, i, 0)),
                   pl.BlockSpec((1, tm, LANES), lambda b, i: (b, i, 0)),
                   pl.BlockSpec((1, 8, tm), lambda b, i: (b, 0, i))],
        out_shape=[jax.ShapeDtypeStruct((bsz, s, EVEN_MAIN), BF16),
                   jax.ShapeDtypeStruct((bsz, s, LANES), F32),
                   jax.ShapeDtypeStruct((bsz, 8, s), F32)],
        compiler_params=_cparams("parallel", "parallel"),
        name="even_in_proj",
    )(x, mod, w1, w2, alog, dtb)


def _causal_conv(x, halo, w, taps):
    rows = x.shape[0]
    xf = jnp.concatenate([halo, x], axis=0)
    y = w[taps - 1:taps] * x
    for j in range(taps - 1):
        off = HALO - (taps - 1) + j
        y = y + w[j:j + 1] * xf[off:off + rows]
    return y


def _inv_unit_lower(low, xor_ij):
    c = low.shape[0]
    eye = jnp.where(xor_ij == 0, 1.0, 0.0)
    m = eye - jnp.where(xor_ij == 1, low, 0.0)
    s = 2
    while s < c:
        shift = s.bit_length() - 1
        cs = jnp.where((xor_ij >> shift) == 1, low, 0.0).astype(BF16)
        mb = m.astype(BF16)
        m = m - _dot(mb, _dot(cs, mb).astype(BF16))
        s *= 2
    return m


def _gdn_prep_kernel(qkv_ref, halo_ref, cw_ref, bg_ref, bgt_ref, ltri_ref, utri_ref,
                     qd_ref, kd_ref, u_ref, w_ref, ic_ref, gl_ref):
    ts = qkv_ref.shape[1]
    c = GDN_CHUNK
    hd = GDN_HEAD_DIM
    i = pl.program_id(1)
    x = qkv_ref[0].astype(F32)
    halo = jnp.where(i > 0, halo_ref[0].astype(F32), 0.0)
    y = _silu(_causal_conv(x, halo, cw_ref[...], GDN_CONV))

    bg = bg_ref[0]
    gc_col = jnp.dot(ltri_ref[...], bg, precision=HIGHEST, preferred_element_type=F32)
    gc_row = jnp.dot(bgt_ref[0], utri_ref[...], precision=HIGHEST, preferred_element_type=F32)

    ii = lax.broadcasted_iota(jnp.int32, (c, c), 0)
    jj = lax.broadcasted_iota(jnp.int32, (c, c), 1)
    xor_ij = ii ^ jj
    causal = ii >= jj
    strict = ii > jj

    for h in range(GDN_HEADS):
        q = y[:, h * hd:(h + 1) * hd]
        k = y[:, GDN_WIDTH + h * hd:GDN_WIDTH + (h + 1) * hd]
        v = y[:, 2 * GDN_WIDTH + h * hd:2 * GDN_WIDTH + (h + 1) * hd]
        q = q * lax.rsqrt(jnp.sum(q * q, -1, keepdims=True) + NORM_EPS) * (hd ** -0.5)
        k = k * lax.rsqrt(jnp.sum(k * k, -1, keepdims=True) + NORM_EPS)
        for n in range(ts // c):
            r0 = n * c
            qn, kn, vn = q[r0:r0 + c], k[r0:r0 + c], v[r0:r0 + c]
            beta = bg[r0:r0 + c, h:h + 1]
            gcol = gc_col[r0:r0 + c, GDN_HEADS + h:GDN_HEADS + h + 1]
            grow = gc_row[GDN_HEADS + h:GDN_HEADS + h + 1, r0:r0 + c]
            glast = gc_col[r0 + c - 1:r0 + c, GDN_HEADS + h:GDN_HEADS + h + 1]
            decay = jnp.exp(jnp.where(causal, gcol - grow, -jnp.inf))
            kb = kn * beta
            kbf = kn.astype(BF16)
            low = jnp.where(strict, _dot_nt(kb.astype(BF16), kbf) * decay, 0.0)
            intra = jnp.where(causal, _dot_nt(qn.astype(BF16), kbf) * decay, 0.0)
            egc = jnp.exp(gcol)
            rhs = jnp.concatenate([vn * beta, kb * egc], axis=-1).astype(BF16)
            sol = _dot(_inv_unit_lower(low, xor_ij).astype(BF16), rhs)
            cols = slice(h * hd, (h + 1) * hd)
            u_ref[0, r0:r0 + c, cols] = sol[:, :hd]
            w_ref[0, r0:r0 + c, cols] = sol[:, hd:].astype(BF16)
            qd_ref[0, r0:r0 + c, cols] = (qn * egc).astype(BF16)
            kd_ref[0, r0:r0 + c, cols] = (kn * jnp.exp(glast - gcol)).astype(BF16)
            ic_ref[0, r0:r0 + c, h * c:(h + 1) * c] = intra.astype(BF16)
            gl_ref[0, n, h:h + 1, :] = jnp.broadcast_to(jnp.exp(glast), (1, hd))


def gdn_prep(o1, bg, bgt, conv_w):
    bsz, s, _ = o1.shape
    ts = min(GDN_TILE, s)
    c = GDN_CHUNK
    nc = ts // c
    qkv_w = 3 * GDN_WIDTH
    r = np.arange(ts)
    same = (r[:, None] // c) == (r[None, :] // c)
    ltri = jnp.asarray((same & (r[:, None] >= r[None, :])).astype(np.float32))
    utri = jnp.asarray((same & (r[:, None] <= r[None, :])).astype(np.float32))
    hb = ts // HALO
    tok = lambda b, i: (b, i, 0)
    const = lambda b, i: (0, 0)
    wide = jax.ShapeDtypeStruct((bsz, s, GDN_WIDTH), BF16)
    return pl.pallas_call(
        _gdn_prep_kernel,
        grid=(bsz, s // ts),
        in_specs=[pl.BlockSpec((1, ts, qkv_w), tok),
                  pl.BlockSpec((1, HALO, qkv_w), lambda b, i: (b, jnp.maximum(i * hb - 1, 0), 0)),
                  pl.BlockSpec((GDN_CONV, qkv_w), const),
                  pl.BlockSpec((1, ts, LANES), tok),
                  pl.BlockSpec((1, 8, ts), lambda b, i: (b, 0, i)),
                  pl.BlockSpec((ts, ts), const),
                  pl.BlockSpec((ts, ts), const)],
        out_specs=[pl.BlockSpec((1, ts, GDN_WIDTH), tok),
                   pl.BlockSpec((1, ts, GDN_WIDTH), tok),
                   pl.BlockSpec((1, ts, GDN_WIDTH), tok),
                   pl.BlockSpec((1, ts, GDN_WIDTH), tok),
                   pl.BlockSpec((1, ts, GDN_HEADS * c), tok),
                   pl.BlockSpec((1, nc, GDN_HEADS, GDN_HEAD_DIM), lambda b, i: (b, i, 0, 0))],
        out_shape=[wide, wide, jax.ShapeDtypeStruct((bsz, s, GDN_WIDTH), F32), wide,
                   jax.ShapeDtypeStruct((bsz, s, GDN_HEADS * c), BF16),
                   jax.ShapeDtypeStruct((bsz, s // c, GDN_HEADS, GDN_HEAD_DIM), F32)],
        compiler_params=_cparams("parallel", "parallel"),
        name="gdn_prep",
    )(o1, o1, conv_w.astype(F32), bg, bgt, ltri, utri)


def _gdn_scan_kernel(qd_ref, kd_ref, u_ref, w_ref, ic_ref, gl_ref, o_ref, state_ref):
    ts = qd_ref.shape[1]
    c = GDN_CHUNK
    hd = GDN_HEAD_DIM

    @pl.when(pl.program_id(1) == 0)
    def _():
        state_ref[...] = jnp.zeros_like(state_ref)

    for n in range(ts // c):
        rows = slice(n * c, (n + 1) * c)
        for h in range(GDN_HEADS):
            cols = slice(h * hd, (h + 1) * hd)
            st = state_ref[h]
            stb = st.astype(BF16)
            v_new = u_ref[0, rows, cols] - _dot(w_ref[0, rows, cols], stb)
            vb = v_new.astype(BF16)
            o_ref[0, rows, cols] = _dot(qd_ref[0, rows, cols], stb) + _dot(ic_ref[0, rows, h * c:(h + 1) * c], vb)
            state_ref[h] = st * gl_ref[0, n, h:h + 1, :] + _dot_tn(kd_ref[0, rows, cols], vb)


def gdn_scan(qd, kd, u, w, ic, gl):
    bsz, s, _ = qd.shape
    ts = min(GDN_TILE, s)
    c = GDN_CHUNK
    tok = lambda b, i: (b, i, 0)
    return pl.pallas_call(
        _gdn_scan_kernel,
        grid=(bsz, s // ts),
        in_specs=[pl.BlockSpec((1, ts, GDN_WIDTH), tok)] * 4 + [
            pl.BlockSpec((1, ts, GDN_HEADS * c), tok),
            pl.BlockSpec((1, ts // c, GDN_HEADS, GDN_HEAD_DIM), lambda b, i: (b, i, 0, 0))],
        out_specs=pl.BlockSpec((1, ts, GDN_WIDTH), tok),
        out_shape=jax.ShapeDtypeStruct((bsz, s, GDN_WIDTH), F32),
        scratch_shapes=[pltpu.VMEM((GDN_HEADS, GDN_HEAD_DIM, GDN_HEAD_DIM), F32)],
        compiler_params=_cparams("parallel", "arbitrary"),
        name="gdn_scan",
    )(qd, kd, u, w, ic, gl)


def _even_post_kernel(o_ref, z_ref, scb_ref, scc_ref, sch_ref, scc_halo_ref, sch_halo_ref,
                      nw_ref, cw_ref, y_ref):
    hd = GDN_HEAD_DIM
    i = pl.program_id(1)
    o = o_ref[0]
    z = z_ref[0].astype(F32)
    nw = nw_ref[...]
    for h in range(GDN_HEADS):
        cols = slice(h * hd, (h + 1) * hd)
        oh = o[:, cols]
        on = oh * lax.rsqrt(jnp.mean(oh * oh, -1, keepdims=True) + NORM_EPS) * nw
        y_ref[0, :, cols] = (on * _silu(z[:, cols])).astype(BF16)
    ch = scc_ref[0].astype(F32) * sch_ref[0].astype(F32)
    ch_halo = jnp.where(i > 0, scc_halo_ref[0].astype(F32) * sch_halo_ref[0].astype(F32), 0.0)
    yb = scb_ref[0].astype(F32) * _causal_conv(ch, ch_halo, cw_ref[...], SC_CONV)
    y_ref[0, :, GDN_WIDTH:] = yb.astype(BF16)


def even_post(o, o1, norm_w, sc_conv_w):
    bsz, s, _ = o.shape
    tm = min(TOKEN_TILE, s)
    hb = tm // HALO
    wb = 512
    blk = lambda j: pl.BlockSpec((1, tm, wb), lambda b, i, j=j: (b, i, j))
    halo = lambda j: pl.BlockSpec((1, HALO, wb), lambda b, i, j=j: (b, jnp.maximum(i * hb - 1, 0), j))
    const = lambda b, i: (0, 0)
    return pl.pallas_call(
        _even_post_kernel,
        grid=(bsz, s // tm),
        in_specs=[pl.BlockSpec((1, tm, GDN_WIDTH), lambda b, i: (b, i, 0)),
                  blk(3), blk(4), blk(5), blk(6), halo(5), halo(6),
                  pl.BlockSpec((1, GDN_HEAD_DIM), const),
                  pl.BlockSpec((SC_CONV, SC_WIDTH), const)],
        out_specs=pl.BlockSpec((1, tm, GDN_WIDTH + SC_WIDTH), lambda b, i: (b, i, 0)),
        out_shape=jax.ShapeDtypeStruct((bsz, s, GDN_WIDTH + SC_WIDTH), BF16),
        compiler_params=_cparams("parallel", "parallel"),
        name="even_post",
    )(o, o1, o1, o1, o1, o1, o1, norm_w.astype(F32).reshape(1, -1), sc_conv_w.astype(F32))


def _layer_norm(r, g, b):
    mu = jnp.mean(r, -1, keepdims=True)
    rc = r - mu
    var = jnp.mean(rc * rc, -1, keepdims=True)
    return rc * lax.rsqrt(var + LN_EPS) * g + b


def _route(logits_t, bias):
    scores = _sigmoid(logits_t)
    biased = scores + bias
    t = logits_t.shape[1]
    epg = EXPERTS_PER_GROUP
    sub = lax.broadcasted_iota(jnp.int32, (epg, t), 0).astype(F32)
    best = None
    for g in range(N_GROUPS):
        bgp = biased[g * epg:(g + 1) * epg]
        m1 = jnp.max(bgp, axis=0, keepdims=True)
        i1 = jnp.min(jnp.where(bgp == m1, sub, float(epg)), axis=0, keepdims=True)
        rest = jnp.where(sub == i1, -jnp.inf, bgp)
        m2 = jnp.max(rest, axis=0, keepdims=True)
        i2 = jnp.min(jnp.where(rest == m2, sub, float(epg)), axis=0, keepdims=True)
        gs = m1 + m2
        if best is None:
            best, e0, e1 = gs, i1, i2
        else:
            better = gs > best
            best = jnp.where(better, gs, best)
            e0 = jnp.where(better, float(g * epg) + i1, e0)
            e1 = jnp.where(better, float(g * epg) + i2, e1)
    eio = lax.broadcasted_iota(jnp.int32, scores.shape, 0).astype(F32)
    hit0 = eio == e0
    hit1 = eio == e1
    s0 = jnp.sum(jnp.where(hit0, scores, 0.0), axis=0, keepdims=True)
    s1 = jnp.sum(jnp.where(hit1, scores, 0.0), axis=0, keepdims=True)
    tot = s0 + s1
    return e0.astype(jnp.int32), e1.astype(jnp.int32), s0 / tot, s1 / tot, hit0, hit1


def _mix_out_kernel(y_ref, wo_ref, x_ref, moda_ref, modb_ref, lng_ref, lnb_ref, rwt_ref, rb_ref,
                    upper_ref, x1_ref, hm_ref, seli_ref, selw_ref, cnt_ref):
    d = D_MODEL
    gate = moda_ref[0, 0][:, 2 * d:]
    modb = modb_ref[0, 0]
    y = _dot(y_ref[0], wo_ref[...])
    x1 = _layer_norm(DN_ALPHA * x_ref[0] + (1.0 + gate) * y, lng_ref[...], lnb_ref[...])
    x1_ref[0] = x1
    hm = x1 * (1.0 + modb[:, d:2 * d]) + modb[:, :d]
    hm_ref[0] = hm.astype(BF16)

    logits_t = _dot_nt(rwt_ref[...], hm, precision=HIGHEST)
    e0, e1, w0, w1, hit0, hit1 = _route(logits_t, rb_ref[...])
    member = jnp.where(hit0 | hit1, 1.0, 0.0)
    before = _dot(member.astype(BF16), upper_ref[...])
    rank0 = jnp.sum(jnp.where(hit0, before, 0.0), axis=0, keepdims=True).astype(jnp.int32)
    rank1 = jnp.sum(jnp.where(hit1, before, 0.0), axis=0, keepdims=True).astype(jnp.int32)
    zi = jnp.zeros_like(e0)
    seli_ref[0] = jnp.concatenate([e0, e1, rank0, rank1, zi, zi, zi, zi], axis=0)
    zf = jnp.zeros_like(w0)
    selw_ref[0] = jnp.concatenate([w0, w1, zf, zf, zf, zf, zf, zf], axis=0)
    cnt_ref[0, 0] = jnp.broadcast_to(jnp.sum(member, axis=1, keepdims=True), (N_EXPERTS, LANES))


def mix_out(ymix, w_out, x, moda, modb, ln_g, ln_b, router_w, router_bias):
    bsz, s, d = x.shape
    tm = min(TOKEN_TILE, s)
    nt = s // tm
    r = np.arange(tm)
    upper = jnp.asarray((r[:, None] < r[None, :]).astype(np.float32), dtype=BF16)
    tok = lambda b, i: (b, i, 0)
    const = lambda b, i: (0, 0)
    modspec = pl.BlockSpec((1, 1, 1, 3 * d), lambda b, i: (b, 0, 0, 0))
    row8 = pl.BlockSpec((1, 8, tm), lambda b, i: (b, 0, i))
    return pl.pallas_call(
        _mix_out_kernel,
        grid=(bsz, nt),
        in_specs=[pl.BlockSpec((1, tm, d), tok),
                  pl.BlockSpec((d, d), const),
                  pl.BlockSpec((1, tm, d), tok),
                  modspec, modspec,
                  pl.BlockSpec((1, d), const), pl.BlockSpec((1, d), const),
                  pl.BlockSpec((N_EXPERTS, d), const),
                  pl.BlockSpec((N_EXPERTS, 1), const),
                  pl.BlockSpec((tm, tm), const)],
        out_specs=[pl.BlockSpec((1, tm, d), tok),
                   pl.BlockSpec((1, tm, d), tok),
                   row8, row8,
                   pl.BlockSpec((1, 1, N_EXPERTS, LANES), lambda b, i: (b, i, 0, 0))],
        out_shape=[jax.ShapeDtypeStruct((bsz, s, d), F32),
                   jax.ShapeDtypeStruct((bsz, s, d), BF16),
                   jax.ShapeDtypeStruct((bsz, 8, s), jnp.int32),
                   jax.ShapeDtypeStruct((bsz, 8, s), F32),
                   jax.ShapeDtypeStruct((bsz, nt, N_EXPERTS, LANES), F32)],
        compiler_params=_cparams("parallel", "parallel"),
        name="mix_out",
    )(ymix, w_out.astype(BF16), x, moda, modb, ln_g.reshape(1, d), ln_b.reshape(1, d),
      router_w.T.astype(F32), router_bias.astype(F32).reshape(N_EXPERTS, 1), upper)


def _experts_kernel(be_ref, nused_ref, xs_ref, wgu_ref, wd_ref, ys_ref):
    i = pl.program_id(0)

    @pl.when(i < nused_ref[0])
    def _():
        hgu = _dot(xs_ref[...], wgu_ref[0])
        act = (_silu(hgu[:, :D_EXPERT]) * hgu[:, D_EXPERT:]).astype(BF16)
        ys_ref[...] = _dot(act, wd_ref[0]).astype(BF16)

    @pl.when(i >= nused_ref[0])
    def _():
        ys_ref[...] = jnp.zeros_like(ys_ref)


def moe_experts(xs, block_expert, n_used, w_gate, w_up, w_down):
    rows, d = xs.shape
    nb = rows // MOE_BLOCK
    wgu = jnp.concatenate([w_gate, w_up], axis=-1).astype(BF16)
    wd = w_down.astype(BF16)
    return pl.pallas_call(
        _experts_kernel,
        grid_spec=pltpu.PrefetchScalarGridSpec(
            num_scalar_prefetch=2,
            grid=(nb,),
            in_specs=[pl.BlockSpec((MOE_BLOCK, d), lambda i, be, nu: (i, 0)),
                      pl.BlockSpec((1, d, 2 * D_EXPERT), lambda i, be, nu: (be[i], 0, 0)),
                      pl.BlockSpec((1, D_EXPERT, d), lambda i, be, nu: (be[i], 0, 0))],
            out_specs=pl.BlockSpec((MOE_BLOCK, d), lambda i, be, nu: (i, 0))),
        out_shape=jax.ShapeDtypeStruct((rows, d), BF16),
        compiler_params=_cparams("arbitrary"),
        name="moe_experts",
    )(block_expert, n_used, xs, wgu, wd)


def _moe_combine_kernel(g0_ref, g1_ref, wt_ref, x_ref, mod_ref, lng_ref, lnb_ref, o_ref):
    d = D_MODEL
    gate = mod_ref[0, 0][:, 2 * d:]
    wt = wt_ref[0]
    y = g0_ref[0].astype(F32) * wt[:, 0:1] + g1_ref[0].astype(F32) * wt[:, 1:2]
    o_ref[0] = _layer_norm(DN_ALPHA * x_ref[0] + (1.0 + gate) * y, lng_ref[...], lnb_ref[...])


def moe_combine(g0, g1, wt, x1, modb, ln_g, ln_b):
    bsz, s, d = x1.shape
    tm = min(TOKEN_TILE, s)
    tok = lambda b, i: (b, i, 0)
    const = lambda b, i: (0, 0)
    return pl.pallas_call(
        _moe_combine_kernel,
        grid=(bsz, s // tm),
        in_specs=[pl.BlockSpec((1, tm, d), tok), pl.BlockSpec((1, tm, d), tok),
                  pl.BlockSpec((1, tm, 8), tok), pl.BlockSpec((1, tm, d), tok),
                  pl.BlockSpec((1, 1, 1, 3 * d), lambda b, i: (b, 0, 0, 0)),
                  pl.BlockSpec((1, d), const), pl.BlockSpec((1, d), const)],
        out_specs=pl.BlockSpec((1, tm, d), tok),
        out_shape=jax.ShapeDtypeStruct((bsz, s, d), F32),
        compiler_params=_cparams("parallel", "parallel"),
        name="moe_combine",
    )(g0, g1, wt, x1, modb, ln_g.reshape(1, d), ln_b.reshape(1, d))


def moe_layer(x1, hm, seli, selw, cnt, modb, ln_g, ln_b, w_gate, w_up, w_down):
    bsz, s, d = x1.shape
    t = bsz * s
    tm = min(TOKEN_TILE, s)
    blk = MOE_BLOCK
    cnt = cnt[..., 0].reshape(-1, N_EXPERTS).astype(jnp.int32)
    tile_off = jnp.cumsum(cnt, axis=0) - cnt
    counts = cnt.sum(0)
    padded = (counts + blk - 1) // blk * blk
    pend = jnp.cumsum(padded)
    pstart = pend - padded
    nb = -(-(2 * t) // blk) + N_EXPERTS
    rows = nb * blk
    e01 = seli[:, 0:2, :]
    rank = seli[:, 2:4, :]
    tile_id = (jnp.arange(bsz)[:, None, None] * (s // tm) + jnp.arange(s)[None, None, :] // tm)
    base = (pstart[None, :] + tile_off).reshape(-1)
    dest = base[tile_id * N_EXPERTS + e01] + rank
    tok_id = jnp.broadcast_to(jnp.arange(t, dtype=jnp.int32).reshape(bsz, 1, s), dest.shape)
    slot_tok = jnp.zeros((rows,), jnp.int32).at[dest.reshape(-1)].set(
        tok_id.reshape(-1), unique_indices=True, mode="drop")
    block_expert = jnp.minimum(
        jnp.searchsorted(pend, jnp.arange(nb, dtype=jnp.int32) * blk, side="right"),
        N_EXPERTS - 1).astype(jnp.int32)
    n_used = (pend[-1] // blk).astype(jnp.int32).reshape(1)
    xs = jnp.take(hm.reshape(t, d), slot_tok, axis=0)
    ys = moe_experts(xs, block_expert, n_used, w_gate, w_up, w_down)
    g0 = jnp.take(ys, dest[:, 0, :].reshape(-1), axis=0).reshape(bsz, s, d)
    g1 = jnp.take(ys, dest[:, 1, :].reshape(-1), axis=0).reshape(bsz, s, d)
    wt = jnp.swapaxes(selw, 1, 2)
    return moe_combine(g0, g1, wt, x1, modb, ln_g, ln_b)


MLA_IN_COLS = Q_LORA + KV_LORA + 2 * LANES
MLA_QK = 2 * LANES


def _mla_in_kernel(x_ref, mod_ref, pos_ref, invf_ref, win_ref, qn_ref, kvn_ref, wq_ref, wkv_ref,
                   q_ref, k_ref, v_ref):
    d = D_MODEL
    nh = MLA_HEADS
    mod = mod_ref[0, 0]
    h = (x_ref[0] * (1.0 + mod[:, d:2 * d]) + mod[:, :d]).astype(BF16)
    proj = _dot(h, win_ref[...])
    ang = pos_ref[0].astype(F32) * invf_ref[...]
    lane = lax.broadcasted_iota(jnp.int32, ang.shape, 1)
    cos = jnp.where(lane < QK_ROPE, jnp.cos(ang), 0.0)
    sin = jnp.where(lane < QK_ROPE, jnp.sin(ang), 0.0)

    qa = proj[:, :Q_LORA]
    qa = (qa * lax.rsqrt(jnp.mean(qa * qa, -1, keepdims=True) + NORM_EPS) * qn_ref[...]).astype(BF16)
    kva = proj[:, Q_LORA:Q_LORA + KV_LORA]
    kva = (kva * lax.rsqrt(jnp.mean(kva * kva, -1, keepdims=True) + NORM_EPS) * kvn_ref[...]).astype(BF16)
    kr0 = Q_LORA + KV_LORA
    k_rope = (proj[:, kr0:kr0 + LANES] * cos + proj[:, kr0 + LANES:kr0 + 2 * LANES] * sin).astype(BF16)

    scale = (QK_NOPE + QK_ROPE) ** -0.5
    hw = nh * LANES
    q_nope = _dot(qa, wq_ref[:, :hw])
    q_rope = _dot(qa, wq_ref[:, hw:2 * hw])
    q_rot = _dot(qa, wq_ref[:, 2 * hw:])
    k_nope = _dot(kva, wkv_ref[:, :hw])
    v_ref[0] = _dot(kva, wkv_ref[:, hw:]).astype(BF16)
    for hh in range(nh):
        cols = slice(hh * LANES, (hh + 1) * LANES)
        q_ref[0, :, hh * MLA_QK:hh * MLA_QK + LANES] = (q_nope[:, cols] * scale).astype(BF16)
        q_ref[0, :, hh * MLA_QK + LANES:(hh + 1) * MLA_QK] = (
            (q_rope[:, cols] * cos + q_rot[:, cols] * sin) * scale).astype(BF16)
        k_ref[0, :, hh * MLA_QK:hh * MLA_QK + LANES] = k_nope[:, cols].astype(BF16)
        k_ref[0, :, hh * MLA_QK + LANES:(hh + 1) * MLA_QK] = k_rope


def _rope_cols(w):
    half = QK_ROPE // 2
    pad = [(0, 0)] * (w.ndim - 1) + [(0, LANES - QK_ROPE)]
    rot = jnp.concatenate([-w[..., half:], w[..., :half]], axis=-1)
    return jnp.pad(w, pad), jnp.pad(rot, pad)


def mla_in(x, mod, positions, w_in, q_a_norm, w_q_b, kv_a_norm, w_kv_b):
    bsz, s, d = x.shape
    tm = min(TOKEN_TILE, s)
    nh = MLA_HEADS
    kr, kr_rot = _rope_cols(w_in[:, Q_LORA + KV_LORA:])
    win = jnp.concatenate([w_in[:, :Q_LORA + KV_LORA], kr, kr_rot], axis=1).astype(BF16)
    wq = w_q_b.reshape(Q_LORA, nh, QK_NOPE + QK_ROPE)
    qr, qr_rot = _rope_cols(wq[..., QK_NOPE:])
    wq = jnp.concatenate([wq[..., :QK_NOPE].reshape(Q_LORA, -1), qr.reshape(Q_LORA, -1),
                          qr_rot.reshape(Q_LORA, -1)], axis=1).astype(BF16)
    wkv = w_kv_b.reshape(KV_LORA, nh, QK_NOPE + V_DIM)
    wkv = jnp.concatenate([wkv[..., :QK_NOPE].reshape(KV_LORA, -1),
                           wkv[..., QK_NOPE:].reshape(KV_LORA, -1)], axis=1).astype(BF16)
    inv = ROPE_THETA ** (-np.arange(0, QK_ROPE, 2, dtype=np.float32) / QK_ROPE)
    invf = np.zeros((1, LANES), np.float32)
    invf[0, :QK_ROPE] = np.concatenate([inv, inv])
    tok = lambda b, i: (b, i, 0)
    const = lambda b, i: (0, 0)
    return pl.pallas_call(
        _mla_in_kernel,
        grid=(bsz, s // tm),
        in_specs=[pl.BlockSpec((1, tm, d), tok),
                  pl.BlockSpec((1, 1, 1, 3 * d), lambda b, i: (b, 0, 0, 0)),
                  pl.BlockSpec((1, tm, 1), tok),
                  pl.BlockSpec((1, LANES), const),
                  pl.BlockSpec((d, MLA_IN_COLS), const),
                  pl.BlockSpec((1, Q_LORA), const),
                  pl.BlockSpec((1, KV_LORA), const),
                  pl.BlockSpec((Q_LORA, 3 * nh * LANES), const),
                  pl.BlockSpec((KV_LORA, 2 * nh * LANES), const)],
        out_specs=[pl.BlockSpec((1, tm, nh * MLA_QK), tok),
                   pl.BlockSpec((1, tm, nh * MLA_QK), tok),
                   pl.BlockSpec((1, tm, nh * V_DIM), tok)],
        out_shape=[jax.ShapeDtypeStruct((bsz, s, nh * MLA_QK), BF16),
                   jax.ShapeDtypeStruct((bsz, s, nh * MLA_QK), BF16),
                   jax.ShapeDtypeStruct((bsz, s, nh * V_DIM), BF16)],
        compiler_params=_cparams("parallel", "parallel"),
        name="mla_in",
    )(x, mod, positions.reshape(bsz, s, 1), jnp.asarray(invf), win,
      q_a_norm.astype(F32).reshape(1, -1), kv_a_norm.astype(F32).reshape(1, -1), wq, wkv)


def _attn_kernel(q_ref, k_ref, v_ref, o_ref):
    tq = q_ref.shape[1]
    tk = tq
    i = pl.program_id(2)
    q = q_ref[0]

    def step(j, carry, masked):
        m, l, acc = carry
        off = pl.multiple_of(j * tk, tk)
        sc = _dot_nt(q, k_ref[0, pl.ds(off, tk), :])
        if masked:
            qi = lax.broadcasted_iota(jnp.int32, sc.shape, 0)
            ki = lax.broadcasted_iota(jnp.int32, sc.shape, 1)
            sc = jnp.where(ki <= qi, sc, -jnp.inf)
        m_new = jnp.maximum(m, jnp.max(sc, -1, keepdims=True))
        alpha = jnp.exp(m - m_new)
        p = jnp.exp(sc - m_new)
        l = alpha * l + jnp.sum(p, -1, keepdims=True)
        acc = alpha * acc + _dot(p.astype(BF16), v_ref[0, pl.ds(off, tk), :])
        return m_new, l, acc

    init = (jnp.full((tq, 1), -jnp.inf, F32), jnp.zeros((tq, 1), F32), jnp.zeros((tq, V_DIM), F32))
    carry = lax.fori_loop(0, i, lambda j, c: step(j, c, False), init)
    _, l, acc = step(i, carry, True)
    o_ref[0] = (acc / l).astype(BF16)


def attention(q, k, v):
    bsz, s, _ = q.shape
    tq = min(ATTN_TILE, s)
    return pl.pallas_call(
        _attn_kernel,
        grid=(bsz, MLA_HEADS, s // tq),
        in_specs=[pl.BlockSpec((1, tq, MLA_QK), lambda b, h, i: (b, i, h)),
                  pl.BlockSpec((1, s, MLA_QK), lambda b, h, i: (b, 0, h)),
                  pl.BlockSpec((1, s, V_DIM), lambda b, h, i: (b, 0, h))],
        out_specs=pl.BlockSpec((1, tq, V_DIM), lambda b, h, i: (b, i, h)),
        out_shape=jax.ShapeDtypeStruct((bsz, s, MLA_HEADS * V_DIM), BF16),
        compiler_params=_cparams("parallel", "parallel", "arbitrary"),
        name="attention",
    )(q, k, v)


def kernel(x, c, positions, ada_w, ada_b, ln_g, ln_b, w_in_e, gdn_conv_w, gdn_a_log, gdn_dt_bias,
           gdn_norm_w, sc_conv_w, w_out_e, w_in_o, q_a_norm, w_q_b, kv_a_norm, w_kv_b, w_out_o,
           router_w, router_bias, w_gate, w_up, w_down):
    bsz, s, d = x.shape
    depth = ada_w.shape[0]
    mod = ada_mod(c, ada_w, ada_b).reshape(depth, 2, bsz, 1, 3 * d)
    for i in range(depth):
        j = i // 2
        moda, modb = mod[i, 0][:, None], mod[i, 1][:, None]
        if i % 2 == 0:
            o1, bg, bgt = even_in_proj(x, moda, w_in_e[j], gdn_a_log[j], gdn_dt_bias[j])
            qd, kd, u, w, ic, gl = gdn_prep(o1, bg, bgt, gdn_conv_w[j])
            o = gdn_scan(qd, kd, u, w, ic, gl)
            ymix = even_post(o, o1, gdn_norm_w[j], sc_conv_w[j])
            w_out = w_out_e[j]
        else:
            q, k, v = mla_in(x, moda, positions, w_in_o[j], q_a_norm[j], w_q_b[j], kv_a_norm[j], w_kv_b[j])
            ymix = attention(q, k, v)
            w_out = w_out_o[j]
        x1, hm, seli, selw, cnt = mix_out(ymix, w_out, x, moda, modb, ln_g[i, 0], ln_b[i, 0],
                                          router_w, router_bias)
        x = moe_layer(x1, hm, seli, selw, cnt, modb, ln_g[i, 1], ln_b[i, 1],
                      w_gate[i], w_up[i], w_down[i])
    return x
```

```python
import functools

import numpy as np
import jax
import jax.numpy as jnp
from jax import lax
from jax.experimental import pallas as pl
from jax.experimental.pallas import tpu as pltpu

F32 = jnp.float32
BF16 = jnp.bfloat16
HIGHEST = lax.Precision.HIGHEST

D_MODEL = 1024
DEPTH = 2
DN_ALPHA = (2.0 * DEPTH) ** 0.25

GDN_HEADS = 4
GDN_HEAD_DIM = 128
GDN_WIDTH = GDN_HEADS * GDN_HEAD_DIM
GDN_CONV = 4
GDN_CHUNK = 64
SC_WIDTH = 512
SC_CONV = 3

MLA_HEADS = 8
Q_LORA = 384
KV_LORA = 256
QK_NOPE = 128
QK_ROPE = 64
V_DIM = 128
ROPE_THETA = 10000.0

N_EXPERTS = 32
N_GROUPS = 4
EXPERTS_PER_GROUP = N_EXPERTS // N_GROUPS
D_EXPERT = 512

NORM_EPS = 1e-6
LN_EPS = 1e-5

LANES = 128
HALO = 16
TOKEN_TILE = 512
GDN_TILE = 256
ATTN_TILE = 1024
LOG2_E = 1.4426950408889634
MOE_BLOCK = 256
VMEM_LIMIT = 48 * 1024 * 1024


def _cparams(*sem):
    return pltpu.CompilerParams(dimension_semantics=sem, vmem_limit_bytes=VMEM_LIMIT)


def _sigmoid(x):
    return 1.0 / (1.0 + jnp.exp(-x))


def _silu(x):
    return x * _sigmoid(x)


def _dot(a, b):
    return jnp.dot(a, b, preferred_element_type=F32)


def _dot_nt(a, b, precision=None):
    return lax.dot_general(a, b, (((1,), (1,)), ((), ())), precision=precision,
                           preferred_element_type=F32)


def _dot_tn(a, b):
    return lax.dot_general(a, b, (((0,), (0,)), ((), ())), preferred_element_type=F32)


def _ada_kernel(c_ref, w_ref, b_ref, o_ref):
    cond = _silu(c_ref[...])
    o_ref[0] = jnp.dot(cond, w_ref[0], precision=HIGHEST, preferred_element_type=F32) + b_ref[0]


def ada_mod(c, ada_w, ada_b):
    nl = ada_w.shape[0] * ada_w.shape[1]
    bsz, d = c.shape
    w = ada_w.reshape(nl, d, 3 * d)
    b = ada_b.reshape(nl, 1, 3 * d)
    return pl.pallas_call(
        _ada_kernel,
        grid=(nl, 3),
        in_specs=[pl.BlockSpec((bsz, d), lambda l, j: (0, 0)),
                  pl.BlockSpec((1, d, d), lambda l, j: (l, 0, j)),
                  pl.BlockSpec((1, 1, d), lambda l, j: (l, 0, j))],
        out_specs=pl.BlockSpec((1, bsz, d), lambda l, j: (l, 0, j)),
        out_shape=jax.ShapeDtypeStruct((nl, bsz, 3 * d), F32),
        compiler_params=_cparams("parallel", "parallel"),
        name="ada_mod",
    )(c, w, b)


EVEN_MAIN = 3 * GDN_WIDTH + GDN_WIDTH + 3 * SC_WIDTH


def _even_in_kernel(x_ref, mod_ref, w1_ref, w2_ref, alog_ref, dtb_ref, o1_ref, o2_ref, o2t_ref):
    d = D_MODEL
    mod = mod_ref[0, 0]
    h = (x_ref[0] * (1.0 + mod[:, d:2 * d]) + mod[:, :d]).astype(BF16)
    for j in range(EVEN_MAIN // 512):
        o1_ref[0, :, j * 512:(j + 1) * 512] = _dot(h, w1_ref[:, j * 512:(j + 1) * 512]).astype(BF16)
    r = _dot(h, w2_ref[...])
    lane = lax.broadcasted_iota(jnp.int32, r.shape, 1)
    a = r + dtb_ref[...]
    softplus = jnp.maximum(a, 0.0) + jnp.log(1.0 + jnp.exp(-jnp.abs(a)))
    bg = jnp.where(lane < GDN_HEADS, _sigmoid(r), -jnp.exp(alog_ref[...]) * softplus)
    o2_ref[0] = bg
    o2t_ref[0] = bg.T[:8]


def even_in_proj(x, mod, w_in, a_log, dt_bias):
    bsz, s, d = x.shape
    tm = min(TOKEN_TILE, s)
    q_end = 4 * GDN_WIDTH
    w1 = jnp.concatenate([w_in[:, :q_end], w_in[:, q_end + 2 * GDN_HEADS:]], axis=1).astype(BF16)
    w2 = jnp.pad(w_in[:, q_end:q_end + 2 * GDN_HEADS], ((0, 0), (0, LANES - 2 * GDN_HEADS))).astype(BF16)
    alog = jnp.pad(a_log.astype(F32), (GDN_HEADS, LANES - 2 * GDN_HEADS)).reshape(1, LANES)
    dtb = jnp.pad(dt_bias.astype(F32), (GDN_HEADS, LANES - 2 * GDN_HEADS)).reshape(1, LANES)
    const = lambda b, i: (0, 0)
    return pl.pallas_call(
        _even_in_kernel,
        grid=(bsz, s // tm),
        in_specs=[pl.BlockSpec((1, tm, d), lambda b, i: (b, i, 0)),
                  pl.BlockSpec((1, 1, 1, 3 * d), lambda b, i: (b, 0, 0, 0)),
                  pl.BlockSpec((d, EVEN_MAIN), const),
                  pl.BlockSpec((d, LANES), const),
                  pl.BlockSpec((1, LANES), const),
                  pl.BlockSpec((1, LANES), const)],
        out_specs=[pl.BlockSpec((1, tm, EVEN_MAIN), lambda b, i: (b, i, 0)),
                   pl.BlockSpec((1, tm, LANES), lambda b, i: (b, i, 0)),
                   pl.BlockSpec((1, 8, tm), lambda b, i: (b, 0, i))],
        out_shape=[jax.ShapeDtypeStruct((bsz, s, EVEN_MAIN), BF16),
                   jax.ShapeDtypeStruct((bsz, s, LANES), F32),
                   jax.ShapeDtypeStruct((bsz, 8, s), F32)],
        compiler_params=_cparams("parallel", "parallel"),
        name="even_in_proj",
    )(x, mod, w1, w2, alog, dtb)


def _causal_conv(x, halo, w, taps):
    rows = x.shape[0]
    xf = jnp.concatenate([halo, x], axis=0)
    y = w[taps - 1:taps] * x
    for j in range(taps - 1):
        off = HALO - (taps - 1) + j
        y = y + w[j:j + 1] * xf[off:off + rows]
    return y


def _inv_unit_lower(low, xor_ij, block):
    eye = jnp.where(xor_ij == 0, 1.0, 0.0)
    m = eye - jnp.where(xor_ij == 1, low, 0.0)
    s = 2
    while s < block:
        shift = s.bit_length() - 1
        cs = jnp.where((xor_ij >> shift) == 1, low, 0.0).astype(BF16)
        mb = m.astype(BF16)
        m = m - _dot(mb, _dot(cs, mb).astype(BF16))
        s *= 2
    return m


def _gdn_prep_kernel(qkv_ref, halo_ref, cw_ref, bg_ref, bgt_ref, ltri_ref, utri_ref, same_ref,
                     qd_ref, kd_ref, u_ref, w_ref, ic_ref, gl_ref):
    ts = qkv_ref.shape[1]
    c = GDN_CHUNK
    hd = GDN_HEAD_DIM
    i = pl.program_id(1)
    x = qkv_ref[0].astype(F32)
    halo = jnp.where(i > 0, halo_ref[0].astype(F32), 0.0)
    y = _silu(_causal_conv(x, halo, cw_ref[...], GDN_CONV))

    bg = bg_ref[0]
    gc_col = jnp.dot(ltri_ref[...], bg, precision=HIGHEST, preferred_element_type=F32)
    gc_row = jnp.dot(bgt_ref[0], utri_ref[...], precision=HIGHEST, preferred_element_type=F32)
    gc_end = jnp.dot(same_ref[...], bg, precision=HIGHEST, preferred_element_type=F32)

    ii = lax.broadcasted_iota(jnp.int32, (ts, ts), 0)
    jj = lax.broadcasted_iota(jnp.int32, (ts, ts), 1)
    xor_ij = ii ^ jj
    causal = (same_ref[...] > 0.0) & (ii >= jj)
    diag = xor_ij == 0

    for h in range(GDN_HEADS):
        q = y[:, h * hd:(h + 1) * hd]
        k = y[:, GDN_WIDTH + h * hd:GDN_WIDTH + (h + 1) * hd]
        v = y[:, 2 * GDN_WIDTH + h * hd:2 * GDN_WIDTH + (h + 1) * hd]
        q = q * lax.rsqrt(jnp.sum(q * q, -1, keepdims=True) + NORM_EPS) * (hd ** -0.5)
        k = k * lax.rsqrt(jnp.sum(k * k, -1, keepdims=True) + NORM_EPS)
        beta = bg[:, h:h + 1]
        gcol = gc_col[:, GDN_HEADS + h:GDN_HEADS + h + 1]
        grow = gc_row[GDN_HEADS + h:GDN_HEADS + h + 1, :]
        gend = gc_end[:, GDN_HEADS + h:GDN_HEADS + h + 1]
        decay = jnp.exp(jnp.where(causal, gcol - grow, -jnp.inf))
        kb = k * beta
        kbf = k.astype(BF16)
        low = jnp.where(diag, 0.0, _dot_nt(kb.astype(BF16), kbf) * decay)
        intra = _dot_nt(q.astype(BF16), kbf) * decay
        egc = jnp.exp(gcol)
        rhs = jnp.concatenate([v * beta, kb * egc], axis=-1).astype(BF16)
        sol = _dot(_inv_unit_lower(low, xor_ij, c).astype(BF16), rhs)
        cols = slice(h * hd, (h + 1) * hd)
        u_ref[0, :, cols] = sol[:, :hd]
        w_ref[0, :, cols] = sol[:, hd:].astype(BF16)
        qd_ref[0, :, cols] = (q * egc).astype(BF16)
        kd_ref[0, :, cols] = (k * jnp.exp(gend - gcol)).astype(BF16)
        packed = intra[:, :c]
        for n in range(1, ts // c):
            packed = packed + intra[:, n * c:(n + 1) * c]
        ic_ref[0, :, h * c:(h + 1) * c] = packed.astype(BF16)
        for n in range(ts // c):
            gl_ref[0, n, h:h + 1, :] = jnp.broadcast_to(jnp.exp(gend[n * c:n * c + 1]), (1, hd))


def gdn_prep(o1, bg, bgt, conv_w):
    bsz, s, _ = o1.shape
    ts = min(GDN_TILE, s)
    c = GDN_CHUNK
    nc = ts // c
    qkv_w = 3 * GDN_WIDTH
    r = np.arange(ts)
    same = (r[:, None] // c) == (r[None, :] // c)
    ltri = jnp.asarray((same & (r[:, None] >= r[None, :])).astype(np.float32))
    utri = jnp.asarray((same & (r[:, None] <= r[None, :])).astype(np.float32))
    same = jnp.asarray(same.astype(np.float32))
    hb = ts // HALO
    tok = lambda b, i: (b, i, 0)
    const = lambda b, i: (0, 0)
    wide = jax.ShapeDtypeStruct((bsz, s, GDN_WIDTH), BF16)
    return pl.pallas_call(
        _gdn_prep_kernel,
        grid=(bsz, s // ts),
        in_specs=[pl.BlockSpec((1, ts, qkv_w), tok),
                  pl.BlockSpec((1, HALO, qkv_w), lambda b, i: (b, jnp.maximum(i * hb - 1, 0), 0)),
                  pl.BlockSpec((GDN_CONV, qkv_w), const),
                  pl.BlockSpec((1, ts, LANES), tok),
                  pl.BlockSpec((1, 8, ts), lambda b, i: (b, 0, i)),
                  pl.BlockSpec((ts, ts), const),
                  pl.BlockSpec((ts, ts), const),
                  pl.BlockSpec((ts, ts), const)],
        out_specs=[pl.BlockSpec((1, ts, GDN_WIDTH), tok),
                   pl.BlockSpec((1, ts, GDN_WIDTH), tok),
                   pl.BlockSpec((1, ts, GDN_WIDTH), tok),
                   pl.BlockSpec((1, ts, GDN_WIDTH), tok),
                   pl.BlockSpec((1, ts, GDN_HEADS * c), tok),
                   pl.BlockSpec((1, nc, GDN_HEADS, GDN_HEAD_DIM), lambda b, i: (b, i, 0, 0))],
        out_shape=[wide, wide, jax.ShapeDtypeStruct((bsz, s, GDN_WIDTH), F32), wide,
                   jax.ShapeDtypeStruct((bsz, s, GDN_HEADS * c), BF16),
                   jax.ShapeDtypeStruct((bsz, s // c, GDN_HEADS, GDN_HEAD_DIM), F32)],
        compiler_params=_cparams("parallel", "parallel"),
        name="gdn_prep",
    )(o1, o1, conv_w.astype(F32), bg, bgt, ltri, utri, same)


def _gdn_scan_kernel(qd_ref, kd_ref, u_ref, w_ref, ic_ref, gl_ref, o_ref, state_ref):
    ts = qd_ref.shape[1]
    c = GDN_CHUNK
    hd = GDN_HEAD_DIM

    @pl.when(pl.program_id(1) == 0)
    def _():
        state_ref[...] = jnp.zeros_like(state_ref)

    for n in range(ts // c):
        rows = slice(n * c, (n + 1) * c)
        for h in range(GDN_HEADS):
            cols = slice(h * hd, (h + 1) * hd)
            st = state_ref[h]
            stb = st.astype(BF16)
            v_new = u_ref[0, rows, cols] - _dot(w_ref[0, rows, cols], stb)
            vb = v_new.astype(BF16)
            o_ref[0, rows, cols] = _dot(qd_ref[0, rows, cols], stb) + _dot(ic_ref[0, rows, h * c:(h + 1) * c], vb)
            state_ref[h] = st * gl_ref[0, n, h:h + 1, :] + _dot_tn(kd_ref[0, rows, cols], vb)


def gdn_scan(qd, kd, u, w, ic, gl):
    bsz, s, _ = qd.shape
    ts = min(GDN_TILE, s)
    c = GDN_CHUNK
    tok = lambda b, i: (b, i, 0)
    return pl.pallas_call(
        _gdn_scan_kernel,
        grid=(bsz, s // ts),
        in_specs=[pl.BlockSpec((1, ts, GDN_WIDTH), tok)] * 4 + [
            pl.BlockSpec((1, ts, GDN_HEADS * c), tok),
            pl.BlockSpec((1, ts // c, GDN_HEADS, GDN_HEAD_DIM), lambda b, i: (b, i, 0, 0))],
        out_specs=pl.BlockSpec((1, ts, GDN_WIDTH), tok),
        out_shape=jax.ShapeDtypeStruct((bsz, s, GDN_WIDTH), F32),
        scratch_shapes=[pltpu.VMEM((GDN_HEADS, GDN_HEAD_DIM, GDN_HEAD_DIM), F32)],
        compiler_params=_cparams("parallel", "arbitrary"),
        name="gdn_scan",
    )(qd, kd, u, w, ic, gl)


def _even_post_kernel(o_ref, z_ref, scb_ref, scc_ref, sch_ref, scc_halo_ref, sch_halo_ref,
                      nw_ref, cw_ref, y_ref):
    hd = GDN_HEAD_DIM
    i = pl.program_id(1)
    o = o_ref[0]
    z = z_ref[0].astype(F32)
    nw = nw_ref[...]
    for h in range(GDN_HEADS):
        cols = slice(h * hd, (h + 1) * hd)
        oh = o[:, cols]
        on = oh * lax.rsqrt(jnp.mean(oh * oh, -1, keepdims=True) + NORM_EPS) * nw
        y_ref[0, :, cols] = (on * _silu(z[:, cols])).astype(BF16)
    ch = scc_ref[0].astype(F32) * sch_ref[0].astype(F32)
    ch_halo = jnp.where(i > 0, scc_halo_ref[0].astype(F32) * sch_halo_ref[0].astype(F32), 0.0)
    yb = scb_ref[0].astype(F32) * _causal_conv(ch, ch_halo, cw_ref[...], SC_CONV)
    y_ref[0, :, GDN_WIDTH:] = yb.astype(BF16)


def even_post(o, o1, norm_w, sc_conv_w):
    bsz, s, _ = o.shape
    tm = min(TOKEN_TILE, s)
    hb = tm // HALO
    wb = 512
    blk = lambda j: pl.BlockSpec((1, tm, wb), lambda b, i, j=j: (b, i, j))
    halo = lambda j: pl.BlockSpec((1, HALO, wb), lambda b, i, j=j: (b, jnp.maximum(i * hb - 1, 0), j))
    const = lambda b, i: (0, 0)
    return pl.pallas_call(
        _even_post_kernel,
        grid=(bsz, s // tm),
        in_specs=[pl.BlockSpec((1, tm, GDN_WIDTH), lambda b, i: (b, i, 0)),
                  blk(3), blk(4), blk(5), blk(6), halo(5), halo(6),
                  pl.BlockSpec((1, GDN_HEAD_DIM), const),
                  pl.BlockSpec((SC_CONV, SC_WIDTH), const)],
        out_specs=pl.BlockSpec((1, tm, GDN_WIDTH + SC_WIDTH), lambda b, i: (b, i, 0)),
        out_shape=jax.ShapeDtypeStruct((bsz, s, GDN_WIDTH + SC_WIDTH), BF16),
        compiler_params=_cparams("parallel", "parallel"),
        name="even_post",
    )(o, o1, o1, o1, o1, o1, o1, norm_w.astype(F32).reshape(1, -1), sc_conv_w.astype(F32))


def _layer_norm(r, g, b):
    mu = jnp.mean(r, -1, keepdims=True)
    rc = r - mu
    var = jnp.mean(rc * rc, -1, keepdims=True)
    return rc * lax.rsqrt(var + LN_EPS) * g + b


def _route(logits_t, bias):
    scores = _sigmoid(logits_t)
    biased = scores + bias
    t = logits_t.shape[1]
    epg = EXPERTS_PER_GROUP
    sub = lax.broadcasted_iota(jnp.int32, (epg, t), 0).astype(F32)
    best = None
    for g in range(N_GROUPS):
        bgp = biased[g * epg:(g + 1) * epg]
        m1 = jnp.max(bgp, axis=0, keepdims=True)
        i1 = jnp.min(jnp.where(bgp == m1, sub, float(epg)), axis=0, keepdims=True)
        rest = jnp.where(sub == i1, -jnp.inf, bgp)
        m2 = jnp.max(rest, axis=0, keepdims=True)
        i2 = jnp.min(jnp.where(rest == m2, sub, float(epg)), axis=0, keepdims=True)
        gs = m1 + m2
        if best is None:
            best, e0, e1 = gs, i1, i2
        else:
            better = gs > best
            best = jnp.where(better, gs, best)
            e0 = jnp.where(better, float(g * epg) + i1, e0)
            e1 = jnp.where(better, float(g * epg) + i2, e1)
    eio = lax.broadcasted_iota(jnp.int32, scores.shape, 0).astype(F32)
    hit0 = eio == e0
    hit1 = eio == e1
    s0 = jnp.sum(jnp.where(hit0, scores, 0.0), axis=0, keepdims=True)
    s1 = jnp.sum(jnp.where(hit1, scores, 0.0), axis=0, keepdims=True)
    tot = s0 + s1
    return e0.astype(jnp.int32), e1.astype(jnp.int32), s0 / tot, s1 / tot, hit0, hit1


def _mix_out_kernel(y_ref, wo_ref, x_ref, moda_ref, modb_ref, lng_ref, lnb_ref, rwt_ref, rb_ref,
                    upper_ref, x1_ref, hm_ref, seli_ref, selw_ref, cnt_ref):
    d = D_MODEL
    gate = moda_ref[0, 0][:, 2 * d:]
    modb = modb_ref[0, 0]
    y = _dot(y_ref[0], wo_ref[...])
    x1 = _layer_norm(DN_ALPHA * x_ref[0] + (1.0 + gate) * y, lng_ref[...], lnb_ref[...])
    x1_ref[0] = x1
    hm = x1 * (1.0 + modb[:, d:2 * d]) + modb[:, :d]
    hm_ref[0] = hm.astype(BF16)

    logits_t = _dot_nt(rwt_ref[...], hm, precision=HIGHEST)
    e0, e1, w0, w1, hit0, hit1 = _route(logits_t, rb_ref[...])
    member = jnp.where(hit0 | hit1, 1.0, 0.0)
    before = _dot(member.astype(BF16), upper_ref[...])
    rank0 = jnp.sum(jnp.where(hit0, before, 0.0), axis=0, keepdims=True).astype(jnp.int32)
    rank1 = jnp.sum(jnp.where(hit1, before, 0.0), axis=0, keepdims=True).astype(jnp.int32)
    zi = jnp.zeros_like(e0)
    seli_ref[0] = jnp.concatenate([e0, e1, rank0, rank1, zi, zi, zi, zi], axis=0)
    zf = jnp.zeros_like(w0)
    selw_ref[0] = jnp.concatenate([w0, w1, zf, zf, zf, zf, zf, zf], axis=0)
    cnt_ref[0, 0] = jnp.broadcast_to(jnp.sum(member, axis=1, keepdims=True), (N_EXPERTS, LANES))


def mix_out(ymix, w_out, x, moda, modb, ln_g, ln_b, router_w, router_bias):
    bsz, s, d = x.shape
    tm = min(TOKEN_TILE, s)
    nt = s // tm
    r = np.arange(tm)
    upper = jnp.asarray((r[:, None] < r[None, :]).astype(np.float32), dtype=BF16)
    tok = lambda b, i: (b, i, 0)
    const = lambda b, i: (0, 0)
    modspec = pl.BlockSpec((1, 1, 1, 3 * d), lambda b, i: (b, 0, 0, 0))
    row8 = pl.BlockSpec((1, 8, tm), lambda b, i: (b, 0, i))
    return pl.pallas_call(
        _mix_out_kernel,
        grid=(bsz, nt),
        in_specs=[pl.BlockSpec((1, tm, d), tok),
                  pl.BlockSpec((d, d), const),
                  pl.BlockSpec((1, tm, d), tok),
                  modspec, modspec,
                  pl.BlockSpec((1, d), const), pl.BlockSpec((1, d), const),
                  pl.BlockSpec((N_EXPERTS, d), const),
                  pl.BlockSpec((N_EXPERTS, 1), const),
                  pl.BlockSpec((tm, tm), const)],
        out_specs=[pl.BlockSpec((1, tm, d), tok),
                   pl.BlockSpec((1, tm, d), tok),
                   row8, row8,
                   pl.BlockSpec((1, 1, N_EXPERTS, LANES), lambda b, i: (b, i, 0, 0))],
        out_shape=[jax.ShapeDtypeStruct((bsz, s, d), F32),
                   jax.ShapeDtypeStruct((bsz, s, d), BF16),
                   jax.ShapeDtypeStruct((bsz, 8, s), jnp.int32),
                   jax.ShapeDtypeStruct((bsz, 8, s), F32),
                   jax.ShapeDtypeStruct((bsz, nt, N_EXPERTS, LANES), F32)],
        compiler_params=_cparams("parallel", "parallel"),
        name="mix_out",
    )(ymix, w_out.astype(BF16), x, moda, modb, ln_g.reshape(1, d), ln_b.reshape(1, d),
      router_w.T.astype(F32), router_bias.astype(F32).reshape(N_EXPERTS, 1), upper)


def _experts_kernel(be_ref, nused_ref, xs_ref, wg_ref, wu_ref, wd_ref, ys_ref, wgb_ref, wub_ref, wdb_ref):
    i = pl.program_id(0)

    @pl.when((i == 0) | (be_ref[i] != be_ref[jnp.maximum(i - 1, 0)]))
    def _():
        wgb_ref[...] = wg_ref[0].astype(BF16)
        wub_ref[...] = wu_ref[0].astype(BF16)
        wdb_ref[...] = wd_ref[0].astype(BF16)

    @pl.when(i < nused_ref[0])
    def _():
        x = xs_ref[...]
        act = (_silu(_dot(x, wgb_ref[...])) * _dot(x, wub_ref[...])).astype(BF16)
        ys_ref[...] = _dot(act, wdb_ref[...]).astype(BF16)

    @pl.when(i >= nused_ref[0])
    def _():
        ys_ref[...] = jnp.zeros_like(ys_ref)


def moe_experts(xs, block_expert, n_used, w_gate, w_up, w_down):
    rows, d = xs.shape
    nb = rows // MOE_BLOCK
    by_expert = lambda i, be, nu: (be[i], 0, 0)
    return pl.pallas_call(
        _experts_kernel,
        grid_spec=pltpu.PrefetchScalarGridSpec(
            num_scalar_prefetch=2,
            grid=(nb,),
            in_specs=[pl.BlockSpec((MOE_BLOCK, d), lambda i, be, nu: (i, 0)),
                      pl.BlockSpec((1, d, D_EXPERT), by_expert),
                      pl.BlockSpec((1, d, D_EXPERT), by_expert),
                      pl.BlockSpec((1, D_EXPERT, d), by_expert)],
            out_specs=pl.BlockSpec((MOE_BLOCK, d), lambda i, be, nu: (i, 0)),
            scratch_shapes=[pltpu.VMEM((d, D_EXPERT), BF16), pltpu.VMEM((d, D_EXPERT), BF16),
                            pltpu.VMEM((D_EXPERT, d), BF16)]),
        out_shape=jax.ShapeDtypeStruct((rows, d), BF16),
        compiler_params=_cparams("arbitrary"),
        name="moe_experts",
    )(block_expert, n_used, xs, w_gate, w_up, w_down)


def _plan_kernel(seli_ref, base_ref, dest_ref):
    sel = seli_ref[0]
    base = base_ref[0, 0]
    eio = lax.broadcasted_iota(jnp.int32, (N_EXPERTS, sel.shape[1]), 0)
    out = []
    for kk in range(2):
        first = jnp.sum(jnp.where(eio == sel[kk:kk + 1], base, 0.0), axis=0, keepdims=True)
        out.append(first.astype(jnp.int32) + sel[2 + kk:3 + kk])
    zero = jnp.zeros_like(out[0])
    dest_ref[0] = jnp.concatenate(out + [zero] * 6, axis=0)


def dispatch_plan(seli, base):
    bsz, _, s = seli.shape
    nt = base.shape[1]
    tm = s // nt
    return pl.pallas_call(
        _plan_kernel,
        grid=(bsz, nt),
        in_specs=[pl.BlockSpec((1, 8, tm), lambda b, i: (b, 0, i)),
                  pl.BlockSpec((1, 1, N_EXPERTS, 1), lambda b, i: (b, i, 0, 0))],
        out_specs=pl.BlockSpec((1, 8, tm), lambda b, i: (b, 0, i)),
        out_shape=jax.ShapeDtypeStruct((bsz, 8, s), jnp.int32),
        compiler_params=_cparams("parallel", "parallel"),
        name="dispatch_plan",
    )(seli, base)


def _invert_kernel(dest_ref, slot_tok_ref):
    t = dest_ref.shape[0] // 2

    def clear(r, carry):
        slot_tok_ref[r] = 0
        return carry

    def put(a, carry):
        slot_tok_ref[dest_ref[a]] = a
        slot_tok_ref[dest_ref[t + a]] = a
        return carry

    lax.fori_loop(0, slot_tok_ref.shape[0], clear, 0, unroll=8)
    lax.fori_loop(0, t, put, 0, unroll=8)


def invert_slots(dest_flat, rows):
    return pl.pallas_call(
        _invert_kernel,
        in_specs=[pl.BlockSpec(memory_space=pltpu.SMEM)],
        out_specs=pl.BlockSpec(memory_space=pltpu.SMEM),
        out_shape=jax.ShapeDtypeStruct((rows,), jnp.int32),
        name="invert_slots",
    )(dest_flat)


def _moe_combine_kernel(g0_ref, g1_ref, wt_ref, x_ref, mod_ref, lng_ref, lnb_ref, o_ref):
    d = D_MODEL
    gate = mod_ref[0, 0][:, 2 * d:]
    wt = wt_ref[0]
    y = g0_ref[0].astype(F32) * wt[:, 0:1] + g1_ref[0].astype(F32) * wt[:, 1:2]
    o_ref[0] = _layer_norm(DN_ALPHA * x_ref[0] + (1.0 + gate) * y, lng_ref[...], lnb_ref[...])


def moe_combine(g0, g1, wt, x1, modb, ln_g, ln_b):
    bsz, s, d = x1.shape
    tm = min(TOKEN_TILE, s)
    tok = lambda b, i: (b, i, 0)
    const = lambda b, i: (0, 0)
    return pl.pallas_call(
        _moe_combine_kernel,
        grid=(bsz, s // tm),
        in_specs=[pl.BlockSpec((1, tm, d), tok), pl.BlockSpec((1, tm, d), tok),
                  pl.BlockSpec((1, tm, 8), tok), pl.BlockSpec((1, tm, d), tok),
                  pl.BlockSpec((1, 1, 1, 3 * d), lambda b, i: (b, 0, 0, 0)),
                  pl.BlockSpec((1, d), const), pl.BlockSpec((1, d), const)],
        out_specs=pl.BlockSpec((1, tm, d), tok),
        out_shape=jax.ShapeDtypeStruct((bsz, s, d), F32),
        compiler_params=_cparams("parallel", "parallel"),
        name="moe_combine",
    )(g0, g1, wt, x1, modb, ln_g.reshape(1, d), ln_b.reshape(1, d))


def moe_layer(x1, hm, seli, selw, cnt, modb, ln_g, ln_b, w_gate, w_up, w_down):
    bsz, s, d = x1.shape
    t = bsz * s
    tm = min(TOKEN_TILE, s)
    blk = MOE_BLOCK
    cnt = cnt[..., 0].reshape(-1, N_EXPERTS).astype(jnp.int32)
    tile_off = jnp.cumsum(cnt, axis=0) - cnt
    counts = cnt.sum(0)
    padded = (counts + blk - 1) // blk * blk
    pend = jnp.cumsum(padded)
    pstart = pend - padded
    nb = -(-(2 * t) // blk) + N_EXPERTS
    rows = nb * blk
    base = (pstart[None, :] + tile_off).astype(F32).reshape(bsz, s // tm, N_EXPERTS, 1)
    dest = dispatch_plan(seli, base)
    dest = jnp.swapaxes(dest[:, 0:2, :], 0, 1).reshape(2 * t)
    slot_tok = invert_slots(dest, rows)
    starts = jnp.arange(nb, dtype=jnp.int32) * blk
    block_expert = jnp.minimum(jnp.sum((pend[None, :] <= starts[:, None]).astype(jnp.int32), axis=1),
                               N_EXPERTS - 1)
    n_used = (pend[-1] // blk).astype(jnp.int32).reshape(1)
    xs = jnp.take(hm.reshape(t, d), slot_tok, axis=0)
    ys = moe_experts(xs, block_expert, n_used, w_gate, w_up, w_down)
    g0 = jnp.take(ys, dest[:t], axis=0).reshape(bsz, s, d)
    g1 = jnp.take(ys, dest[t:], axis=0).reshape(bsz, s, d)
    wt = jnp.swapaxes(selw, 1, 2)
    return moe_combine(g0, g1, wt, x1, modb, ln_g, ln_b)


MLA_IN_COLS = Q_LORA + KV_LORA + 2 * LANES
MLA_QK = 2 * LANES


def _mla_in_kernel(x_ref, mod_ref, pos_ref, invf_ref, win_ref, qn_ref, kvn_ref, wq_ref, wkv_ref,
                   q_ref, k_ref, v_ref):
    d = D_MODEL
    nh = MLA_HEADS
    mod = mod_ref[0, 0]
    h = (x_ref[0] * (1.0 + mod[:, d:2 * d]) + mod[:, :d]).astype(BF16)
    proj = _dot(h, win_ref[...])
    ang = pos_ref[0].astype(F32) * invf_ref[...]
    lane = lax.broadcasted_iota(jnp.int32, ang.shape, 1)
    cos = jnp.where(lane < QK_ROPE, jnp.cos(ang), 0.0)
    sin = jnp.where(lane < QK_ROPE, jnp.sin(ang), 0.0)

    qa = proj[:, :Q_LORA]
    qa = (qa * lax.rsqrt(jnp.mean(qa * qa, -1, keepdims=True) + NORM_EPS) * qn_ref[...]).astype(BF16)
    kva = proj[:, Q_LORA:Q_LORA + KV_LORA]
    kva = (kva * lax.rsqrt(jnp.mean(kva * kva, -1, keepdims=True) + NORM_EPS) * kvn_ref[...]).astype(BF16)
    kr0 = Q_LORA + KV_LORA
    k_rope = (proj[:, kr0:kr0 + LANES] * cos + proj[:, kr0 + LANES:kr0 + 2 * LANES] * sin).astype(BF16)

    scale = (QK_NOPE + QK_ROPE) ** -0.5 * LOG2_E
    hw = nh * LANES
    q_nope = _dot(qa, wq_ref[:, :hw])
    q_rope = _dot(qa, wq_ref[:, hw:2 * hw])
    q_rot = _dot(qa, wq_ref[:, 2 * hw:])
    k_nope = _dot(kva, wkv_ref[:, :hw])
    v_ref[0] = _dot(kva, wkv_ref[:, hw:]).astype(BF16)
    for hh in range(nh):
        cols = slice(hh * LANES, (hh + 1) * LANES)
        q_ref[0, :, hh * MLA_QK:hh * MLA_QK + LANES] = (q_nope[:, cols] * scale).astype(BF16)
        q_ref[0, :, hh * MLA_QK + LANES:(hh + 1) * MLA_QK] = (
            (q_rope[:, cols] * cos + q_rot[:, cols] * sin) * scale).astype(BF16)
        k_ref[0, :, hh * MLA_QK:hh * MLA_QK + LANES] = k_nope[:, cols].astype(BF16)
        k_ref[0, :, hh * MLA_QK + LANES:(hh + 1) * MLA_QK] = k_rope


def _rope_cols(w):
    half = QK_ROPE // 2
    pad = [(0, 0)] * (w.ndim - 1) + [(0, LANES - QK_ROPE)]
    rot = jnp.concatenate([-w[..., half:], w[..., :half]], axis=-1)
    return jnp.pad(w, pad), jnp.pad(rot, pad)


def mla_in(x, mod, positions, w_in, q_a_norm, w_q_b, kv_a_norm, w_kv_b):
    bsz, s, d = x.shape
    tm = min(TOKEN_TILE, s)
    nh = MLA_HEADS
    kr, kr_rot = _rope_cols(w_in[:, Q_LORA + KV_LORA:])
    win = jnp.concatenate([w_in[:, :Q_LORA + KV_LORA], kr, kr_rot], axis=1).astype(BF16)
    wq = w_q_b.reshape(Q_LORA, nh, QK_NOPE + QK_ROPE)
    qr, qr_rot = _rope_cols(wq[..., QK_NOPE:])
    wq = jnp.concatenate([wq[..., :QK_NOPE].reshape(Q_LORA, -1), qr.reshape(Q_LORA, -1),
                          qr_rot.reshape(Q_LORA, -1)], axis=1).astype(BF16)
    wkv = w_kv_b.reshape(KV_LORA, nh, QK_NOPE + V_DIM)
    wkv = jnp.concatenate([wkv[..., :QK_NOPE].reshape(KV_LORA, -1),
                           wkv[..., QK_NOPE:].reshape(KV_LORA, -1)], axis=1).astype(BF16)
    inv = ROPE_THETA ** (-np.arange(0, QK_ROPE, 2, dtype=np.float32) / QK_ROPE)
    invf = np.zeros((1, LANES), np.float32)
    invf[0, :QK_ROPE] = np.concatenate([inv, inv])
    tok = lambda b, i: (b, i, 0)
    const = lambda b, i: (0, 0)
    return pl.pallas_call(
        _mla_in_kernel,
        grid=(bsz, s // tm),
        in_specs=[pl.BlockSpec((1, tm, d), tok),
                  pl.BlockSpec((1, 1, 1, 3 * d), lambda b, i: (b, 0, 0, 0)),
                  pl.BlockSpec((1, tm, 1), tok),
                  pl.BlockSpec((1, LANES), const),
                  pl.BlockSpec((d, MLA_IN_COLS), const),
                  pl.BlockSpec((1, Q_LORA), const),
                  pl.BlockSpec((1, KV_LORA), const),
                  pl.BlockSpec((Q_LORA, 3 * nh * LANES), const),
                  pl.BlockSpec((KV_LORA, 2 * nh * LANES), const)],
        out_specs=[pl.BlockSpec((1, tm, nh * MLA_QK), tok),
                   pl.BlockSpec((1, tm, nh * MLA_QK), tok),
                   pl.BlockSpec((1, tm, nh * V_DIM), tok)],
        out_shape=[jax.ShapeDtypeStruct((bsz, s, nh * MLA_QK), BF16),
                   jax.ShapeDtypeStruct((bsz, s, nh * MLA_QK), BF16),
                   jax.ShapeDtypeStruct((bsz, s, nh * V_DIM), BF16)],
        compiler_params=_cparams("parallel", "parallel"),
        name="mla_in",
    )(x, mod, positions.reshape(bsz, s, 1), jnp.asarray(invf), win,
      q_a_norm.astype(F32).reshape(1, -1), kv_a_norm.astype(F32).reshape(1, -1), wq, wkv)


def _attn_kernel(q_ref, k_ref, v_ref, o_ref, m_ref, l_ref, acc_ref):
    tq = q_ref.shape[1]
    half = tq // 2
    i = pl.program_id(2)
    m_ref[...] = jnp.full(m_ref.shape, -jnp.inf, F32)
    l_ref[...] = jnp.zeros(l_ref.shape, F32)
    acc_ref[...] = jnp.zeros(acc_ref.shape, F32)

    def update(r0, nrows, off, width, masked):
        rows = slice(r0, r0 + nrows)
        sc = _dot_nt(q_ref[0, rows, :], k_ref[0, pl.ds(off, width), :])
        if masked:
            qi = lax.broadcasted_iota(jnp.int32, sc.shape, 0)
            ki = lax.broadcasted_iota(jnp.int32, sc.shape, 1)
            sc = jnp.where(ki <= qi, sc, -jnp.inf)
        m_old = m_ref[rows]
        m_new = jnp.maximum(m_old, jnp.max(sc, -1, keepdims=True))
        alpha = jnp.exp2(m_old - m_new)
        p = jnp.exp2(sc - m_new)
        l_ref[rows] = alpha * l_ref[rows] + jnp.sum(p, -1, keepdims=True)
        acc_ref[rows] = alpha * acc_ref[rows] + _dot(p.astype(BF16), v_ref[0, pl.ds(off, width), :])
        m_ref[rows] = m_new

    def full_block(j, carry):
        update(0, tq, pl.multiple_of(j * tq, tq), tq, False)
        return carry

    lax.fori_loop(0, i, full_block, 0)
    base = pl.multiple_of(i * tq, tq)
    update(0, tq, base, half, True)
    update(half, half, base + half, half, True)
    o_ref[0] = (acc_ref[...] / l_ref[...]).astype(BF16)


def attention(q, k, v):
    bsz, s, _ = q.shape
    tq = min(ATTN_TILE, s)
    return pl.pallas_call(
        _attn_kernel,
        grid=(bsz, MLA_HEADS, s // tq),
        in_specs=[pl.BlockSpec((1, tq, MLA_QK), lambda b, h, i: (b, i, h)),
                  pl.BlockSpec((1, s, MLA_QK), lambda b, h, i: (b, 0, h)),
                  pl.BlockSpec((1, s, V_DIM), lambda b, h, i: (b, 0, h))],
        out_specs=pl.BlockSpec((1, tq, V_DIM), lambda b, h, i: (b, i, h)),
        out_shape=jax.ShapeDtypeStruct((bsz, s, MLA_HEADS * V_DIM), BF16),
        scratch_shapes=[pltpu.VMEM((tq, 1), F32), pltpu.VMEM((tq, 1), F32), pltpu.VMEM((tq, V_DIM), F32)],
        compiler_params=_cparams("parallel", "parallel", "arbitrary"),
        name="attention",
    )(q, k, v)


def kernel(x, c, positions, ada_w, ada_b, ln_g, ln_b, w_in_e, gdn_conv_w, gdn_a_log, gdn_dt_bias,
           gdn_norm_w, sc_conv_w, w_out_e, w_in_o, q_a_norm, w_q_b, kv_a_norm, w_kv_b, w_out_o,
           router_w, router_bias, w_gate, w_up, w_down):
    bsz, s, d = x.shape
    depth = ada_w.shape[0]
    mod = ada_mod(c, ada_w, ada_b).reshape(depth, 2, bsz, 1, 3 * d)
    for i in range(depth):
        j = i // 2
        moda, modb = mod[i, 0][:, None], mod[i, 1][:, None]
        if i % 2 == 0:
            o1, bg, bgt = even_in_proj(x, moda, w_in_e[j], gdn_a_log[j], gdn_dt_bias[j])
            qd, kd, u, w, ic, gl = gdn_prep(o1, bg, bgt, gdn_conv_w[j])
            o = gdn_scan(qd, kd, u, w, ic, gl)
            ymix = even_post(o, o1, gdn_norm_w[j], sc_conv_w[j])
            w_out = w_out_e[j]
        else:
            q, k, v = mla_in(x, moda, positions, w_in_o[j], q_a_norm[j], w_q_b[j], kv_a_norm[j], w_kv_b[j])
            ymix = attention(q, k, v)
            w_out = w_out_o[j]
        x1, hm, seli, selw, cnt = mix_out(ymix, w_out, x, moda, modb, ln_g[i, 0], ln_b[i, 0],
                                          router_w, router_bias)
        x = moe_layer(x1, hm, seli, selw, cnt, modb, ln_g[i, 1], ln_b[i, 1],
                      w_gate[i], w_up[i], w_down[i])
    return x
```

```python
import functools

import numpy as np
import jax
import jax.numpy as jnp
from jax import lax
from jax.experimental import pallas as pl
from jax.experimental.pallas import tpu as pltpu

F32 = jnp.float32
BF16 = jnp.bfloat16
HIGHEST = lax.Precision.HIGHEST

D_MODEL = 1024
DEPTH = 2
DN_ALPHA = (2.0 * DEPTH) ** 0.25

GDN_HEADS = 4
GDN_HEAD_DIM = 128
GDN_WIDTH = GDN_HEADS * GDN_HEAD_DIM
GDN_CONV = 4
GDN_CHUNK = 64
SC_WIDTH = 512
SC_CONV = 3

MLA_HEADS = 8
Q_LORA = 384
KV_LORA = 256
QK_NOPE = 128
QK_ROPE = 64
V_DIM = 128
ROPE_THETA = 10000.0

N_EXPERTS = 32
N_GROUPS = 4
EXPERTS_PER_GROUP = N_EXPERTS // N_GROUPS
D_EXPERT = 512

NORM_EPS = 1e-6
LN_EPS = 1e-5

LANES = 128
HALO = 16
TOKEN_TILE = 512
GDN_TILE = 256
ATTN_TILE = 1024
ATTN_ROWS = 256
LOG2_E = 1.4426950408889634
MOE_BLOCK = 256
BATCH_GROUPS = 2
VMEM_LIMIT = 48 * 1024 * 1024


def _cparams(*sem):
    return pltpu.CompilerParams(dimension_semantics=sem, vmem_limit_bytes=VMEM_LIMIT)


def _sigmoid(x):
    return 1.0 / (1.0 + jnp.exp(-x))


def _silu(x):
    return x * _sigmoid(x)


def _dot(a, b):
    return jnp.dot(a, b, preferred_element_type=F32)


def _dot_nt(a, b, precision=None):
    return lax.dot_general(a, b, (((1,), (1,)), ((), ())), precision=precision,
                           preferred_element_type=F32)


def _dot_tn(a, b):
    return lax.dot_general(a, b, (((0,), (0,)), ((), ())), preferred_element_type=F32)


def _ada_kernel(c_ref, w_ref, b_ref, o_ref):
    cond = _silu(c_ref[...])
    o_ref[0] = jnp.dot(cond, w_ref[0], precision=HIGHEST, preferred_element_type=F32) + b_ref[0]


def ada_mod(c, ada_w, ada_b):
    nl = ada_w.shape[0] * ada_w.shape[1]
    bsz, d = c.shape
    w = ada_w.reshape(nl, d, 3 * d)
    b = ada_b.reshape(nl, 1, 3 * d)
    return pl.pallas_call(
        _ada_kernel,
        grid=(nl, 3),
        in_specs=[pl.BlockSpec((bsz, d), lambda l, j: (0, 0)),
                  pl.BlockSpec((1, d, d), lambda l, j: (l, 0, j)),
                  pl.BlockSpec((1, 1, d), lambda l, j: (l, 0, j))],
        out_specs=pl.BlockSpec((1, bsz, d), lambda l, j: (l, 0, j)),
        out_shape=jax.ShapeDtypeStruct((nl, bsz, 3 * d), F32),
        compiler_params=_cparams("parallel", "parallel"),
        name="ada_mod",
    )(c, w, b)


EVEN_MAIN = 3 * GDN_WIDTH + GDN_WIDTH + 3 * SC_WIDTH


def _even_in_kernel(x_ref, mod_ref, w1_ref, w2_ref, alog_ref, dtb_ref, o1_ref, o2_ref, o2t_ref):
    d = D_MODEL
    mod = mod_ref[0, 0]
    h = (x_ref[0] * (1.0 + mod[:, d:2 * d]) + mod[:, :d]).astype(BF16)
    for j in range(EVEN_MAIN // 512):
        o1_ref[0, :, j * 512:(j + 1) * 512] = _dot(h, w1_ref[:, j * 512:(j + 1) * 512]).astype(BF16)
    r = _dot(h, w2_ref[...])
    lane = lax.broadcasted_iota(jnp.int32, r.shape, 1)
    a = r + dtb_ref[...]
    softplus = jnp.maximum(a, 0.0) + jnp.log(1.0 + jnp.exp(-jnp.abs(a)))
    bg = jnp.where(lane < GDN_HEADS, _sigmoid(r), -jnp.exp(alog_ref[...]) * softplus)
    o2_ref[0] = bg
    o2t_ref[0] = bg.T[:8]


def even_in_proj(x, mod, w_in, a_log, dt_bias):
    bsz, s, d = x.shape
    tm = min(TOKEN_TILE, s)
    q_end = 4 * GDN_WIDTH
    w1 = jnp.concatenate([w_in[:, :q_end], w_in[:, q_end + 2 * GDN_HEADS:]], axis=1).astype(BF16)
    w2 = jnp.pad(w_in[:, q_end:q_end + 2 * GDN_HEADS], ((0, 0), (0, LANES - 2 * GDN_HEADS))).astype(BF16)
    alog = jnp.pad(a_log.astype(F32), (GDN_HEADS, LANES - 2 * GDN_HEADS)).reshape(1, LANES)
    dtb = jnp.pad(dt_bias.astype(F32), (GDN_HEADS, LANES - 2 * GDN_HEADS)).reshape(1, LANES)
    const = lambda b, i: (0, 0)
    return pl.pallas_call(
        _even_in_kernel,
        grid=(bsz, s // tm),
        in_specs=[pl.BlockSpec((1, tm, d), lambda b, i: (b, i, 0)),
                  pl.BlockSpec((1, 1, 1, 3 * d), lambda b, i: (b, 0, 0, 0)),
                  pl.BlockSpec((d, EVEN_MAIN), const),
                  pl.BlockSpec((d, LANES), const),
                  pl.BlockSpec((1, LANES), const),
                  pl.BlockSpec((1, LANES), const)],
        out_specs=[pl.BlockSpec((1, tm, EVEN_MAIN), lambda b, i: (b, i, 0)),
                   pl.BlockSpec((1, tm, LANES), lambda b, i: (b, i, 0)),
                   pl.BlockSpec((1, 8, tm), lambda b, i: (b, 0, i))],
        out_shape=[jax.ShapeDtypeStruct((bsz, s, EVEN_MAIN), BF16),
                   jax.ShapeDtypeStruct((bsz, s, LANES), F32),
                   jax.ShapeDtypeStruct((bsz, 8, s), F32)],
        compiler_params=_cparams("parallel", "parallel"),
        name="even_in_proj",
    )(x, mod, w1, w2, alog, dtb)


def _causal_conv(x, halo, w, taps):
    rows = x.shape[0]
    xf = jnp.concatenate([halo, x], axis=0)
    y = w[taps - 1:taps] * x
    for j in range(taps - 1):
        off = HALO - (taps - 1) + j
        y = y + w[j:j + 1] * xf[off:off + rows]
    return y


def _inv_unit_lower(low, xor_ij, block):
    eye = jnp.where(xor_ij == 0, 1.0, 0.0)
    m = eye - jnp.where(xor_ij == 1, low, 0.0)
    s = 2
    while s < block:
        shift = s.bit_length() - 1
        cs = jnp.where((xor_ij >> shift) == 1, low, 0.0).astype(BF16)
        mb = m.astype(BF16)
        m = m - _dot(mb, _dot(cs, mb).astype(BF16))
        s *= 2
    return m


def _gdn_prep_kernel(qkv_ref, halo_ref, cw_ref, bg_ref, bgt_ref, ltri_ref, utri_ref, same_ref,
                     qd_ref, kd_ref, u_ref, w_ref, ic_ref, gl_ref):
    ts = qkv_ref.shape[1]
    c = GDN_CHUNK
    hd = GDN_HEAD_DIM
    i = pl.program_id(1)
    x = qkv_ref[0].astype(F32)
    halo = jnp.where(i > 0, halo_ref[0].astype(F32), 0.0)
    y = _silu(_causal_conv(x, halo, cw_ref[...], GDN_CONV))

    bg = bg_ref[0]
    gc_col = jnp.dot(ltri_ref[...], bg, precision=HIGHEST, preferred_element_type=F32)
    gc_row = jnp.dot(bgt_ref[0], utri_ref[...], precision=HIGHEST, preferred_element_type=F32)
    gc_end = jnp.dot(same_ref[...], bg, precision=HIGHEST, preferred_element_type=F32)

    ii = lax.broadcasted_iota(jnp.int32, (ts, ts), 0)
    jj = lax.broadcasted_iota(jnp.int32, (ts, ts), 1)
    xor_ij = ii ^ jj
    causal = (same_ref[...] > 0.0) & (ii >= jj)
    diag = xor_ij == 0

    for h in range(GDN_HEADS):
        q = y[:, h * hd:(h + 1) * hd]
        k = y[:, GDN_WIDTH + h * hd:GDN_WIDTH + (h + 1) * hd]
        v = y[:, 2 * GDN_WIDTH + h * hd:2 * GDN_WIDTH + (h + 1) * hd]
        q = q * lax.rsqrt(jnp.sum(q * q, -1, keepdims=True) + NORM_EPS) * (hd ** -0.5)
        k = k * lax.rsqrt(jnp.sum(k * k, -1, keepdims=True) + NORM_EPS)
        beta = bg[:, h:h + 1]
        gcol = gc_col[:, GDN_HEADS + h:GDN_HEADS + h + 1]
        grow = gc_row[GDN_HEADS + h:GDN_HEADS + h + 1, :]
        gend = gc_end[:, GDN_HEADS + h:GDN_HEADS + h + 1]
        decay = jnp.exp(jnp.where(causal, gcol - grow, -jnp.inf))
        kb = k * beta
        kbf = k.astype(BF16)
        low = jnp.where(diag, 0.0, _dot_nt(kb.astype(BF16), kbf) * decay)
        intra = _dot_nt(q.astype(BF16), kbf) * decay
        egc = jnp.exp(gcol)
        rhs = jnp.concatenate([v * beta, kb * egc], axis=-1).astype(BF16)
        sol = _dot(_inv_unit_lower(low, xor_ij, c).astype(BF16), rhs)
        cols = slice(h * hd, (h + 1) * hd)
        u_ref[0, :, cols] = sol[:, :hd]
        w_ref[0, :, cols] = sol[:, hd:].astype(BF16)
        qd_ref[0, :, cols] = (q * egc).astype(BF16)
        kd_ref[0, :, cols] = (k * jnp.exp(gend - gcol)).astype(BF16)
        packed = intra[:, :c]
        for n in range(1, ts // c):
            packed = packed + intra[:, n * c:(n + 1) * c]
        ic_ref[0, :, h * c:(h + 1) * c] = packed.astype(BF16)
        for n in range(ts // c):
            gl_ref[0, n, h:h + 1, :] = jnp.broadcast_to(jnp.exp(gend[n * c:n * c + 1]), (1, hd))


def gdn_prep(o1, bg, bgt, conv_w):
    bsz, s, _ = o1.shape
    ts = min(GDN_TILE, s)
    c = GDN_CHUNK
    nc = ts // c
    qkv_w = 3 * GDN_WIDTH
    r = np.arange(ts)
    same = (r[:, None] // c) == (r[None, :] // c)
    ltri = jnp.asarray((same & (r[:, None] >= r[None, :])).astype(np.float32))
    utri = jnp.asarray((same & (r[:, None] <= r[None, :])).astype(np.float32))
    same = jnp.asarray(same.astype(np.float32))
    hb = ts // HALO
    tok = lambda b, i: (b, i, 0)
    const = lambda b, i: (0, 0)
    wide = jax.ShapeDtypeStruct((bsz, s, GDN_WIDTH), BF16)
    return pl.pallas_call(
        _gdn_prep_kernel,
        grid=(bsz, s // ts),
        in_specs=[pl.BlockSpec((1, ts, qkv_w), tok),
                  pl.BlockSpec((1, HALO, qkv_w), lambda b, i: (b, jnp.maximum(i * hb - 1, 0), 0)),
                  pl.BlockSpec((GDN_CONV, qkv_w), const),
                  pl.BlockSpec((1, ts, LANES), tok),
                  pl.BlockSpec((1, 8, ts), lambda b, i: (b, 0, i)),
                  pl.BlockSpec((ts, ts), const),
                  pl.BlockSpec((ts, ts), const),
                  pl.BlockSpec((ts, ts), const)],
        out_specs=[pl.BlockSpec((1, ts, GDN_WIDTH), tok),
                   pl.BlockSpec((1, ts, GDN_WIDTH), tok),
                   pl.BlockSpec((1, ts, GDN_WIDTH), tok),
                   pl.BlockSpec((1, ts, GDN_WIDTH), tok),
                   pl.BlockSpec((1, ts, GDN_HEADS * c), tok),
                   pl.BlockSpec((1, nc, GDN_HEADS, GDN_HEAD_DIM), lambda b, i: (b, i, 0, 0))],
        out_shape=[wide, wide, jax.ShapeDtypeStruct((bsz, s, GDN_WIDTH), F32), wide,
                   jax.ShapeDtypeStruct((bsz, s, GDN_HEADS * c), BF16),
                   jax.ShapeDtypeStruct((bsz, s // c, GDN_HEADS, GDN_HEAD_DIM), F32)],
        compiler_params=_cparams("parallel", "parallel"),
        name="gdn_prep",
    )(o1, o1, conv_w.astype(F32), bg, bgt, ltri, utri, same)


def _gdn_scan_kernel(qd_ref, kd_ref, u_ref, w_ref, ic_ref, gl_ref, o_ref, state_ref):
    ts = qd_ref.shape[1]
    c = GDN_CHUNK
    hd = GDN_HEAD_DIM

    @pl.when(pl.program_id(1) == 0)
    def _():
        state_ref[...] = jnp.zeros_like(state_ref)

    for n in range(ts // c):
        rows = slice(n * c, (n + 1) * c)
        for h in range(GDN_HEADS):
            cols = slice(h * hd, (h + 1) * hd)
            st = state_ref[h]
            stb = st.astype(BF16)
            v_new = u_ref[0, rows, cols] - _dot(w_ref[0, rows, cols], stb)
            vb = v_new.astype(BF16)
            o_ref[0, rows, cols] = _dot(qd_ref[0, rows, cols], stb) + _dot(ic_ref[0, rows, h * c:(h + 1) * c], vb)
            state_ref[h] = st * gl_ref[0, n, h:h + 1, :] + _dot_tn(kd_ref[0, rows, cols], vb)


def gdn_scan(qd, kd, u, w, ic, gl):
    bsz, s, _ = qd.shape
    ts = min(GDN_TILE, s)
    c = GDN_CHUNK
    tok = lambda b, i: (b, i, 0)
    return pl.pallas_call(
        _gdn_scan_kernel,
        grid=(bsz, s // ts),
        in_specs=[pl.BlockSpec((1, ts, GDN_WIDTH), tok)] * 4 + [
            pl.BlockSpec((1, ts, GDN_HEADS * c), tok),
            pl.BlockSpec((1, ts // c, GDN_HEADS, GDN_HEAD_DIM), lambda b, i: (b, i, 0, 0))],
        out_specs=pl.BlockSpec((1, ts, GDN_WIDTH), tok),
        out_shape=jax.ShapeDtypeStruct((bsz, s, GDN_WIDTH), F32),
        scratch_shapes=[pltpu.VMEM((GDN_HEADS, GDN_HEAD_DIM, GDN_HEAD_DIM), F32)],
        compiler_params=_cparams("parallel", "arbitrary"),
        name="gdn_scan",
    )(qd, kd, u, w, ic, gl)


def _even_post_kernel(o_ref, z_ref, scb_ref, scc_ref, sch_ref, scc_halo_ref, sch_halo_ref,
                      nw_ref, cw_ref, y_ref):
    hd = GDN_HEAD_DIM
    i = pl.program_id(1)
    o = o_ref[0]
    z = z_ref[0].astype(F32)
    nw = nw_ref[...]
    for h in range(GDN_HEADS):
        cols = slice(h * hd, (h + 1) * hd)
        oh = o[:, cols]
        on = oh * lax.rsqrt(jnp.mean(oh * oh, -1, keepdims=True) + NORM_EPS) * nw
        y_ref[0, :, cols] = (on * _silu(z[:, cols])).astype(BF16)
    ch = scc_ref[0].astype(F32) * sch_ref[0].astype(F32)
    ch_halo = jnp.where(i > 0, scc_halo_ref[0].astype(F32) * sch_halo_ref[0].astype(F32), 0.0)
    yb = scb_ref[0].astype(F32) * _causal_conv(ch, ch_halo, cw_ref[...], SC_CONV)
    y_ref[0, :, GDN_WIDTH:] = yb.astype(BF16)


def even_post(o, o1, norm_w, sc_conv_w):
    bsz, s, _ = o.shape
    tm = min(TOKEN_TILE, s)
    hb = tm // HALO
    wb = 512
    blk = lambda j: pl.BlockSpec((1, tm, wb), lambda b, i, j=j: (b, i, j))
    halo = lambda j: pl.BlockSpec((1, HALO, wb), lambda b, i, j=j: (b, jnp.maximum(i * hb - 1, 0), j))
    const = lambda b, i: (0, 0)
    return pl.pallas_call(
        _even_post_kernel,
        grid=(bsz, s // tm),
        in_specs=[pl.BlockSpec((1, tm, GDN_WIDTH), lambda b, i: (b, i, 0)),
                  blk(3), blk(4), blk(5), blk(6), halo(5), halo(6),
                  pl.BlockSpec((1, GDN_HEAD_DIM), const),
                  pl.BlockSpec((SC_CONV, SC_WIDTH), const)],
        out_specs=pl.BlockSpec((1, tm, GDN_WIDTH + SC_WIDTH), lambda b, i: (b, i, 0)),
        out_shape=jax.ShapeDtypeStruct((bsz, s, GDN_WIDTH + SC_WIDTH), BF16),
        compiler_params=_cparams("parallel", "parallel"),
        name="even_post",
    )(o, o1, o1, o1, o1, o1, o1, norm_w.astype(F32).reshape(1, -1), sc_conv_w.astype(F32))


def _layer_norm(r, g, b):
    mu = jnp.mean(r, -1, keepdims=True)
    rc = r - mu
    var = jnp.mean(rc * rc, -1, keepdims=True)
    return rc * lax.rsqrt(var + LN_EPS) * g + b


def _route(logits_t, bias):
    scores = _sigmoid(logits_t)
    biased = scores + bias
    t = logits_t.shape[1]
    epg = EXPERTS_PER_GROUP
    sub = lax.broadcasted_iota(jnp.int32, (epg, t), 0).astype(F32)
    best = None
    for g in range(N_GROUPS):
        bgp = biased[g * epg:(g + 1) * epg]
        m1 = jnp.max(bgp, axis=0, keepdims=True)
        i1 = jnp.min(jnp.where(bgp == m1, sub, float(epg)), axis=0, keepdims=True)
        rest = jnp.where(sub == i1, -jnp.inf, bgp)
        m2 = jnp.max(rest, axis=0, keepdims=True)
        i2 = jnp.min(jnp.where(rest == m2, sub, float(epg)), axis=0, keepdims=True)
        gs = m1 + m2
        if best is None:
            best, e0, e1 = gs, i1, i2
        else:
            better = gs > best
            best = jnp.where(better, gs, best)
            e0 = jnp.where(better, float(g * epg) + i1, e0)
            e1 = jnp.where(better, float(g * epg) + i2, e1)
    eio = lax.broadcasted_iota(jnp.int32, scores.shape, 0).astype(F32)
    hit0 = eio == e0
    hit1 = eio == e1
    s0 = jnp.sum(jnp.where(hit0, scores, 0.0), axis=0, keepdims=True)
    s1 = jnp.sum(jnp.where(hit1, scores, 0.0), axis=0, keepdims=True)
    tot = s0 + s1
    return e0.astype(jnp.int32), e1.astype(jnp.int32), s0 / tot, s1 / tot, hit0, hit1


def _mix_out_kernel(y_ref, wo_ref, x_ref, moda_ref, modb_ref, lng_ref, lnb_ref, rwt_ref, rb_ref,
                    upper_ref, x1_ref, hm_ref, seli_ref, selw_ref, cnt_ref):
    d = D_MODEL
    gate = moda_ref[0, 0][:, 2 * d:]
    modb = modb_ref[0, 0]
    y = _dot(y_ref[0], wo_ref[...])
    x1 = _layer_norm(DN_ALPHA * x_ref[0] + (1.0 + gate) * y, lng_ref[...], lnb_ref[...])
    x1_ref[0] = x1
    hm = x1 * (1.0 + modb[:, d:2 * d]) + modb[:, :d]
    hm_ref[0] = hm.astype(BF16)

    logits_t = _dot_nt(rwt_ref[...], hm, precision=HIGHEST)
    e0, e1, w0, w1, hit0, hit1 = _route(logits_t, rb_ref[...])
    member = jnp.where(hit0 | hit1, 1.0, 0.0)
    before = _dot(member.astype(BF16), upper_ref[...])
    rank0 = jnp.sum(jnp.where(hit0, before, 0.0), axis=0, keepdims=True).astype(jnp.int32)
    rank1 = jnp.sum(jnp.where(hit1, before, 0.0), axis=0, keepdims=True).astype(jnp.int32)
    zi = jnp.zeros_like(e0)
    seli_ref[0] = jnp.concatenate([e0, e1, rank0, rank1, zi, zi, zi, zi], axis=0)
    zf = jnp.zeros_like(w0)
    selw_ref[0] = jnp.concatenate([w0, w1, zf, zf, zf, zf, zf, zf], axis=0)
    cnt_ref[0, 0] = jnp.broadcast_to(jnp.sum(member, axis=1, keepdims=True), (N_EXPERTS, LANES))


def mix_out(ymix, w_out, x, moda, modb, ln_g, ln_b, router_w, router_bias):
    bsz, s, d = x.shape
    tm = min(TOKEN_TILE, s)
    nt = s // tm
    r = np.arange(tm)
    upper = jnp.asarray((r[:, None] < r[None, :]).astype(np.float32), dtype=BF16)
    tok = lambda b, i: (b, i, 0)
    const = lambda b, i: (0, 0)
    modspec = pl.BlockSpec((1, 1, 1, 3 * d), lambda b, i: (b, 0, 0, 0))
    row8 = pl.BlockSpec((1, 8, tm), lambda b, i: (b, 0, i))
    return pl.pallas_call(
        _mix_out_kernel,
        grid=(bsz, nt),
        in_specs=[pl.BlockSpec((1, tm, d), tok),
                  pl.BlockSpec((d, d), const),
                  pl.BlockSpec((1, tm, d), tok),
                  modspec, modspec,
                  pl.BlockSpec((1, d), const), pl.BlockSpec((1, d), const),
                  pl.BlockSpec((N_EXPERTS, d), const),
                  pl.BlockSpec((N_EXPERTS, 1), const),
                  pl.BlockSpec((tm, tm), const)],
        out_specs=[pl.BlockSpec((1, tm, d), tok),
                   pl.BlockSpec((1, tm, d), tok),
                   row8, row8,
                   pl.BlockSpec((1, 1, N_EXPERTS, LANES), lambda b, i: (b, i, 0, 0))],
        out_shape=[jax.ShapeDtypeStruct((bsz, s, d), F32),
                   jax.ShapeDtypeStruct((bsz, s, d), BF16),
                   jax.ShapeDtypeStruct((bsz, 8, s), jnp.int32),
                   jax.ShapeDtypeStruct((bsz, 8, s), F32),
                   jax.ShapeDtypeStruct((bsz, nt, N_EXPERTS, LANES), F32)],
        compiler_params=_cparams("parallel", "parallel"),
        name="mix_out",
    )(ymix, w_out.astype(BF16), x, moda, modb, ln_g.reshape(1, d), ln_b.reshape(1, d),
      router_w.T.astype(F32), router_bias.astype(F32).reshape(N_EXPERTS, 1), upper)


def _experts_kernel(be_ref, nused_ref, xs_ref, wg_ref, wu_ref, wd_ref, ys_ref, wgb_ref, wub_ref, wdb_ref):
    i = pl.program_id(0)

    @pl.when((i == 0) | (be_ref[i] != be_ref[jnp.maximum(i - 1, 0)]))
    def _():
        wgb_ref[...] = wg_ref[0, 0].astype(BF16)
        wub_ref[...] = wu_ref[0, 0].astype(BF16)
        wdb_ref[...] = wd_ref[0, 0].astype(BF16)

    @pl.when(i < nused_ref[0])
    def _():
        x = xs_ref[...]
        act = (_silu(_dot(x, wgb_ref[...])) * _dot(x, wub_ref[...])).astype(BF16)
        ys_ref[...] = _dot(act, wdb_ref[...]).astype(BF16)

    @pl.when(i >= nused_ref[0])
    def _():
        ys_ref[...] = jnp.zeros_like(ys_ref)


def moe_experts(xs, block_expert, n_used, w_gate, w_up, w_down, layer):
    rows, d = xs.shape
    nb = rows // MOE_BLOCK
    by_expert = lambda i, be, nu: (layer, be[i], 0, 0)
    return pl.pallas_call(
        _experts_kernel,
        grid_spec=pltpu.PrefetchScalarGridSpec(
            num_scalar_prefetch=2,
            grid=(nb,),
            in_specs=[pl.BlockSpec((MOE_BLOCK, d), lambda i, be, nu: (i, 0)),
                      pl.BlockSpec((1, 1, d, D_EXPERT), by_expert),
                      pl.BlockSpec((1, 1, d, D_EXPERT), by_expert),
                      pl.BlockSpec((1, 1, D_EXPERT, d), by_expert)],
            out_specs=pl.BlockSpec((MOE_BLOCK, d), lambda i, be, nu: (i, 0)),
            scratch_shapes=[pltpu.VMEM((d, D_EXPERT), BF16), pltpu.VMEM((d, D_EXPERT), BF16),
                            pltpu.VMEM((D_EXPERT, d), BF16)]),
        out_shape=jax.ShapeDtypeStruct((rows, d), BF16),
        compiler_params=_cparams("arbitrary"),
        name="moe_experts",
    )(block_expert, n_used, xs, w_gate, w_up, w_down)


def _plan_kernel(seli_ref, base_ref, dest_ref):
    sel = seli_ref[0]
    base = base_ref[0, 0]
    eio = lax.broadcasted_iota(jnp.int32, (N_EXPERTS, sel.shape[1]), 0)
    out = []
    for kk in range(2):
        first = jnp.sum(jnp.where(eio == sel[kk:kk + 1], base, 0.0), axis=0, keepdims=True)
        out.append(first.astype(jnp.int32) + sel[2 + kk:3 + kk])
    zero = jnp.zeros_like(out[0])
    dest_ref[0] = jnp.concatenate(out + [zero] * 6, axis=0)


def dispatch_plan(seli, base):
    bsz, _, s = seli.shape
    nt = base.shape[1]
    tm = s // nt
    return pl.pallas_call(
        _plan_kernel,
        grid=(bsz, nt),
        in_specs=[pl.BlockSpec((1, 8, tm), lambda b, i: (b, 0, i)),
                  pl.BlockSpec((1, 1, N_EXPERTS, 1), lambda b, i: (b, i, 0, 0))],
        out_specs=pl.BlockSpec((1, 8, tm), lambda b, i: (b, 0, i)),
        out_shape=jax.ShapeDtypeStruct((bsz, 8, s), jnp.int32),
        compiler_params=_cparams("parallel", "parallel"),
        name="dispatch_plan",
    )(seli, base)


def _invert_kernel(dest_ref, slot_tok_ref):
    t = dest_ref.shape[0] // 2

    def clear(r, carry):
        slot_tok_ref[r] = 0
        return carry

    def put(a, carry):
        slot_tok_ref[dest_ref[a]] = a
        slot_tok_ref[dest_ref[t + a]] = a
        return carry

    lax.fori_loop(0, slot_tok_ref.shape[0], clear, 0, unroll=8)
    lax.fori_loop(0, t, put, 0, unroll=8)


def invert_slots(dest_flat, rows):
    return pl.pallas_call(
        _invert_kernel,
        in_specs=[pl.BlockSpec(memory_space=pltpu.SMEM)],
        out_specs=pl.BlockSpec(memory_space=pltpu.SMEM),
        out_shape=jax.ShapeDtypeStruct((rows,), jnp.int32),
        name="invert_slots",
    )(dest_flat)


def _moe_combine_kernel(g0_ref, g1_ref, wt_ref, x_ref, mod_ref, lng_ref, lnb_ref, o_ref):
    d = D_MODEL
    gate = mod_ref[0, 0][:, 2 * d:]
    wt = wt_ref[0]
    y = g0_ref[0].astype(F32) * wt[:, 0:1] + g1_ref[0].astype(F32) * wt[:, 1:2]
    o_ref[0] = _layer_norm(DN_ALPHA * x_ref[0] + (1.0 + gate) * y, lng_ref[...], lnb_ref[...])


def moe_combine(g0, g1, wt, x1, modb, ln_g, ln_b):
    bsz, s, d = x1.shape
    tm = min(TOKEN_TILE, s)
    tok = lambda b, i: (b, i, 0)
    const = lambda b, i: (0, 0)
    return pl.pallas_call(
        _moe_combine_kernel,
        grid=(bsz, s // tm),
        in_specs=[pl.BlockSpec((1, tm, d), tok), pl.BlockSpec((1, tm, d), tok),
                  pl.BlockSpec((1, tm, 8), tok), pl.BlockSpec((1, tm, d), tok),
                  pl.BlockSpec((1, 1, 1, 3 * d), lambda b, i: (b, 0, 0, 0)),
                  pl.BlockSpec((1, d), const), pl.BlockSpec((1, d), const)],
        out_specs=pl.BlockSpec((1, tm, d), tok),
        out_shape=jax.ShapeDtypeStruct((bsz, s, d), F32),
        compiler_params=_cparams("parallel", "parallel"),
        name="moe_combine",
    )(g0, g1, wt, x1, modb, ln_g.reshape(1, d), ln_b.reshape(1, d))


def moe_layer(x1, hm, seli, selw, cnt, modb, ln_g, ln_b, w_gate, w_up, w_down, layer):
    bsz, s, d = x1.shape
    t = bsz * s
    tm = min(TOKEN_TILE, s)
    blk = MOE_BLOCK
    cnt = cnt[..., 0].reshape(-1, N_EXPERTS).astype(jnp.int32)
    tile_off = jnp.cumsum(cnt, axis=0) - cnt
    counts = cnt.sum(0)
    padded = (counts + blk - 1) // blk * blk
    pend = jnp.cumsum(padded)
    pstart = pend - padded
    nb = -(-(2 * t) // blk) + N_EXPERTS
    rows = nb * blk
    base = (pstart[None, :] + tile_off).astype(F32).reshape(bsz, s // tm, N_EXPERTS, 1)
    dest = dispatch_plan(seli, base)
    dest = jnp.swapaxes(dest[:, 0:2, :], 0, 1).reshape(2 * t)
    slot_tok = invert_slots(dest, rows)
    starts = jnp.arange(nb, dtype=jnp.int32) * blk
    block_expert = jnp.minimum(jnp.sum((pend[None, :] <= starts[:, None]).astype(jnp.int32), axis=1),
                               N_EXPERTS - 1)
    n_used = (pend[-1] // blk).astype(jnp.int32).reshape(1)
    xs = jnp.take(hm.reshape(t, d), slot_tok, axis=0, mode="clip")
    ys = moe_experts(xs, block_expert, n_used, w_gate, w_up, w_down, layer)
    g0 = jnp.take(ys, dest[:t], axis=0, mode="clip").reshape(bsz, s, d)
    g1 = jnp.take(ys, dest[t:], axis=0, mode="clip").reshape(bsz, s, d)
    wt = jnp.swapaxes(selw, 1, 2)
    return moe_combine(g0, g1, wt, x1, modb, ln_g, ln_b)


MLA_IN_COLS = Q_LORA + KV_LORA + 2 * LANES
MLA_QK = 2 * LANES


def _mla_in_kernel(x_ref, mod_ref, pos_ref, invf_ref, win_ref, qn_ref, kvn_ref, wq_ref, wkv_ref,
                   q_ref, k_ref, v_ref):
    d = D_MODEL
    nh = MLA_HEADS
    mod = mod_ref[0, 0]
    h = (x_ref[0] * (1.0 + mod[:, d:2 * d]) + mod[:, :d]).astype(BF16)
    proj = _dot(h, win_ref[...])
    ang = pos_ref[0].astype(F32) * invf_ref[...]
    lane = lax.broadcasted_iota(jnp.int32, ang.shape, 1)
    cos = jnp.where(lane < QK_ROPE, jnp.cos(ang), 0.0)
    sin = jnp.where(lane < QK_ROPE, jnp.sin(ang), 0.0)

    qa = proj[:, :Q_LORA]
    qa = (qa * lax.rsqrt(jnp.mean(qa * qa, -1, keepdims=True) + NORM_EPS) * qn_ref[...]).astype(BF16)
    kva = proj[:, Q_LORA:Q_LORA + KV_LORA]
    kva = (kva * lax.rsqrt(jnp.mean(kva * kva, -1, keepdims=True) + NORM_EPS) * kvn_ref[...]).astype(BF16)
    kr0 = Q_LORA + KV_LORA
    k_rope = (proj[:, kr0:kr0 + LANES] * cos + proj[:, kr0 + LANES:kr0 + 2 * LANES] * sin).astype(BF16)

    scale = (QK_NOPE + QK_ROPE) ** -0.5 * LOG2_E
    hw = nh * LANES
    q_nope = _dot(qa, wq_ref[:, :hw])
    q_rope = _dot(qa, wq_ref[:, hw:2 * hw])
    q_rot = _dot(qa, wq_ref[:, 2 * hw:])
    k_nope = _dot(kva, wkv_ref[:, :hw])
    v_ref[0] = _dot(kva, wkv_ref[:, hw:]).astype(BF16)
    for hh in range(nh):
        cols = slice(hh * LANES, (hh + 1) * LANES)
        q_ref[0, :, hh * MLA_QK:hh * MLA_QK + LANES] = (q_nope[:, cols] * scale).astype(BF16)
        q_ref[0, :, hh * MLA_QK + LANES:(hh + 1) * MLA_QK] = (
            (q_rope[:, cols] * cos + q_rot[:, cols] * sin) * scale).astype(BF16)
        k_ref[0, :, hh * MLA_QK:hh * MLA_QK + LANES] = k_nope[:, cols].astype(BF16)
        k_ref[0, :, hh * MLA_QK + LANES:(hh + 1) * MLA_QK] = k_rope


def _rope_cols(w):
    half = QK_ROPE // 2
    pad = [(0, 0)] * (w.ndim - 1) + [(0, LANES - QK_ROPE)]
    rot = jnp.concatenate([-w[..., half:], w[..., :half]], axis=-1)
    return jnp.pad(w, pad), jnp.pad(rot, pad)


def mla_in(x, mod, positions, w_in, q_a_norm, w_q_b, kv_a_norm, w_kv_b):
    bsz, s, d = x.shape
    tm = min(TOKEN_TILE, s)
    nh = MLA_HEADS
    kr, kr_rot = _rope_cols(w_in[:, Q_LORA + KV_LORA:])
    win = jnp.concatenate([w_in[:, :Q_LORA + KV_LORA], kr, kr_rot], axis=1).astype(BF16)
    wq = w_q_b.reshape(Q_LORA, nh, QK_NOPE + QK_ROPE)
    qr, qr_rot = _rope_cols(wq[..., QK_NOPE:])
    wq = jnp.concatenate([wq[..., :QK_NOPE].reshape(Q_LORA, -1), qr.reshape(Q_LORA, -1),
                          qr_rot.reshape(Q_LORA, -1)], axis=1).astype(BF16)
    wkv = w_kv_b.reshape(KV_LORA, nh, QK_NOPE + V_DIM)
    wkv = jnp.concatenate([wkv[..., :QK_NOPE].reshape(KV_LORA, -1),
                           wkv[..., QK_NOPE:].reshape(KV_LORA, -1)], axis=1).astype(BF16)
    inv = ROPE_THETA ** (-np.arange(0, QK_ROPE, 2, dtype=np.float32) / QK_ROPE)
    invf = np.zeros((1, LANES), np.float32)
    invf[0, :QK_ROPE] = np.concatenate([inv, inv])
    tok = lambda b, i: (b, i, 0)
    const = lambda b, i: (0, 0)
    return pl.pallas_call(
        _mla_in_kernel,
        grid=(bsz, s // tm),
        in_specs=[pl.BlockSpec((1, tm, d), tok),
                  pl.BlockSpec((1, 1, 1, 3 * d), lambda b, i: (b, 0, 0, 0)),
                  pl.BlockSpec((1, tm, 1), tok),
                  pl.BlockSpec((1, LANES), const),
                  pl.BlockSpec((d, MLA_IN_COLS), const),
                  pl.BlockSpec((1, Q_LORA), const),
                  pl.BlockSpec((1, KV_LORA), const),
                  pl.BlockSpec((Q_LORA, 3 * nh * LANES), const),
                  pl.BlockSpec((KV_LORA, 2 * nh * LANES), const)],
        out_specs=[pl.BlockSpec((1, tm, nh * MLA_QK), tok),
                   pl.BlockSpec((1, tm, nh * MLA_QK), tok),
                   pl.BlockSpec((1, tm, nh * V_DIM), tok)],
        out_shape=[jax.ShapeDtypeStruct((bsz, s, nh * MLA_QK), BF16),
                   jax.ShapeDtypeStruct((bsz, s, nh * MLA_QK), BF16),
                   jax.ShapeDtypeStruct((bsz, s, nh * V_DIM), BF16)],
        compiler_params=_cparams("parallel", "parallel"),
        name="mla_in",
    )(x, mod, positions.reshape(bsz, s, 1), jnp.asarray(invf), win,
      q_a_norm.astype(F32).reshape(1, -1), kv_a_norm.astype(F32).reshape(1, -1), wq, wkv)


def _attn_kernel(q_ref, k_ref, v_ref, o_ref, m_ref, l_ref, acc_ref):
    tq = q_ref.shape[1]
    i = pl.program_id(2)
    m_ref[...] = jnp.full(m_ref.shape, -jnp.inf, F32)
    l_ref[...] = jnp.zeros(l_ref.shape, F32)
    acc_ref[...] = jnp.zeros(acc_ref.shape, F32)

    def update_rows(r0, nrows, off, width, diag_row):
        rows = slice(r0, r0 + nrows)
        sc = _dot_nt(q_ref[0, rows, :], k_ref[0, pl.ds(off, width), :])
        if diag_row is not None:
            qi = lax.broadcasted_iota(jnp.int32, sc.shape, 0) + (r0 - diag_row)
            ki = lax.broadcasted_iota(jnp.int32, sc.shape, 1)
            sc = jnp.where(ki <= qi, sc, -jnp.inf)
        m_old = m_ref[rows]
        m_new = jnp.maximum(m_old, jnp.max(sc, -1, keepdims=True))
        alpha = jnp.exp2(m_old - m_new)
        p = jnp.exp2(sc - pltpu.repeat(m_new, width // LANES, axis=1))
        l_ref[rows] = alpha * l_ref[rows] + jnp.sum(p, -1, keepdims=True)
        acc_ref[rows] = alpha * acc_ref[rows] + _dot(p.astype(BF16), v_ref[0, pl.ds(off, width), :])
        m_ref[rows] = m_new

    def update(r0, nrows, off, width, diag_row=None):
        for g in range(nrows // ATTN_ROWS):
            update_rows(r0 + g * ATTN_ROWS, ATTN_ROWS, off, width, diag_row)

    def full_block(j, carry):
        update(0, tq, pl.multiple_of(j * tq, tq), tq)
        return carry

    lax.fori_loop(0, i, full_block, 0)
    base = pl.multiple_of(i * tq, tq)
    for g in range(tq // ATTN_ROWS):
        r0 = g * ATTN_ROWS
        if g > 0:
            update_rows(r0, ATTN_ROWS, base, r0, None)
        update_rows(r0, ATTN_ROWS, base + r0, ATTN_ROWS, r0)
    o_ref[0] = (acc_ref[...] / l_ref[...]).astype(BF16)


def attention(q, k, v):
    bsz, s, _ = q.shape
    tq = min(ATTN_TILE, s)
    return pl.pallas_call(
        _attn_kernel,
        grid=(bsz, MLA_HEADS, s // tq),
        in_specs=[pl.BlockSpec((1, tq, MLA_QK), lambda b, h, i: (b, i, h)),
                  pl.BlockSpec((1, s, MLA_QK), lambda b, h, i: (b, 0, h)),
                  pl.BlockSpec((1, s, V_DIM), lambda b, h, i: (b, 0, h))],
        out_specs=pl.BlockSpec((1, tq, V_DIM), lambda b, h, i: (b, i, h)),
        out_shape=jax.ShapeDtypeStruct((bsz, s, MLA_HEADS * V_DIM), BF16),
        scratch_shapes=[pltpu.VMEM((tq, LANES), F32), pltpu.VMEM((tq, LANES), F32), pltpu.VMEM((tq, V_DIM), F32)],
        compiler_params=_cparams("parallel", "parallel", "arbitrary"),
        name="attention",
    )(q, k, v)


def kernel(x, c, positions, ada_w, ada_b, ln_g, ln_b, w_in_e, gdn_conv_w, gdn_a_log, gdn_dt_bias,
           gdn_norm_w, sc_conv_w, w_out_e, w_in_o, q_a_norm, w_q_b, kv_a_norm, w_kv_b, w_out_o,
           router_w, router_bias, w_gate, w_up, w_down):
    bsz, s, d = x.shape
    depth = ada_w.shape[0]
    mod = ada_mod(c, ada_w, ada_b).reshape(depth, 2, bsz, 1, 3 * d)

    def layers(x, mod, positions):
        for i in range(depth):
            j = i // 2
            moda, modb = mod[i, 0][:, None], mod[i, 1][:, None]
            if i % 2 == 0:
                o1, bg, bgt = even_in_proj(x, moda, w_in_e[j], gdn_a_log[j], gdn_dt_bias[j])
                qd, kd, u, w, ic, gl = gdn_prep(o1, bg, bgt, gdn_conv_w[j])
                o = gdn_scan(qd, kd, u, w, ic, gl)
                ymix = even_post(o, o1, gdn_norm_w[j], sc_conv_w[j])
                w_out = w_out_e[j]
            else:
                q, k, v = mla_in(x, moda, positions, w_in_o[j], q_a_norm[j], w_q_b[j], kv_a_norm[j],
                                 w_kv_b[j])
                ymix = attention(q, k, v)
                w_out = w_out_o[j]
            x1, hm, seli, selw, cnt = mix_out(ymix, w_out, x, moda, modb, ln_g[i, 0], ln_b[i, 0],
                                              router_w, router_bias)
            x = moe_layer(x1, hm, seli, selw, cnt, modb, ln_g[i, 1], ln_b[i, 1], w_gate, w_up, w_down, i)
        return x

    ng = BATCH_GROUPS if bsz % BATCH_GROUPS == 0 else 1
    gb = bsz // ng
    outs = [layers(x[g * gb:(g + 1) * gb], mod[:, :, g * gb:(g + 1) * gb], positions[g * gb:(g + 1) * gb])
            for g in range(ng)]
    return outs[0] if ng == 1 else jnp.concatenate(outs, axis=0)
```

```python
import functools

import numpy as np
import jax
import jax.numpy as jnp
from jax import lax
from jax.experimental import pallas as pl
from jax.experimental.pallas import tpu as pltpu

F32 = jnp.float32
BF16 = jnp.bfloat16
HIGHEST = lax.Precision.HIGHEST

D_MODEL = 1024
DEPTH = 2
DN_ALPHA = (2.0 * DEPTH) ** 0.25

GDN_HEADS = 4
GDN_HEAD_DIM = 128
GDN_WIDTH = GDN_HEADS * GDN_HEAD_DIM
GDN_CONV = 4
GDN_CHUNK = 64
SC_WIDTH = 512
SC_CONV = 3

MLA_HEADS = 8
Q_LORA = 384
KV_LORA = 256
QK_NOPE = 128
QK_ROPE = 64
V_DIM = 128
ROPE_THETA = 10000.0

N_EXPERTS = 32
N_GROUPS = 4
EXPERTS_PER_GROUP = N_EXPERTS // N_GROUPS
D_EXPERT = 512

NORM_EPS = 1e-6
LN_EPS = 1e-5

LANES = 128
HALO = 16
TOKEN_TILE = 512
GDN_TILE = 256
ATTN_TILE = 1024
ATTN_ROWS = 256
LOG2_E = 1.4426950408889634
MOE_BLOCK = 256
BATCH_GROUPS = 1
VMEM_LIMIT = 48 * 1024 * 1024


def _cparams(*sem):
    return pltpu.CompilerParams(dimension_semantics=sem, vmem_limit_bytes=VMEM_LIMIT)


def _sigmoid(x):
    return 1.0 / (1.0 + jnp.exp(-x))


def _silu(x):
    return x * _sigmoid(x)


def _dot(a, b):
    return jnp.dot(a, b, preferred_element_type=F32)


def _dot_nt(a, b, precision=None):
    return lax.dot_general(a, b, (((1,), (1,)), ((), ())), precision=precision,
                           preferred_element_type=F32)


def _dot_tn(a, b):
    return lax.dot_general(a, b, (((0,), (0,)), ((), ())), preferred_element_type=F32)


def _ada_kernel(c_ref, w_ref, b_ref, o_ref):
    cond = _silu(c_ref[...])
    o_ref[0] = jnp.dot(cond, w_ref[0], precision=HIGHEST, preferred_element_type=F32) + b_ref[0]


def ada_mod(c, ada_w, ada_b):
    nl = ada_w.shape[0] * ada_w.shape[1]
    bsz, d = c.shape
    w = ada_w.reshape(nl, d, 3 * d)
    b = ada_b.reshape(nl, 1, 3 * d)
    return pl.pallas_call(
        _ada_kernel,
        grid=(nl, 3),
        in_specs=[pl.BlockSpec((bsz, d), lambda l, j: (0, 0)),
                  pl.BlockSpec((1, d, d), lambda l, j: (l, 0, j)),
                  pl.BlockSpec((1, 1, d), lambda l, j: (l, 0, j))],
        out_specs=pl.BlockSpec((1, bsz, d), lambda l, j: (l, 0, j)),
        out_shape=jax.ShapeDtypeStruct((nl, bsz, 3 * d), F32),
        compiler_params=_cparams("parallel", "parallel"),
        name="ada_mod",
    )(c, w, b)


EVEN_MAIN = 3 * GDN_WIDTH + GDN_WIDTH + 3 * SC_WIDTH


def _even_in_kernel(x_ref, mod_ref, w1_ref, w2_ref, alog_ref, dtb_ref, o1_ref, o2_ref, o2t_ref):
    d = D_MODEL
    mod = mod_ref[0, 0]
    h = (x_ref[0] * (1.0 + mod[:, d:2 * d]) + mod[:, :d]).astype(BF16)
    for j in range(EVEN_MAIN // 512):
        o1_ref[0, :, j * 512:(j + 1) * 512] = _dot(h, w1_ref[:, j * 512:(j + 1) * 512]).astype(BF16)
    r = _dot(h, w2_ref[...])
    lane = lax.broadcasted_iota(jnp.int32, r.shape, 1)
    a = r + dtb_ref[...]
    softplus = jnp.maximum(a, 0.0) + jnp.log(1.0 + jnp.exp(-jnp.abs(a)))
    bg = jnp.where(lane < GDN_HEADS, _sigmoid(r), -jnp.exp(alog_ref[...]) * softplus)
    o2_ref[0] = bg
    o2t_ref[0] = bg.T[:8]


def even_in_proj(x, mod, w_in, a_log, dt_bias):
    bsz, s, d = x.shape
    tm = min(TOKEN_TILE, s)
    q_end = 4 * GDN_WIDTH
    w1 = jnp.concatenate([w_in[:, :q_end], w_in[:, q_end + 2 * GDN_HEADS:]], axis=1).astype(BF16)
    w2 = jnp.pad(w_in[:, q_end:q_end + 2 * GDN_HEADS], ((0, 0), (0, LANES - 2 * GDN_HEADS))).astype(BF16)
    alog = jnp.pad(a_log.astype(F32), (GDN_HEADS, LANES - 2 * GDN_HEADS)).reshape(1, LANES)
    dtb = jnp.pad(dt_bias.astype(F32), (GDN_HEADS, LANES - 2 * GDN_HEADS)).reshape(1, LANES)
    const = lambda b, i: (0, 0)
    return pl.pallas_call(
        _even_in_kernel,
        grid=(bsz, s // tm),
        in_specs=[pl.BlockSpec((1, tm, d), lambda b, i: (b, i, 0)),
                  pl.BlockSpec((1, 1, 1, 3 * d), lambda b, i: (b, 0, 0, 0)),
                  pl.BlockSpec((d, EVEN_MAIN), const),
                  pl.BlockSpec((d, LANES), const),
                  pl.BlockSpec((1, LANES), const),
                  pl.BlockSpec((1, LANES), const)],
        out_specs=[pl.BlockSpec((1, tm, EVEN_MAIN), lambda b, i: (b, i, 0)),
                   pl.BlockSpec((1, tm, LANES), lambda b, i: (b, i, 0)),
                   pl.BlockSpec((1, 8, tm), lambda b, i: (b, 0, i))],
        out_shape=[jax.ShapeDtypeStruct((bsz, s, EVEN_MAIN), BF16),
                   jax.ShapeDtypeStruct((bsz, s, LANES), F32),
                   jax.ShapeDtypeStruct((bsz, 8, s), F32)],
        compiler_params=_cparams("parallel", "parallel"),
        name="even_in_proj",
    )(x, mod, w1, w2, alog, dtb)


def _causal_conv(x, halo, w, taps):
    rows = x.shape[0]
    xf = jnp.concatenate([halo, x], axis=0)
    y = w[taps - 1:taps] * x
    for j in range(taps - 1):
        off = HALO - (taps - 1) + j
        y = y + w[j:j + 1] * xf[off:off + rows]
    return y


def _inv_unit_lower(low, xor_ij, block):
    eye = jnp.where(xor_ij == 0, 1.0, 0.0)
    m = eye - jnp.where(xor_ij == 1, low, 0.0)
    s = 2
    while s < block:
        shift = s.bit_length() - 1
        cs = jnp.where((xor_ij >> shift) == 1, low, 0.0).astype(BF16)
        mb = m.astype(BF16)
        m = m - _dot(mb, _dot(cs, mb).astype(BF16))
        s *= 2
    return m


def _gdn_prep_kernel(qkv_ref, halo_ref, cw_ref, bg_ref, bgt_ref, ltri_ref, utri_ref, same_ref,
                     qd_ref, kd_ref, u_ref, w_ref, ic_ref, gl_ref):
    ts = qkv_ref.shape[1]
    c = GDN_CHUNK
    hd = GDN_HEAD_DIM
    i = pl.program_id(1)
    x = qkv_ref[0].astype(F32)
    halo = jnp.where(i > 0, halo_ref[0].astype(F32), 0.0)
    y = _silu(_causal_conv(x, halo, cw_ref[...], GDN_CONV))

    bg = bg_ref[0]
    gc_col = jnp.dot(ltri_ref[...], bg, precision=HIGHEST, preferred_element_type=F32)
    gc_row = jnp.dot(bgt_ref[0], utri_ref[...], precision=HIGHEST, preferred_element_type=F32)
    gc_end = jnp.dot(same_ref[...], bg, precision=HIGHEST, preferred_element_type=F32)

    ii = lax.broadcasted_iota(jnp.int32, (ts, ts), 0)
    jj = lax.broadcasted_iota(jnp.int32, (ts, ts), 1)
    xor_ij = ii ^ jj
    causal = (same_ref[...] > 0.0) & (ii >= jj)
    diag = xor_ij == 0

    for h in range(GDN_HEADS):
        q = y[:, h * hd:(h + 1) * hd]
        k = y[:, GDN_WIDTH + h * hd:GDN_WIDTH + (h + 1) * hd]
        v = y[:, 2 * GDN_WIDTH + h * hd:2 * GDN_WIDTH + (h + 1) * hd]
        q = q * lax.rsqrt(jnp.sum(q * q, -1, keepdims=True) + NORM_EPS) * (hd ** -0.5)
        k = k * lax.rsqrt(jnp.sum(k * k, -1, keepdims=True) + NORM_EPS)
        beta = bg[:, h:h + 1]
        gcol = gc_col[:, GDN_HEADS + h:GDN_HEADS + h + 1]
        grow = gc_row[GDN_HEADS + h:GDN_HEADS + h + 1, :]
        gend = gc_end[:, GDN_HEADS + h:GDN_HEADS + h + 1]
        decay = jnp.exp(jnp.where(causal, gcol - grow, -jnp.inf))
        kb = k * beta
        kbf = k.astype(BF16)
        low = jnp.where(diag, 0.0, _dot_nt(kb.astype(BF16), kbf) * decay)
        intra = _dot_nt(q.astype(BF16), kbf) * decay
        egc = jnp.exp(gcol)
        rhs = jnp.concatenate([v * beta, kb * egc], axis=-1).astype(BF16)
        sol = _dot(_inv_unit_lower(low, xor_ij, c).astype(BF16), rhs)
        cols = slice(h * hd, (h + 1) * hd)
        u_ref[0, :, cols] = sol[:, :hd]
        w_ref[0, :, cols] = sol[:, hd:].astype(BF16)
        qd_ref[0, :, cols] = (q * egc).astype(BF16)
        kd_ref[0, :, cols] = (k * jnp.exp(gend - gcol)).astype(BF16)
        packed = intra[:, :c]
        for n in range(1, ts // c):
            packed = packed + intra[:, n * c:(n + 1) * c]
        ic_ref[0, :, h * c:(h + 1) * c] = packed.astype(BF16)
        for n in range(ts // c):
            gl_ref[0, n, h:h + 1, :] = jnp.broadcast_to(jnp.exp(gend[n * c:n * c + 1]), (1, hd))


def gdn_prep(o1, bg, bgt, conv_w):
    bsz, s, _ = o1.shape
    ts = min(GDN_TILE, s)
    c = GDN_CHUNK
    nc = ts // c
    qkv_w = 3 * GDN_WIDTH
    r = np.arange(ts)
    same = (r[:, None] // c) == (r[None, :] // c)
    ltri = jnp.asarray((same & (r[:, None] >= r[None, :])).astype(np.float32))
    utri = jnp.asarray((same & (r[:, None] <= r[None, :])).astype(np.float32))
    same = jnp.asarray(same.astype(np.float32))
    hb = ts // HALO
    tok = lambda b, i: (b, i, 0)
    const = lambda b, i: (0, 0)
    wide = jax.ShapeDtypeStruct((bsz, s, GDN_WIDTH), BF16)
    return pl.pallas_call(
        _gdn_prep_kernel,
        grid=(bsz, s // ts),
        in_specs=[pl.BlockSpec((1, ts, qkv_w), tok),
                  pl.BlockSpec((1, HALO, qkv_w), lambda b, i: (b, jnp.maximum(i * hb - 1, 0), 0)),
                  pl.BlockSpec((GDN_CONV, qkv_w), const),
                  pl.BlockSpec((1, ts, LANES), tok),
                  pl.BlockSpec((1, 8, ts), lambda b, i: (b, 0, i)),
                  pl.BlockSpec((ts, ts), const),
                  pl.BlockSpec((ts, ts), const),
                  pl.BlockSpec((ts, ts), const)],
        out_specs=[pl.BlockSpec((1, ts, GDN_WIDTH), tok),
                   pl.BlockSpec((1, ts, GDN_WIDTH), tok),
                   pl.BlockSpec((1, ts, GDN_WIDTH), tok),
                   pl.BlockSpec((1, ts, GDN_WIDTH), tok),
                   pl.BlockSpec((1, ts, GDN_HEADS * c), tok),
                   pl.BlockSpec((1, nc, GDN_HEADS, GDN_HEAD_DIM), lambda b, i: (b, i, 0, 0))],
        out_shape=[wide, wide, jax.ShapeDtypeStruct((bsz, s, GDN_WIDTH), F32), wide,
                   jax.ShapeDtypeStruct((bsz, s, GDN_HEADS * c), BF16),
                   jax.ShapeDtypeStruct((bsz, s // c, GDN_HEADS, GDN_HEAD_DIM), F32)],
        compiler_params=_cparams("parallel", "parallel"),
        name="gdn_prep",
    )(o1, o1, conv_w.astype(F32), bg, bgt, ltri, utri, same)


def _gdn_scan_kernel(qd_ref, kd_ref, u_ref, w_ref, ic_ref, gl_ref, o_ref, state_ref):
    ts = qd_ref.shape[1]
    c = GDN_CHUNK
    hd = GDN_HEAD_DIM

    @pl.when(pl.program_id(1) == 0)
    def _():
        state_ref[...] = jnp.zeros_like(state_ref)

    for n in range(ts // c):
        rows = slice(n * c, (n + 1) * c)
        for h in range(GDN_HEADS):
            cols = slice(h * hd, (h + 1) * hd)
            st = state_ref[h]
            stb = st.astype(BF16)
            v_new = u_ref[0, rows, cols] - _dot(w_ref[0, rows, cols], stb)
            vb = v_new.astype(BF16)
            o_ref[0, rows, cols] = _dot(qd_ref[0, rows, cols], stb) + _dot(ic_ref[0, rows, h * c:(h + 1) * c], vb)
            state_ref[h] = st * gl_ref[0, n, h:h + 1, :] + _dot_tn(kd_ref[0, rows, cols], vb)


def gdn_scan(qd, kd, u, w, ic, gl):
    bsz, s, _ = qd.shape
    ts = min(GDN_TILE, s)
    c = GDN_CHUNK
    tok = lambda b, i: (b, i, 0)
    return pl.pallas_call(
        _gdn_scan_kernel,
        grid=(bsz, s // ts),
        in_specs=[pl.BlockSpec((1, ts, GDN_WIDTH), tok)] * 4 + [
            pl.BlockSpec((1, ts, GDN_HEADS * c), tok),
            pl.BlockSpec((1, ts // c, GDN_HEADS, GDN_HEAD_DIM), lambda b, i: (b, i, 0, 0))],
        out_specs=pl.BlockSpec((1, ts, GDN_WIDTH), tok),
        out_shape=jax.ShapeDtypeStruct((bsz, s, GDN_WIDTH), F32),
        scratch_shapes=[pltpu.VMEM((GDN_HEADS, GDN_HEAD_DIM, GDN_HEAD_DIM), F32)],
        compiler_params=_cparams("parallel", "arbitrary"),
        name="gdn_scan",
    )(qd, kd, u, w, ic, gl)


def _even_post_kernel(o_ref, z_ref, scb_ref, scc_ref, sch_ref, scc_halo_ref, sch_halo_ref,
                      nw_ref, cw_ref, y_ref):
    hd = GDN_HEAD_DIM
    i = pl.program_id(1)
    o = o_ref[0]
    z = z_ref[0].astype(F32)
    nw = nw_ref[...]
    for h in range(GDN_HEADS):
        cols = slice(h * hd, (h + 1) * hd)
        oh = o[:, cols]
        on = oh * lax.rsqrt(jnp.mean(oh * oh, -1, keepdims=True) + NORM_EPS) * nw
        y_ref[0, :, cols] = (on * _silu(z[:, cols])).astype(BF16)
    ch = scc_ref[0].astype(F32) * sch_ref[0].astype(F32)
    ch_halo = jnp.where(i > 0, scc_halo_ref[0].astype(F32) * sch_halo_ref[0].astype(F32), 0.0)
    yb = scb_ref[0].astype(F32) * _causal_conv(ch, ch_halo, cw_ref[...], SC_CONV)
    y_ref[0, :, GDN_WIDTH:] = yb.astype(BF16)


def even_post(o, o1, norm_w, sc_conv_w):
    bsz, s, _ = o.shape
    tm = min(TOKEN_TILE, s)
    hb = tm // HALO
    wb = 512
    blk = lambda j: pl.BlockSpec((1, tm, wb), lambda b, i, j=j: (b, i, j))
    halo = lambda j: pl.BlockSpec((1, HALO, wb), lambda b, i, j=j: (b, jnp.maximum(i * hb - 1, 0), j))
    const = lambda b, i: (0, 0)
    return pl.pallas_call(
        _even_post_kernel,
        grid=(bsz, s // tm),
        in_specs=[pl.BlockSpec((1, tm, GDN_WIDTH), lambda b, i: (b, i, 0)),
                  blk(3), blk(4), blk(5), blk(6), halo(5), halo(6),
                  pl.BlockSpec((1, GDN_HEAD_DIM), const),
                  pl.BlockSpec((SC_CONV, SC_WIDTH), const)],
        out_specs=pl.BlockSpec((1, tm, GDN_WIDTH + SC_WIDTH), lambda b, i: (b, i, 0)),
        out_shape=jax.ShapeDtypeStruct((bsz, s, GDN_WIDTH + SC_WIDTH), BF16),
        compiler_params=_cparams("parallel", "parallel"),
        name="even_post",
    )(o, o1, o1, o1, o1, o1, o1, norm_w.astype(F32).reshape(1, -1), sc_conv_w.astype(F32))


def _layer_norm(r, g, b):
    mu = jnp.mean(r, -1, keepdims=True)
    rc = r - mu
    var = jnp.mean(rc * rc, -1, keepdims=True)
    return rc * lax.rsqrt(var + LN_EPS) * g + b


def _route(logits_t, bias):
    scores = _sigmoid(logits_t)
    biased = scores + bias
    t = logits_t.shape[1]
    epg = EXPERTS_PER_GROUP
    sub = lax.broadcasted_iota(jnp.int32, (epg, t), 0).astype(F32)
    best = None
    for g in range(N_GROUPS):
        bgp = biased[g * epg:(g + 1) * epg]
        m1 = jnp.max(bgp, axis=0, keepdims=True)
        i1 = jnp.min(jnp.where(bgp == m1, sub, float(epg)), axis=0, keepdims=True)
        rest = jnp.where(sub == i1, -jnp.inf, bgp)
        m2 = jnp.max(rest, axis=0, keepdims=True)
        i2 = jnp.min(jnp.where(rest == m2, sub, float(epg)), axis=0, keepdims=True)
        gs = m1 + m2
        if best is None:
            best, e0, e1 = gs, i1, i2
        else:
            better = gs > best
            best = jnp.where(better, gs, best)
            e0 = jnp.where(better, float(g * epg) + i1, e0)
            e1 = jnp.where(better, float(g * epg) + i2, e1)
    eio = lax.broadcasted_iota(jnp.int32, scores.shape, 0).astype(F32)
    hit0 = eio == e0
    hit1 = eio == e1
    s0 = jnp.sum(jnp.where(hit0, scores, 0.0), axis=0, keepdims=True)
    s1 = jnp.sum(jnp.where(hit1, scores, 0.0), axis=0, keepdims=True)
    tot = s0 + s1
    return e0.astype(jnp.int32), e1.astype(jnp.int32), s0 / tot, s1 / tot, hit0, hit1


def _mix_out_kernel(y_ref, wo_ref, x_ref, moda_ref, modb_ref, lng_ref, lnb_ref, rwt_ref, rb_ref,
                    upper_ref, x1_ref, hm_ref, seli_ref, selw_ref, cnt_ref):
    d = D_MODEL
    gate = moda_ref[0, 0][:, 2 * d:]
    modb = modb_ref[0, 0]
    y = _dot(y_ref[0], wo_ref[...])
    x1 = _layer_norm(DN_ALPHA * x_ref[0] + (1.0 + gate) * y, lng_ref[...], lnb_ref[...])
    x1_ref[0] = x1
    hm = x1 * (1.0 + modb[:, d:2 * d]) + modb[:, :d]
    hm_ref[0] = hm.astype(BF16)

    logits_t = _dot_nt(rwt_ref[...], hm, precision=HIGHEST)
    e0, e1, w0, w1, hit0, hit1 = _route(logits_t, rb_ref[...])
    member = jnp.where(hit0 | hit1, 1.0, 0.0)
    before = _dot(member.astype(BF16), upper_ref[...])
    rank0 = jnp.sum(jnp.where(hit0, before, 0.0), axis=0, keepdims=True).astype(jnp.int32)
    rank1 = jnp.sum(jnp.where(hit1, before, 0.0), axis=0, keepdims=True).astype(jnp.int32)
    zi = jnp.zeros_like(e0)
    seli_ref[0] = jnp.concatenate([e0, e1, rank0, rank1, zi, zi, zi, zi], axis=0)
    zf = jnp.zeros_like(w0)
    selw_ref[0] = jnp.concatenate([w0, w1, zf, zf, zf, zf, zf, zf], axis=0)
    cnt_ref[0, 0] = jnp.broadcast_to(jnp.sum(member, axis=1, keepdims=True), (N_EXPERTS, LANES))


def mix_out(ymix, w_out, x, moda, modb, ln_g, ln_b, router_w, router_bias):
    bsz, s, d = x.shape
    tm = min(TOKEN_TILE, s)
    nt = s // tm
    r = np.arange(tm)
    upper = jnp.asarray((r[:, None] < r[None, :]).astype(np.float32), dtype=BF16)
    tok = lambda b, i: (b, i, 0)
    const = lambda b, i: (0, 0)
    modspec = pl.BlockSpec((1, 1, 1, 3 * d), lambda b, i: (b, 0, 0, 0))
    row8 = pl.BlockSpec((1, 8, tm), lambda b, i: (b, 0, i))
    return pl.pallas_call(
        _mix_out_kernel,
        grid=(bsz, nt),
        in_specs=[pl.BlockSpec((1, tm, d), tok),
                  pl.BlockSpec((d, d), const),
                  pl.BlockSpec((1, tm, d), tok),
                  modspec, modspec,
                  pl.BlockSpec((1, d), const), pl.BlockSpec((1, d), const),
                  pl.BlockSpec((N_EXPERTS, d), const),
                  pl.BlockSpec((N_EXPERTS, 1), const),
                  pl.BlockSpec((tm, tm), const)],
        out_specs=[pl.BlockSpec((1, tm, d), tok),
                   pl.BlockSpec((1, tm, d), tok),
                   row8, row8,
                   pl.BlockSpec((1, 1, N_EXPERTS, LANES), lambda b, i: (b, i, 0, 0))],
        out_shape=[jax.ShapeDtypeStruct((bsz, s, d), F32),
                   jax.ShapeDtypeStruct((bsz, s, d), BF16),
                   jax.ShapeDtypeStruct((bsz, 8, s), jnp.int32),
                   jax.ShapeDtypeStruct((bsz, 8, s), F32),
                   jax.ShapeDtypeStruct((bsz, nt, N_EXPERTS, LANES), F32)],
        compiler_params=_cparams("parallel", "parallel"),
        name="mix_out",
    )(ymix, w_out.astype(BF16), x, moda, modb, ln_g.reshape(1, d), ln_b.reshape(1, d),
      router_w.T.astype(F32), router_bias.astype(F32).reshape(N_EXPERTS, 1), upper)


def _experts_kernel(be_ref, nused_ref, xs_ref, wg_ref, wu_ref, wd_ref, ys_ref, wgb_ref, wub_ref, wdb_ref):
    i = pl.program_id(0)

    @pl.when((i == 0) | (be_ref[i] != be_ref[jnp.maximum(i - 1, 0)]))
    def _():
        wgb_ref[...] = wg_ref[0, 0].astype(BF16)
        wub_ref[...] = wu_ref[0, 0].astype(BF16)
        wdb_ref[...] = wd_ref[0, 0].astype(BF16)

    @pl.when(i < nused_ref[0])
    def _():
        x = xs_ref[...]
        act = (_silu(_dot(x, wgb_ref[...])) * _dot(x, wub_ref[...])).astype(BF16)
        ys_ref[...] = _dot(act, wdb_ref[...]).astype(BF16)

    @pl.when(i >= nused_ref[0])
    def _():
        ys_ref[...] = jnp.zeros_like(ys_ref)


def moe_experts(xs, block_expert, n_used, w_gate, w_up, w_down, layer):
    rows, d = xs.shape
    nb = rows // MOE_BLOCK
    by_expert = lambda i, be, nu: (layer, be[i], 0, 0)
    return pl.pallas_call(
        _experts_kernel,
        grid_spec=pltpu.PrefetchScalarGridSpec(
            num_scalar_prefetch=2,
            grid=(nb,),
            in_specs=[pl.BlockSpec((MOE_BLOCK, d), lambda i, be, nu: (i, 0)),
                      pl.BlockSpec((1, 1, d, D_EXPERT), by_expert),
                      pl.BlockSpec((1, 1, d, D_EXPERT), by_expert),
                      pl.BlockSpec((1, 1, D_EXPERT, d), by_expert)],
            out_specs=pl.BlockSpec((MOE_BLOCK, d), lambda i, be, nu: (i, 0)),
            scratch_shapes=[pltpu.VMEM((d, D_EXPERT), BF16), pltpu.VMEM((d, D_EXPERT), BF16),
                            pltpu.VMEM((D_EXPERT, d), BF16)]),
        out_shape=jax.ShapeDtypeStruct((rows, d), BF16),
        compiler_params=_cparams("arbitrary"),
        name="moe_experts",
    )(block_expert, n_used, xs, w_gate, w_up, w_down)


def _plan_kernel(seli_ref, base_ref, dest_ref):
    sel = seli_ref[0]
    base = base_ref[0, 0]
    eio = lax.broadcasted_iota(jnp.int32, (N_EXPERTS, sel.shape[1]), 0)
    out = []
    for kk in range(2):
        first = jnp.sum(jnp.where(eio == sel[kk:kk + 1], base, 0.0), axis=0, keepdims=True)
        out.append(first.astype(jnp.int32) + sel[2 + kk:3 + kk])
    zero = jnp.zeros_like(out[0])
    dest_ref[0] = jnp.concatenate(out + [zero] * 6, axis=0)


def dispatch_plan(seli, base):
    bsz, _, s = seli.shape
    nt = base.shape[1]
    tm = s // nt
    return pl.pallas_call(
        _plan_kernel,
        grid=(bsz, nt),
        in_specs=[pl.BlockSpec((1, 8, tm), lambda b, i: (b, 0, i)),
                  pl.BlockSpec((1, 1, N_EXPERTS, 1), lambda b, i: (b, i, 0, 0))],
        out_specs=pl.BlockSpec((1, 8, tm), lambda b, i: (b, 0, i)),
        out_shape=jax.ShapeDtypeStruct((bsz, 8, s), jnp.int32),
        compiler_params=_cparams("parallel", "parallel"),
        name="dispatch_plan",
    )(seli, base)


def _invert_kernel(dest_ref, pad_lo_ref, pad_hi_ref, slot_tok_ref):
    t = dest_ref.shape[0] // 2

    def clear_range(e, carry):
        def clear(r, c):
            slot_tok_ref[r] = 0
            return c
        return lax.fori_loop(pad_lo_ref[e], pad_hi_ref[e], clear, carry)

    def put(a, carry):
        slot_tok_ref[dest_ref[a]] = a
        slot_tok_ref[dest_ref[t + a]] = a
        return carry

    lax.fori_loop(0, pad_lo_ref.shape[0], clear_range, 0)
    lax.fori_loop(0, t, put, 0, unroll=8)


def invert_slots(dest_flat, pad_lo, pad_hi, rows):
    smem = pl.BlockSpec(memory_space=pltpu.SMEM)
    return pl.pallas_call(
        _invert_kernel,
        in_specs=[smem, smem, smem],
        out_specs=smem,
        out_shape=jax.ShapeDtypeStruct((rows,), jnp.int32),
        name="invert_slots",
    )(dest_flat, pad_lo, pad_hi)


def _moe_combine_kernel(g0_ref, g1_ref, wt_ref, x_ref, mod_ref, lng_ref, lnb_ref, o_ref):
    d = D_MODEL
    gate = mod_ref[0, 0][:, 2 * d:]
    wt = wt_ref[0]
    y = g0_ref[0, 0].astype(F32) * wt[:, 0:1] + g1_ref[0, 0].astype(F32) * wt[:, 1:2]
    o_ref[0] = _layer_norm(DN_ALPHA * x_ref[0] + (1.0 + gate) * y, lng_ref[...], lnb_ref[...])


def moe_combine(g, wt, x1, modb, ln_g, ln_b):
    bsz, s, d = x1.shape
    tm = min(TOKEN_TILE, s)
    tok = lambda b, i: (b, i, 0)
    const = lambda b, i: (0, 0)
    return pl.pallas_call(
        _moe_combine_kernel,
        grid=(bsz, s // tm),
        in_specs=[pl.BlockSpec((1, 1, tm, d), lambda b, i: (0, b, i, 0)),
                  pl.BlockSpec((1, 1, tm, d), lambda b, i: (1, b, i, 0)),
                  pl.BlockSpec((1, tm, 8), tok), pl.BlockSpec((1, tm, d), tok),
                  pl.BlockSpec((1, 1, 1, 3 * d), lambda b, i: (b, 0, 0, 0)),
                  pl.BlockSpec((1, d), const), pl.BlockSpec((1, d), const)],
        out_specs=pl.BlockSpec((1, tm, d), tok),
        out_shape=jax.ShapeDtypeStruct((bsz, s, d), F32),
        compiler_params=_cparams("parallel", "parallel"),
        name="moe_combine",
    )(g, g, wt, x1, modb, ln_g.reshape(1, d), ln_b.reshape(1, d))


def moe_layer(x1, hm, seli, selw, cnt, modb, ln_g, ln_b, w_gate, w_up, w_down, layer):
    bsz, s, d = x1.shape
    t = bsz * s
    tm = min(TOKEN_TILE, s)
    blk = MOE_BLOCK
    cnt = cnt[..., 0].reshape(-1, N_EXPERTS).astype(jnp.int32)
    tile_off = jnp.cumsum(cnt, axis=0) - cnt
    counts = cnt.sum(0)
    padded = (counts + blk - 1) // blk * blk
    pend = jnp.cumsum(padded)
    pstart = pend - padded
    nb = -(-(2 * t) // blk) + N_EXPERTS
    rows = nb * blk
    base = (pstart[None, :] + tile_off).astype(F32).reshape(bsz, s // tm, N_EXPERTS, 1)
    dest = dispatch_plan(seli, base)
    dest = jnp.swapaxes(dest[:, 0:2, :], 0, 1).reshape(2 * t)
    pad_lo = jnp.concatenate([pstart + counts, pend[-1:]]).astype(jnp.int32)
    pad_hi = jnp.concatenate([pend, jnp.full((1,), rows)]).astype(jnp.int32)
    slot_tok = invert_slots(dest, pad_lo, pad_hi, rows)
    starts = jnp.arange(nb, dtype=jnp.int32) * blk
    block_expert = jnp.minimum(jnp.sum((pend[None, :] <= starts[:, None]).astype(jnp.int32), axis=1),
                               N_EXPERTS - 1)
    n_used = (pend[-1] // blk).astype(jnp.int32).reshape(1)
    xs = hm.reshape(t, d).at[slot_tok].get(mode="promise_in_bounds")
    ys = moe_experts(xs, block_expert, n_used, w_gate, w_up, w_down, layer)
    g = ys.at[dest].get(mode="promise_in_bounds").reshape(2, bsz, s, d)
    wt = jnp.swapaxes(selw, 1, 2)
    return moe_combine(g, wt, x1, modb, ln_g, ln_b)


MLA_IN_COLS = Q_LORA + KV_LORA + 2 * LANES
MLA_QK = 2 * LANES


def _mla_in_kernel(x_ref, mod_ref, pos_ref, invf_ref, win_ref, qn_ref, kvn_ref, wq_ref, wkv_ref,
                   q_ref, k_ref, v_ref):
    d = D_MODEL
    nh = MLA_HEADS
    mod = mod_ref[0, 0]
    h = (x_ref[0] * (1.0 + mod[:, d:2 * d]) + mod[:, :d]).astype(BF16)
    proj = _dot(h, win_ref[...])
    ang = pos_ref[0].astype(F32) * invf_ref[...]
    lane = lax.broadcasted_iota(jnp.int32, ang.shape, 1)
    cos = jnp.where(lane < QK_ROPE, jnp.cos(ang), 0.0)
    sin = jnp.where(lane < QK_ROPE, jnp.sin(ang), 0.0)

    qa = proj[:, :Q_LORA]
    qa = (qa * lax.rsqrt(jnp.mean(qa * qa, -1, keepdims=True) + NORM_EPS) * qn_ref[...]).astype(BF16)
    kva = proj[:, Q_LORA:Q_LORA + KV_LORA]
    kva = (kva * lax.rsqrt(jnp.mean(kva * kva, -1, keepdims=True) + NORM_EPS) * kvn_ref[...]).astype(BF16)
    kr0 = Q_LORA + KV_LORA
    k_rope = (proj[:, kr0:kr0 + LANES] * cos + proj[:, kr0 + LANES:kr0 + 2 * LANES] * sin).astype(BF16)

    scale = (QK_NOPE + QK_ROPE) ** -0.5 * LOG2_E
    hw = nh * LANES
    q_nope = _dot(qa, wq_ref[:, :hw])
    q_rope = _dot(qa, wq_ref[:, hw:2 * hw])
    q_rot = _dot(qa, wq_ref[:, 2 * hw:])
    k_nope = _dot(kva, wkv_ref[:, :hw])
    v_ref[0] = _dot(kva, wkv_ref[:, hw:]).astype(BF16)
    for hh in range(nh):
        cols = slice(hh * LANES, (hh + 1) * LANES)
        q_ref[0, :, hh * MLA_QK:hh * MLA_QK + LANES] = (q_nope[:, cols] * scale).astype(BF16)
        q_ref[0, :, hh * MLA_QK + LANES:(hh + 1) * MLA_QK] = (
            (q_rope[:, cols] * cos + q_rot[:, cols] * sin) * scale).astype(BF16)
        k_ref[0, :, hh * MLA_QK:hh * MLA_QK + LANES] = k_nope[:, cols].astype(BF16)
        k_ref[0, :, hh * MLA_QK + LANES:(hh + 1) * MLA_QK] = k_rope


def _rope_cols(w):
    half = QK_ROPE // 2
    pad = [(0, 0)] * (w.ndim - 1) + [(0, LANES - QK_ROPE)]
    rot = jnp.concatenate([-w[..., half:], w[..., :half]], axis=-1)
    return jnp.pad(w, pad), jnp.pad(rot, pad)


def mla_in(x, mod, positions, w_in, q_a_norm, w_q_b, kv_a_norm, w_kv_b):
    bsz, s, d = x.shape
    tm = min(TOKEN_TILE, s)
    nh = MLA_HEADS
    kr, kr_rot = _rope_cols(w_in[:, Q_LORA + KV_LORA:])
    win = jnp.concatenate([w_in[:, :Q_LORA + KV_LORA], kr, kr_rot], axis=1).astype(BF16)
    wq = w_q_b.reshape(Q_LORA, nh, QK_NOPE + QK_ROPE)
    qr, qr_rot = _rope_cols(wq[..., QK_NOPE:])
    wq = jnp.concatenate([wq[..., :QK_NOPE].reshape(Q_LORA, -1), qr.reshape(Q_LORA, -1),
                          qr_rot.reshape(Q_LORA, -1)], axis=1).astype(BF16)
    wkv = w_kv_b.reshape(KV_LORA, nh, QK_NOPE + V_DIM)
    wkv = jnp.concatenate([wkv[..., :QK_NOPE].reshape(KV_LORA, -1),
                           wkv[..., QK_NOPE:].reshape(KV_LORA, -1)], axis=1).astype(BF16)
    inv = ROPE_THETA ** (-np.arange(0, QK_ROPE, 2, dtype=np.float32) / QK_ROPE)
    invf = np.zeros((1, LANES), np.float32)
    invf[0, :QK_ROPE] = np.concatenate([inv, inv])
    tok = lambda b, i: (b, i, 0)
    const = lambda b, i: (0, 0)
    return pl.pallas_call(
        _mla_in_kernel,
        grid=(bsz, s // tm),
        in_specs=[pl.BlockSpec((1, tm, d), tok),
                  pl.BlockSpec((1, 1, 1, 3 * d), lambda b, i: (b, 0, 0, 0)),
                  pl.BlockSpec((1, tm, 1), tok),
                  pl.BlockSpec((1, LANES), const),
                  pl.BlockSpec((d, MLA_IN_COLS), const),
                  pl.BlockSpec((1, Q_LORA), const),
                  pl.BlockSpec((1, KV_LORA), const),
                  pl.BlockSpec((Q_LORA, 3 * nh * LANES), const),
                  pl.BlockSpec((KV_LORA, 2 * nh * LANES), const)],
        out_specs=[pl.BlockSpec((1, tm, nh * MLA_QK), tok),
                   pl.BlockSpec((1, tm, nh * MLA_QK), tok),
                   pl.BlockSpec((1, tm, nh * V_DIM), tok)],
        out_shape=[jax.ShapeDtypeStruct((bsz, s, nh * MLA_QK), BF16),
                   jax.ShapeDtypeStruct((bsz, s, nh * MLA_QK), BF16),
                   jax.ShapeDtypeStruct((bsz, s, nh * V_DIM), BF16)],
        compiler_params=_cparams("parallel", "parallel"),
        name="mla_in",
    )(x, mod, positions.reshape(bsz, s, 1), jnp.asarray(invf), win,
      q_a_norm.astype(F32).reshape(1, -1), kv_a_norm.astype(F32).reshape(1, -1), wq, wkv)


def _attn_kernel(q_ref, k_ref, v_ref, o_ref, m_ref, l_ref, acc_ref):
    tq = q_ref.shape[1]
    i = pl.program_id(2)
    m_ref[...] = jnp.full(m_ref.shape, -jnp.inf, F32)
    l_ref[...] = jnp.zeros(l_ref.shape, F32)
    acc_ref[...] = jnp.zeros(acc_ref.shape, F32)

    def update_rows(r0, nrows, off, width, diag_row):
        rows = slice(r0, r0 + nrows)
        sc = _dot_nt(q_ref[0, rows, :], k_ref[0, pl.ds(off, width), :])
        if diag_row is not None:
            qi = lax.broadcasted_iota(jnp.int32, sc.shape, 0) + (r0 - diag_row)
            ki = lax.broadcasted_iota(jnp.int32, sc.shape, 1)
            sc = jnp.where(ki <= qi, sc, -jnp.inf)
        m_old = m_ref[rows]
        m_new = jnp.maximum(m_old, jnp.max(sc, -1, keepdims=True))
        alpha = jnp.exp2(m_old - m_new)
        p = jnp.exp2(sc - pltpu.repeat(m_new, width // LANES, axis=1))
        l_ref[rows] = alpha * l_ref[rows] + jnp.sum(p, -1, keepdims=True)
        acc_ref[rows] = alpha * acc_ref[rows] + _dot(p.astype(BF16), v_ref[0, pl.ds(off, width), :])
        m_ref[rows] = m_new

    def update(r0, nrows, off, width, diag_row=None):
        for g in range(nrows // ATTN_ROWS):
            update_rows(r0 + g * ATTN_ROWS, ATTN_ROWS, off, width, diag_row)

    def full_block(j, carry):
        update(0, tq, pl.multiple_of(j * tq, tq), tq)
        return carry

    lax.fori_loop(0, i, full_block, 0)
    base = pl.multiple_of(i * tq, tq)
    for g in range(tq // ATTN_ROWS):
        r0 = g * ATTN_ROWS
        if g > 0:
            update_rows(r0, ATTN_ROWS, base, r0, None)
        update_rows(r0, ATTN_ROWS, base + r0, ATTN_ROWS, r0)
    o_ref[0] = (acc_ref[...] / l_ref[...]).astype(BF16)


def attention(q, k, v):
    bsz, s, _ = q.shape
    tq = min(ATTN_TILE, s)
    return pl.pallas_call(
        _attn_kernel,
        grid=(bsz, MLA_HEADS, s // tq),
        in_specs=[pl.BlockSpec((1, tq, MLA_QK), lambda b, h, i: (b, i, h)),
                  pl.BlockSpec((1, s, MLA_QK), lambda b, h, i: (b, 0, h)),
                  pl.BlockSpec((1, s, V_DIM), lambda b, h, i: (b, 0, h))],
        out_specs=pl.BlockSpec((1, tq, V_DIM), lambda b, h, i: (b, i, h)),
        out_shape=jax.ShapeDtypeStruct((bsz, s, MLA_HEADS * V_DIM), BF16),
        scratch_shapes=[pltpu.VMEM((tq, LANES), F32), pltpu.VMEM((tq, LANES), F32), pltpu.VMEM((tq, V_DIM), F32)],
        compiler_params=_cparams("parallel", "parallel", "arbitrary"),
        name="attention",
    )(q, k, v)


def kernel(x, c, positions, ada_w, ada_b, ln_g, ln_b, w_in_e, gdn_conv_w, gdn_a_log, gdn_dt_bias,
           gdn_norm_w, sc_conv_w, w_out_e, w_in_o, q_a_norm, w_q_b, kv_a_norm, w_kv_b, w_out_o,
           router_w, router_bias, w_gate, w_up, w_down):
    bsz, s, d = x.shape
    depth = ada_w.shape[0]
    mod = ada_mod(c, ada_w, ada_b).reshape(depth, 2, bsz, 1, 3 * d)

    def layers(x, mod, positions):
        for i in range(depth):
            j = i // 2
            moda, modb = mod[i, 0][:, None], mod[i, 1][:, None]
            if i % 2 == 0:
                o1, bg, bgt = even_in_proj(x, moda, w_in_e[j], gdn_a_log[j], gdn_dt_bias[j])
                qd, kd, u, w, ic, gl = gdn_prep(o1, bg, bgt, gdn_conv_w[j])
                o = gdn_scan(qd, kd, u, w, ic, gl)
                ymix = even_post(o, o1, gdn_norm_w[j], sc_conv_w[j])
                w_out = w_out_e[j]
            else:
                q, k, v = mla_in(x, moda, positions, w_in_o[j], q_a_norm[j], w_q_b[j], kv_a_norm[j],
                                 w_kv_b[j])
                ymix = attention(q, k, v)
                w_out = w_out_o[j]
            x1, hm, seli, selw, cnt = mix_out(ymix, w_out, x, moda, modb, ln_g[i, 0], ln_b[i, 0],
                                              router_w, router_bias)
            x = moe_layer(x1, hm, seli, selw, cnt, modb, ln_g[i, 1], ln_b[i, 1], w_gate, w_up, w_down, i)
        return x

    ng = BATCH_GROUPS if bsz % BATCH_GROUPS == 0 else 1
    gb = bsz // ng
    outs = [layers(x[g * gb:(g + 1) * gb], mod[:, :, g * gb:(g + 1) * gb], positions[g * gb:(g + 1) * gb])
            for g in range(ng)]
    return outs[0] if ng == 1 else jnp.concatenate(outs, axis=0)
```

```python
import functools

import numpy as np
import jax
import jax.numpy as jnp
from jax import lax
from jax.experimental import pallas as pl
from jax.experimental.pallas import tpu as pltpu

F32 = jnp.float32
BF16 = jnp.bfloat16
HIGHEST = lax.Precision.HIGHEST

D_MODEL = 1024
DEPTH = 2
DN_ALPHA = (2.0 * DEPTH) ** 0.25

GDN_HEADS = 4
GDN_HEAD_DIM = 128
GDN_WIDTH = GDN_HEADS * GDN_HEAD_DIM
GDN_CONV = 4
GDN_CHUNK = 64
SC_WIDTH = 512
SC_CONV = 3

MLA_HEADS = 8
Q_LORA = 384
KV_LORA = 256
QK_NOPE = 128
QK_ROPE = 64
V_DIM = 128
ROPE_THETA = 10000.0

N_EXPERTS = 32
N_GROUPS = 4
EXPERTS_PER_GROUP = N_EXPERTS // N_GROUPS
D_EXPERT = 512

NORM_EPS = 1e-6
LN_EPS = 1e-5

LANES = 128
HALO = 16
TOKEN_TILE = 512
GDN_TILE = 256
ATTN_TILE = 1024
ATTN_ROWS = 256
LOG2_E = 1.4426950408889634
MOE_BLOCK = 256
BATCH_GROUPS = 1
VMEM_LIMIT = 48 * 1024 * 1024


def _cparams(*sem):
    return pltpu.CompilerParams(dimension_semantics=sem, vmem_limit_bytes=VMEM_LIMIT)


def _sigmoid(x):
    return 1.0 / (1.0 + jnp.exp(-x))


def _silu(x):
    return x * _sigmoid(x)


def _dot(a, b):
    return jnp.dot(a, b, preferred_element_type=F32)


def _dot_nt(a, b, precision=None):
    return lax.dot_general(a, b, (((1,), (1,)), ((), ())), precision=precision,
                           preferred_element_type=F32)


def _dot_tn(a, b):
    return lax.dot_general(a, b, (((0,), (0,)), ((), ())), preferred_element_type=F32)


def _ada_kernel(c_ref, w_ref, b_ref, o_ref):
    cond = _silu(c_ref[...])
    o_ref[0] = jnp.dot(cond, w_ref[0], precision=HIGHEST, preferred_element_type=F32) + b_ref[0]


def ada_mod(c, ada_w, ada_b):
    nl = ada_w.shape[0] * ada_w.shape[1]
    bsz, d = c.shape
    w = ada_w.reshape(nl, d, 3 * d)
    b = ada_b.reshape(nl, 1, 3 * d)
    return pl.pallas_call(
        _ada_kernel,
        grid=(nl, 3),
        in_specs=[pl.BlockSpec((bsz, d), lambda l, j: (0, 0)),
                  pl.BlockSpec((1, d, d), lambda l, j: (l, 0, j)),
                  pl.BlockSpec((1, 1, d), lambda l, j: (l, 0, j))],
        out_specs=pl.BlockSpec((1, bsz, d), lambda l, j: (l, 0, j)),
        out_shape=jax.ShapeDtypeStruct((nl, bsz, 3 * d), F32),
        compiler_params=_cparams("parallel", "parallel"),
        name="ada_mod",
    )(c, w, b)


EVEN_MAIN = 3 * GDN_WIDTH + GDN_WIDTH + 3 * SC_WIDTH


def _even_in_kernel(x_ref, mod_ref, w1_ref, w2_ref, alog_ref, dtb_ref, o1_ref, o2_ref, o2t_ref):
    d = D_MODEL
    mod = mod_ref[0, 0]
    h = (x_ref[0] * (1.0 + mod[:, d:2 * d]) + mod[:, :d]).astype(BF16)
    for j in range(EVEN_MAIN // 512):
        o1_ref[0, :, j * 512:(j + 1) * 512] = _dot(h, w1_ref[:, j * 512:(j + 1) * 512]).astype(BF16)
    r = _dot(h, w2_ref[...])
    lane = lax.broadcasted_iota(jnp.int32, r.shape, 1)
    a = r + dtb_ref[...]
    softplus = jnp.maximum(a, 0.0) + jnp.log(1.0 + jnp.exp(-jnp.abs(a)))
    bg = jnp.where(lane < GDN_HEADS, _sigmoid(r), -jnp.exp(alog_ref[...]) * softplus)
    o2_ref[0] = bg
    o2t_ref[0] = bg.T[:8]


def even_in_proj(x, mod, w_in, a_log, dt_bias):
    bsz, s, d = x.shape
    tm = min(TOKEN_TILE, s)
    q_end = 4 * GDN_WIDTH
    w1 = jnp.concatenate([w_in[:, :q_end], w_in[:, q_end + 2 * GDN_HEADS:]], axis=1).astype(BF16)
    w2 = jnp.pad(w_in[:, q_end:q_end + 2 * GDN_HEADS], ((0, 0), (0, LANES - 2 * GDN_HEADS))).astype(BF16)
    alog = jnp.pad(a_log.astype(F32), (GDN_HEADS, LANES - 2 * GDN_HEADS)).reshape(1, LANES)
    dtb = jnp.pad(dt_bias.astype(F32), (GDN_HEADS, LANES - 2 * GDN_HEADS)).reshape(1, LANES)
    const = lambda b, i: (0, 0)
    return pl.pallas_call(
        _even_in_kernel,
        grid=(bsz, s // tm),
        in_specs=[pl.BlockSpec((1, tm, d), lambda b, i: (b, i, 0)),
                  pl.BlockSpec((1, 1, 1, 3 * d), lambda b, i: (b, 0, 0, 0)),
                  pl.BlockSpec((d, EVEN_MAIN), const),
                  pl.BlockSpec((d, LANES), const),
                  pl.BlockSpec((1, LANES), const),
                  pl.BlockSpec((1, LANES), const)],
        out_specs=[pl.BlockSpec((1, tm, EVEN_MAIN), lambda b, i: (b, i, 0)),
                   pl.BlockSpec((1, tm, LANES), lambda b, i: (b, i, 0)),
                   pl.BlockSpec((1, 8, tm), lambda b, i: (b, 0, i))],
        out_shape=[jax.ShapeDtypeStruct((bsz, s, EVEN_MAIN), BF16),
                   jax.ShapeDtypeStruct((bsz, s, LANES), F32),
                   jax.ShapeDtypeStruct((bsz, 8, s), F32)],
        compiler_params=_cparams("parallel", "parallel"),
        name="even_in_proj",
    )(x, mod, w1, w2, alog, dtb)


def _causal_conv(x, halo, w, taps):
    rows = x.shape[0]
    xf = jnp.concatenate([halo, x], axis=0)
    y = w[taps - 1:taps] * x
    for j in range(taps - 1):
        off = HALO - (taps - 1) + j
        y = y + w[j:j + 1] * xf[off:off + rows]
    return y


def _inv_unit_lower(lows, xor_ij, block):
    eye = jnp.where(xor_ij == 0, 1.0, 0.0)
    ms = [eye - jnp.where(xor_ij == 1, low, 0.0) for low in lows]
    s = 2
    while s < block:
        level = (xor_ij >> (s.bit_length() - 1)) == 1
        mbs = [m.astype(BF16) for m in ms]
        cms = [_dot(jnp.where(level, low, 0.0).astype(BF16), mb).astype(BF16) for low, mb in zip(lows, mbs)]
        ms = [m - _dot(mb, cm) for m, mb, cm in zip(ms, mbs, cms)]
        s *= 2
    return ms


def _gdn_prep_kernel(qkv_ref, halo_ref, cw_ref, bg_ref, bgt_ref, ltri_ref, utri_ref, same_ref,
                     qd_ref, kd_ref, u_ref, w_ref, ic_ref, gl_ref):
    ts = qkv_ref.shape[1]
    c = GDN_CHUNK
    hd = GDN_HEAD_DIM
    i = pl.program_id(1)
    x = qkv_ref[0].astype(F32)
    halo = jnp.where(i > 0, halo_ref[0].astype(F32), 0.0)
    y = _silu(_causal_conv(x, halo, cw_ref[...], GDN_CONV))

    bg = bg_ref[0]
    gc_col = jnp.dot(ltri_ref[...], bg, precision=HIGHEST, preferred_element_type=F32)
    gc_row = jnp.dot(bgt_ref[0], utri_ref[...], precision=HIGHEST, preferred_element_type=F32)
    gc_end = jnp.dot(same_ref[...], bg, precision=HIGHEST, preferred_element_type=F32)

    ii = lax.broadcasted_iota(jnp.int32, (ts, ts), 0)
    jj = lax.broadcasted_iota(jnp.int32, (ts, ts), 1)
    xor_ij = ii ^ jj
    causal = (same_ref[...] > 0.0) & (ii >= jj)
    diag = xor_ij == 0

    heads = range(GDN_HEADS)
    gcols = [gc_col[:, GDN_HEADS + h:GDN_HEADS + h + 1] for h in heads]
    gends = [gc_end[:, GDN_HEADS + h:GDN_HEADS + h + 1] for h in heads]
    egcs = [jnp.exp(g) for g in gcols]
    lows, intras, rhss, qs, ks = [], [], [], [], []
    for h in heads:
        q = y[:, h * hd:(h + 1) * hd]
        k = y[:, GDN_WIDTH + h * hd:GDN_WIDTH + (h + 1) * hd]
        v = y[:, 2 * GDN_WIDTH + h * hd:2 * GDN_WIDTH + (h + 1) * hd]
        q = q * lax.rsqrt(jnp.sum(q * q, -1, keepdims=True) + NORM_EPS) * (hd ** -0.5)
        k = k * lax.rsqrt(jnp.sum(k * k, -1, keepdims=True) + NORM_EPS)
        beta = bg[:, h:h + 1]
        grow = gc_row[GDN_HEADS + h:GDN_HEADS + h + 1, :]
        decay = jnp.exp(jnp.where(causal, gcols[h] - grow, -jnp.inf))
        kb = k * beta
        kbf = k.astype(BF16)
        lows.append(jnp.where(diag, 0.0, _dot_nt(kb.astype(BF16), kbf) * decay))
        intras.append(_dot_nt(q.astype(BF16), kbf) * decay)
        rhss.append(jnp.concatenate([v * beta, kb * egcs[h]], axis=-1).astype(BF16))
        qs.append(q)
        ks.append(k)

    invs = _inv_unit_lower(lows, xor_ij, c)
    for h in heads:
        sol = _dot(invs[h].astype(BF16), rhss[h])
        cols = slice(h * hd, (h + 1) * hd)
        u_ref[0, :, cols] = sol[:, :hd]
        w_ref[0, :, cols] = sol[:, hd:].astype(BF16)
        qd_ref[0, :, cols] = (qs[h] * egcs[h]).astype(BF16)
        kd_ref[0, :, cols] = (ks[h] * jnp.exp(gends[h] - gcols[h])).astype(BF16)
        packed = intras[h][:, :c]
        for n in range(1, ts // c):
            packed = packed + intras[h][:, n * c:(n + 1) * c]
        ic_ref[0, :, h * c:(h + 1) * c] = packed.astype(BF16)
        for n in range(ts // c):
            gl_ref[0, n, h:h + 1, :] = jnp.broadcast_to(jnp.exp(gends[h][n * c:n * c + 1]), (1, hd))


def gdn_prep(o1, bg, bgt, conv_w):
    bsz, s, _ = o1.shape
    ts = min(GDN_TILE, s)
    c = GDN_CHUNK
    nc = ts // c
    qkv_w = 3 * GDN_WIDTH
    r = np.arange(ts)
    same = (r[:, None] // c) == (r[None, :] // c)
    ltri = jnp.asarray((same & (r[:, None] >= r[None, :])).astype(np.float32))
    utri = jnp.asarray((same & (r[:, None] <= r[None, :])).astype(np.float32))
    same = jnp.asarray(same.astype(np.float32))
    hb = ts // HALO
    tok = lambda b, i: (b, i, 0)
    const = lambda b, i: (0, 0)
    wide = jax.ShapeDtypeStruct((bsz, s, GDN_WIDTH), BF16)
    return pl.pallas_call(
        _gdn_prep_kernel,
        grid=(bsz, s // ts),
        in_specs=[pl.BlockSpec((1, ts, qkv_w), tok),
                  pl.BlockSpec((1, HALO, qkv_w), lambda b, i: (b, jnp.maximum(i * hb - 1, 0), 0)),
                  pl.BlockSpec((GDN_CONV, qkv_w), const),
                  pl.BlockSpec((1, ts, LANES), tok),
                  pl.BlockSpec((1, 8, ts), lambda b, i: (b, 0, i)),
                  pl.BlockSpec((ts, ts), const),
                  pl.BlockSpec((ts, ts), const),
                  pl.BlockSpec((ts, ts), const)],
        out_specs=[pl.BlockSpec((1, ts, GDN_WIDTH), tok),
                   pl.BlockSpec((1, ts, GDN_WIDTH), tok),
                   pl.BlockSpec((1, ts, GDN_WIDTH), tok),
                   pl.BlockSpec((1, ts, GDN_WIDTH), tok),
                   pl.BlockSpec((1, ts, GDN_HEADS * c), tok),
                   pl.BlockSpec((1, nc, GDN_HEADS, GDN_HEAD_DIM), lambda b, i: (b, i, 0, 0))],
        out_shape=[wide, wide, jax.ShapeDtypeStruct((bsz, s, GDN_WIDTH), F32), wide,
                   jax.ShapeDtypeStruct((bsz, s, GDN_HEADS * c), BF16),
                   jax.ShapeDtypeStruct((bsz, s // c, GDN_HEADS, GDN_HEAD_DIM), F32)],
        compiler_params=_cparams("parallel", "parallel"),
        name="gdn_prep",
    )(o1, o1, conv_w.astype(F32), bg, bgt, ltri, utri, same)


def _gdn_scan_kernel(qd_ref, kd_ref, u_ref, w_ref, ic_ref, gl_ref, o_ref, state_ref):
    ts = qd_ref.shape[1]
    c = GDN_CHUNK
    hd = GDN_HEAD_DIM

    @pl.when(pl.program_id(1) == 0)
    def _():
        state_ref[...] = jnp.zeros_like(state_ref)

    heads = range(GDN_HEADS)
    cols = [slice(h * hd, (h + 1) * hd) for h in heads]
    sts = [state_ref[h] for h in heads]
    for n in range(ts // c):
        rows = slice(n * c, (n + 1) * c)
        stbs = [st.astype(BF16) for st in sts]
        vbs = [(u_ref[0, rows, cols[h]] - _dot(w_ref[0, rows, cols[h]], stbs[h])).astype(BF16) for h in heads]
        for h in heads:
            o_ref[0, rows, cols[h]] = (_dot(qd_ref[0, rows, cols[h]], stbs[h])
                                       + _dot(ic_ref[0, rows, h * c:(h + 1) * c], vbs[h]))
        sts = [sts[h] * gl_ref[0, n, h:h + 1, :] + _dot_tn(kd_ref[0, rows, cols[h]], vbs[h]) for h in heads]
    for h in heads:
        state_ref[h] = sts[h]


def gdn_scan(qd, kd, u, w, ic, gl):
    bsz, s, _ = qd.shape
    ts = min(GDN_TILE, s)
    c = GDN_CHUNK
    tok = lambda b, i: (b, i, 0)
    return pl.pallas_call(
        _gdn_scan_kernel,
        grid=(bsz, s // ts),
        in_specs=[pl.BlockSpec((1, ts, GDN_WIDTH), tok)] * 4 + [
            pl.BlockSpec((1, ts, GDN_HEADS * c), tok),
            pl.BlockSpec((1, ts // c, GDN_HEADS, GDN_HEAD_DIM), lambda b, i: (b, i, 0, 0))],
        out_specs=pl.BlockSpec((1, ts, GDN_WIDTH), tok),
        out_shape=jax.ShapeDtypeStruct((bsz, s, GDN_WIDTH), F32),
        scratch_shapes=[pltpu.VMEM((GDN_HEADS, GDN_HEAD_DIM, GDN_HEAD_DIM), F32)],
        compiler_params=_cparams("parallel", "arbitrary"),
        name="gdn_scan",
    )(qd, kd, u, w, ic, gl)


def _even_post_kernel(o_ref, z_ref, scb_ref, scc_ref, sch_ref, scc_halo_ref, sch_halo_ref,
                      nw_ref, cw_ref, y_ref):
    hd = GDN_HEAD_DIM
    i = pl.program_id(1)
    o = o_ref[0]
    z = z_ref[0].astype(F32)
    nw = nw_ref[...]
    for h in range(GDN_HEADS):
        cols = slice(h * hd, (h + 1) * hd)
        oh = o[:, cols]
        on = oh * lax.rsqrt(jnp.mean(oh * oh, -1, keepdims=True) + NORM_EPS) * nw
        y_ref[0, :, cols] = (on * _silu(z[:, cols])).astype(BF16)
    ch = scc_ref[0].astype(F32) * sch_ref[0].astype(F32)
    ch_halo = jnp.where(i > 0, scc_halo_ref[0].astype(F32) * sch_halo_ref[0].astype(F32), 0.0)
    yb = scb_ref[0].astype(F32) * _causal_conv(ch, ch_halo, cw_ref[...], SC_CONV)
    y_ref[0, :, GDN_WIDTH:] = yb.astype(BF16)


def even_post(o, o1, norm_w, sc_conv_w):
    bsz, s, _ = o.shape
    tm = min(TOKEN_TILE, s)
    hb = tm // HALO
    wb = 512
    blk = lambda j: pl.BlockSpec((1, tm, wb), lambda b, i, j=j: (b, i, j))
    halo = lambda j: pl.BlockSpec((1, HALO, wb), lambda b, i, j=j: (b, jnp.maximum(i * hb - 1, 0), j))
    const = lambda b, i: (0, 0)
    return pl.pallas_call(
        _even_post_kernel,
        grid=(bsz, s // tm),
        in_specs=[pl.BlockSpec((1, tm, GDN_WIDTH), lambda b, i: (b, i, 0)),
                  blk(3), blk(4), blk(5), blk(6), halo(5), halo(6),
                  pl.BlockSpec((1, GDN_HEAD_DIM), const),
                  pl.BlockSpec((SC_CONV, SC_WIDTH), const)],
        out_specs=pl.BlockSpec((1, tm, GDN_WIDTH + SC_WIDTH), lambda b, i: (b, i, 0)),
        out_shape=jax.ShapeDtypeStruct((bsz, s, GDN_WIDTH + SC_WIDTH), BF16),
        compiler_params=_cparams("parallel", "parallel"),
        name="even_post",
    )(o, o1, o1, o1, o1, o1, o1, norm_w.astype(F32).reshape(1, -1), sc_conv_w.astype(F32))


def _layer_norm(r, g, b):
    mu = jnp.mean(r, -1, keepdims=True)
    rc = r - mu
    var = jnp.mean(rc * rc, -1, keepdims=True)
    return rc * lax.rsqrt(var + LN_EPS) * g + b


def _route(logits_t, bias):
    scores = _sigmoid(logits_t)
    biased = scores + bias
    t = logits_t.shape[1]
    epg = EXPERTS_PER_GROUP
    sub = lax.broadcasted_iota(jnp.int32, (epg, t), 0).astype(F32)
    best = None
    for g in range(N_GROUPS):
        bgp = biased[g * epg:(g + 1) * epg]
        m1 = jnp.max(bgp, axis=0, keepdims=True)
        i1 = jnp.min(jnp.where(bgp == m1, sub, float(epg)), axis=0, keepdims=True)
        rest = jnp.where(sub == i1, -jnp.inf, bgp)
        m2 = jnp.max(rest, axis=0, keepdims=True)
        i2 = jnp.min(jnp.where(rest == m2, sub, float(epg)), axis=0, keepdims=True)
        gs = m1 + m2
        if best is None:
            best, e0, e1 = gs, i1, i2
        else:
            better = gs > best
            best = jnp.where(better, gs, best)
            e0 = jnp.where(better, float(g * epg) + i1, e0)
            e1 = jnp.where(better, float(g * epg) + i2, e1)
    eio = lax.broadcasted_iota(jnp.int32, scores.shape, 0).astype(F32)
    hit0 = eio == e0
    hit1 = eio == e1
    s0 = jnp.sum(jnp.where(hit0, scores, 0.0), axis=0, keepdims=True)
    s1 = jnp.sum(jnp.where(hit1, scores, 0.0), axis=0, keepdims=True)
    tot = s0 + s1
    return e0.astype(jnp.int32), e1.astype(jnp.int32), s0 / tot, s1 / tot, hit0, hit1


def _mix_out_kernel(y_ref, wo_ref, x_ref, moda_ref, modb_ref, lng_ref, lnb_ref, rwt_ref, rb_ref,
                    upper_ref, x1_ref, hm_ref, seli_ref, selw_ref, cnt_ref):
    d = D_MODEL
    gate = moda_ref[0, 0][:, 2 * d:]
    modb = modb_ref[0, 0]
    y = _dot(y_ref[0], wo_ref[...])
    x1 = _layer_norm(DN_ALPHA * x_ref[0] + (1.0 + gate) * y, lng_ref[...], lnb_ref[...])
    x1_ref[0] = x1
    hm = x1 * (1.0 + modb[:, d:2 * d]) + modb[:, :d]
    hm_ref[0] = hm.astype(BF16)

    logits_t = _dot_nt(rwt_ref[...], hm, precision=HIGHEST)
    e0, e1, w0, w1, hit0, hit1 = _route(logits_t, rb_ref[...])
    member = jnp.where(hit0 | hit1, 1.0, 0.0)
    before = _dot(member.astype(BF16), upper_ref[...])
    rank0 = jnp.sum(jnp.where(hit0, before, 0.0), axis=0, keepdims=True).astype(jnp.int32)
    rank1 = jnp.sum(jnp.where(hit1, before, 0.0), axis=0, keepdims=True).astype(jnp.int32)
    zi = jnp.zeros_like(e0)
    seli_ref[0] = jnp.concatenate([e0, e1, rank0, rank1, zi, zi, zi, zi], axis=0)
    zf = jnp.zeros_like(w0)
    selw_ref[0] = jnp.concatenate([w0, w1, zf, zf, zf, zf, zf, zf], axis=0)
    cnt_ref[0, 0] = jnp.broadcast_to(jnp.sum(member, axis=1, keepdims=True), (N_EXPERTS, LANES))


def mix_out(ymix, w_out, x, moda, modb, ln_g, ln_b, router_w, router_bias):
    bsz, s, d = x.shape
    tm = min(TOKEN_TILE, s)
    nt = s // tm
    r = np.arange(tm)
    upper = jnp.asarray((r[:, None] < r[None, :]).astype(np.float32), dtype=BF16)
    tok = lambda b, i: (b, i, 0)
    const = lambda b, i: (0, 0)
    modspec = pl.BlockSpec((1, 1, 1, 3 * d), lambda b, i: (b, 0, 0, 0))
    row8 = pl.BlockSpec((1, 8, tm), lambda b, i: (b, 0, i))
    return pl.pallas_call(
        _mix_out_kernel,
        grid=(bsz, nt),
        in_specs=[pl.BlockSpec((1, tm, d), tok),
                  pl.BlockSpec((d, d), const),
                  pl.BlockSpec((1, tm, d), tok),
                  modspec, modspec,
                  pl.BlockSpec((1, d), const), pl.BlockSpec((1, d), const),
                  pl.BlockSpec((N_EXPERTS, d), const),
                  pl.BlockSpec((N_EXPERTS, 1), const),
                  pl.BlockSpec((tm, tm), const)],
        out_specs=[pl.BlockSpec((1, tm, d), tok),
                   pl.BlockSpec((1, tm, d), tok),
                   row8, row8,
                   pl.BlockSpec((1, 1, N_EXPERTS, LANES), lambda b, i: (b, i, 0, 0))],
        out_shape=[jax.ShapeDtypeStruct((bsz, s, d), F32),
                   jax.ShapeDtypeStruct((bsz, s, d), BF16),
                   jax.ShapeDtypeStruct((bsz, 8, s), jnp.int32),
                   jax.ShapeDtypeStruct((bsz, 8, s), F32),
                   jax.ShapeDtypeStruct((bsz, nt, N_EXPERTS, LANES), F32)],
        compiler_params=_cparams("parallel", "parallel"),
        name="mix_out",
    )(ymix, w_out.astype(BF16), x, moda, modb, ln_g.reshape(1, d), ln_b.reshape(1, d),
      router_w.T.astype(F32), router_bias.astype(F32).reshape(N_EXPERTS, 1), upper)


def _experts_kernel(be_ref, nused_ref, xs_ref, wg_ref, wu_ref, wd_ref, ys_ref, wgb_ref, wub_ref, wdb_ref):
    i = pl.program_id(0)

    @pl.when((i == 0) | (be_ref[i] != be_ref[jnp.maximum(i - 1, 0)]))
    def _():
        wgb_ref[...] = wg_ref[0, 0].astype(BF16)
        wub_ref[...] = wu_ref[0, 0].astype(BF16)
        wdb_ref[...] = wd_ref[0, 0].astype(BF16)

    @pl.when(i < nused_ref[0])
    def _():
        x = xs_ref[...]
        act = (_silu(_dot(x, wgb_ref[...])) * _dot(x, wub_ref[...])).astype(BF16)
        ys_ref[...] = _dot(act, wdb_ref[...]).astype(BF16)

    @pl.when(i >= nused_ref[0])
    def _():
        ys_ref[...] = jnp.zeros_like(ys_ref)


def moe_experts(xs, block_expert, n_used, w_gate, w_up, w_down, layer):
    rows, d = xs.shape
    nb = rows // MOE_BLOCK
    by_expert = lambda i, be, nu: (layer, be[i], 0, 0)
    return pl.pallas_call(
        _experts_kernel,
        grid_spec=pltpu.PrefetchScalarGridSpec(
            num_scalar_prefetch=2,
            grid=(nb,),
            in_specs=[pl.BlockSpec((MOE_BLOCK, d), lambda i, be, nu: (i, 0)),
                      pl.BlockSpec((1, 1, d, D_EXPERT), by_expert),
                      pl.BlockSpec((1, 1, d, D_EXPERT), by_expert),
                      pl.BlockSpec((1, 1, D_EXPERT, d), by_expert)],
            out_specs=pl.BlockSpec((MOE_BLOCK, d), lambda i, be, nu: (i, 0)),
            scratch_shapes=[pltpu.VMEM((d, D_EXPERT), BF16), pltpu.VMEM((d, D_EXPERT), BF16),
                            pltpu.VMEM((D_EXPERT, d), BF16)]),
        out_shape=jax.ShapeDtypeStruct((rows, d), BF16),
        compiler_params=_cparams("arbitrary"),
        name="moe_experts",
    )(block_expert, n_used, xs, w_gate, w_up, w_down)


def _plan_kernel(seli_ref, base_ref, dest_ref):
    sel = seli_ref[0]
    base = base_ref[0, 0]
    eio = lax.broadcasted_iota(jnp.int32, (N_EXPERTS, sel.shape[1]), 0)
    out = []
    for kk in range(2):
        first = jnp.sum(jnp.where(eio == sel[kk:kk + 1], base, 0.0), axis=0, keepdims=True)
        out.append(first.astype(jnp.int32) + sel[2 + kk:3 + kk])
    zero = jnp.zeros_like(out[0])
    dest_ref[0] = jnp.concatenate(out + [zero] * 6, axis=0)


def dispatch_plan(seli, base):
    bsz, _, s = seli.shape
    nt = base.shape[1]
    tm = s // nt
    return pl.pallas_call(
        _plan_kernel,
        grid=(bsz, nt),
        in_specs=[pl.BlockSpec((1, 8, tm), lambda b, i: (b, 0, i)),
                  pl.BlockSpec((1, 1, N_EXPERTS, 1), lambda b, i: (b, i, 0, 0))],
        out_specs=pl.BlockSpec((1, 8, tm), lambda b, i: (b, 0, i)),
        out_shape=jax.ShapeDtypeStruct((bsz, 8, s), jnp.int32),
        compiler_params=_cparams("parallel", "parallel"),
        name="dispatch_plan",
    )(seli, base)


def _invert_kernel(dest_ref, pad_lo_ref, pad_hi_ref, slot_tok_ref):
    t = dest_ref.shape[0] // 2

    def clear_range(e, carry):
        def clear(r, c):
            slot_tok_ref[r] = lax.rem(r, t)
            return c
        return lax.fori_loop(pad_lo_ref[e], pad_hi_ref[e], clear, carry)

    def put(a, carry):
        slot_tok_ref[dest_ref[a]] = a
        slot_tok_ref[dest_ref[t + a]] = a
        return carry

    lax.fori_loop(0, pad_lo_ref.shape[0], clear_range, 0)
    lax.fori_loop(0, t, put, 0, unroll=8)


def invert_slots(dest_flat, pad_lo, pad_hi, rows):
    smem = pl.BlockSpec(memory_space=pltpu.SMEM)
    return pl.pallas_call(
        _invert_kernel,
        in_specs=[smem, smem, smem],
        out_specs=smem,
        out_shape=jax.ShapeDtypeStruct((rows,), jnp.int32),
        name="invert_slots",
    )(dest_flat, pad_lo, pad_hi)


def _moe_combine_kernel(g0_ref, g1_ref, wt_ref, x_ref, mod_ref, lng_ref, lnb_ref, o_ref):
    d = D_MODEL
    gate = mod_ref[0, 0][:, 2 * d:]
    wt = wt_ref[0]
    y = g0_ref[0, 0].astype(F32) * wt[:, 0:1] + g1_ref[0, 0].astype(F32) * wt[:, 1:2]
    o_ref[0] = _layer_norm(DN_ALPHA * x_ref[0] + (1.0 + gate) * y, lng_ref[...], lnb_ref[...])


def moe_combine(g, wt, x1, modb, ln_g, ln_b):
    bsz, s, d = x1.shape
    tm = min(TOKEN_TILE, s)
    tok = lambda b, i: (b, i, 0)
    const = lambda b, i: (0, 0)
    return pl.pallas_call(
        _moe_combine_kernel,
        grid=(bsz, s // tm),
        in_specs=[pl.BlockSpec((1, 1, tm, d), lambda b, i: (0, b, i, 0)),
                  pl.BlockSpec((1, 1, tm, d), lambda b, i: (1, b, i, 0)),
                  pl.BlockSpec((1, tm, 8), tok), pl.BlockSpec((1, tm, d), tok),
                  pl.BlockSpec((1, 1, 1, 3 * d), lambda b, i: (b, 0, 0, 0)),
                  pl.BlockSpec((1, d), const), pl.BlockSpec((1, d), const)],
        out_specs=pl.BlockSpec((1, tm, d), tok),
        out_shape=jax.ShapeDtypeStruct((bsz, s, d), F32),
        compiler_params=_cparams("parallel", "parallel"),
        name="moe_combine",
    )(g, g, wt, x1, modb, ln_g.reshape(1, d), ln_b.reshape(1, d))


def moe_layer(x1, hm, seli, selw, cnt, modb, ln_g, ln_b, w_gate, w_up, w_down, layer):
    bsz, s, d = x1.shape
    t = bsz * s
    tm = min(TOKEN_TILE, s)
    blk = MOE_BLOCK
    cnt = cnt[..., 0].reshape(-1, N_EXPERTS).astype(jnp.int32)
    tile_off = jnp.cumsum(cnt, axis=0) - cnt
    counts = cnt.sum(0)
    padded = (counts + blk - 1) // blk * blk
    pend = jnp.cumsum(padded)
    pstart = pend - padded
    nb = -(-(2 * t) // blk) + N_EXPERTS
    rows = nb * blk
    base = (pstart[None, :] + tile_off).astype(F32).reshape(bsz, s // tm, N_EXPERTS, 1)
    dest = dispatch_plan(seli, base)
    dest = jnp.swapaxes(dest[:, 0:2, :], 0, 1).reshape(2 * t)
    pad_lo = jnp.concatenate([pstart + counts, pend[-1:]]).astype(jnp.int32)
    pad_hi = jnp.concatenate([pend, jnp.full((1,), rows)]).astype(jnp.int32)
    slot_tok = invert_slots(dest, pad_lo, pad_hi, rows)
    starts = jnp.arange(nb, dtype=jnp.int32) * blk
    block_expert = jnp.minimum(jnp.sum((pend[None, :] <= starts[:, None]).astype(jnp.int32), axis=1),
                               N_EXPERTS - 1)
    n_used = (pend[-1] // blk).astype(jnp.int32).reshape(1)
    xs = hm.reshape(t, d).at[slot_tok].get(mode="promise_in_bounds")
    ys = moe_experts(xs, block_expert, n_used, w_gate, w_up, w_down, layer)
    g = ys.at[dest].get(mode="promise_in_bounds").reshape(2, bsz, s, d)
    wt = jnp.swapaxes(selw, 1, 2)
    return moe_combine(g, wt, x1, modb, ln_g, ln_b)


MLA_IN_COLS = Q_LORA + KV_LORA + 2 * LANES
MLA_QK = 2 * LANES


def _mla_in_kernel(x_ref, mod_ref, pos_ref, invf_ref, win_ref, qn_ref, kvn_ref, wq_ref, wkv_ref,
                   q_ref, k_ref, v_ref):
    d = D_MODEL
    nh = MLA_HEADS
    mod = mod_ref[0, 0]
    h = (x_ref[0] * (1.0 + mod[:, d:2 * d]) + mod[:, :d]).astype(BF16)
    proj = _dot(h, win_ref[...])
    ang = pos_ref[0].astype(F32) * invf_ref[...]
    lane = lax.broadcasted_iota(jnp.int32, ang.shape, 1)
    cos = jnp.where(lane < QK_ROPE, jnp.cos(ang), 0.0)
    sin = jnp.where(lane < QK_ROPE, jnp.sin(ang), 0.0)

    qa = proj[:, :Q_LORA]
    qa = (qa * lax.rsqrt(jnp.mean(qa * qa, -1, keepdims=True) + NORM_EPS) * qn_ref[...]).astype(BF16)
    kva = proj[:, Q_LORA:Q_LORA + KV_LORA]
    kva = (kva * lax.rsqrt(jnp.mean(kva * kva, -1, keepdims=True) + NORM_EPS) * kvn_ref[...]).astype(BF16)
    kr0 = Q_LORA + KV_LORA
    k_rope = (proj[:, kr0:kr0 + LANES] * cos + proj[:, kr0 + LANES:kr0 + 2 * LANES] * sin).astype(BF16)

    scale = (QK_NOPE + QK_ROPE) ** -0.5 * LOG2_E
    hw = nh * LANES
    q_nope = _dot(qa, wq_ref[:, :hw])
    q_rope = _dot(qa, wq_ref[:, hw:2 * hw])
    q_rot = _dot(qa, wq_ref[:, 2 * hw:])
    k_nope = _dot(kva, wkv_ref[:, :hw])
    v_ref[0] = _dot(kva, wkv_ref[:, hw:]).astype(BF16)
    for hh in range(nh):
        cols = slice(hh * LANES, (hh + 1) * LANES)
        q_ref[0, :, hh * MLA_QK:hh * MLA_QK + LANES] = (q_nope[:, cols] * scale).astype(BF16)
        q_ref[0, :, hh * MLA_QK + LANES:(hh + 1) * MLA_QK] = (
            (q_rope[:, cols] * cos + q_rot[:, cols] * sin) * scale).astype(BF16)
        k_ref[0, :, hh * MLA_QK:hh * MLA_QK + LANES] = k_nope[:, cols].astype(BF16)
        k_ref[0, :, hh * MLA_QK + LANES:(hh + 1) * MLA_QK] = k_rope


def _rope_cols(w):
    half = QK_ROPE // 2
    pad = [(0, 0)] * (w.ndim - 1) + [(0, LANES - QK_ROPE)]
    rot = jnp.concatenate([-w[..., half:], w[..., :half]], axis=-1)
    return jnp.pad(w, pad), jnp.pad(rot, pad)


def mla_in(x, mod, positions, w_in, q_a_norm, w_q_b, kv_a_norm, w_kv_b):
    bsz, s, d = x.shape
    tm = min(TOKEN_TILE, s)
    nh = MLA_HEADS
    kr, kr_rot = _rope_cols(w_in[:, Q_LORA + KV_LORA:])
    win = jnp.concatenate([w_in[:, :Q_LORA + KV_LORA], kr, kr_rot], axis=1).astype(BF16)
    wq = w_q_b.reshape(Q_LORA, nh, QK_NOPE + QK_ROPE)
    qr, qr_rot = _rope_cols(wq[..., QK_NOPE:])
    wq = jnp.concatenate([wq[..., :QK_NOPE].reshape(Q_LORA, -1), qr.reshape(Q_LORA, -1),
                          qr_rot.reshape(Q_LORA, -1)], axis=1).astype(BF16)
    wkv = w_kv_b.reshape(KV_LORA, nh, QK_NOPE + V_DIM)
    wkv = jnp.concatenate([wkv[..., :QK_NOPE].reshape(KV_LORA, -1),
                           wkv[..., QK_NOPE:].reshape(KV_LORA, -1)], axis=1).astype(BF16)
    inv = ROPE_THETA ** (-np.arange(0, QK_ROPE, 2, dtype=np.float32) / QK_ROPE)
    invf = np.zeros((1, LANES), np.float32)
    invf[0, :QK_ROPE] = np.concatenate([inv, inv])
    tok = lambda b, i: (b, i, 0)
    const = lambda b, i: (0, 0)
    return pl.pallas_call(
        _mla_in_kernel,
        grid=(bsz, s // tm),
        in_specs=[pl.BlockSpec((1, tm, d), tok),
                  pl.BlockSpec((1, 1, 1, 3 * d), lambda b, i: (b, 0, 0, 0)),
                  pl.BlockSpec((1, tm, 1), tok),
                  pl.BlockSpec((1, LANES), const),
                  pl.BlockSpec((d, MLA_IN_COLS), const),
                  pl.BlockSpec((1, Q_LORA), const),
                  pl.BlockSpec((1, KV_LORA), const),
                  pl.BlockSpec((Q_LORA, 3 * nh * LANES), const),
                  pl.BlockSpec((KV_LORA, 2 * nh * LANES), const)],
        out_specs=[pl.BlockSpec((1, tm, nh * MLA_QK), tok),
                   pl.BlockSpec((1, tm, nh * MLA_QK), tok),
                   pl.BlockSpec((1, tm, nh * V_DIM), tok)],
        out_shape=[jax.ShapeDtypeStruct((bsz, s, nh * MLA_QK), BF16),
                   jax.ShapeDtypeStruct((bsz, s, nh * MLA_QK), BF16),
                   jax.ShapeDtypeStruct((bsz, s, nh * V_DIM), BF16)],
        compiler_params=_cparams("parallel", "parallel"),
        name="mla_in",
    )(x, mod, positions.reshape(bsz, s, 1), jnp.asarray(invf), win,
      q_a_norm.astype(F32).reshape(1, -1), kv_a_norm.astype(F32).reshape(1, -1), wq, wkv)


def _attn_kernel(q_ref, k_ref, v_ref, o_ref, m_ref, l_ref, acc_ref):
    tq = q_ref.shape[1]
    i = pl.program_id(2)
    m_ref[...] = jnp.full(m_ref.shape, -jnp.inf, F32)
    l_ref[...] = jnp.zeros(l_ref.shape, F32)
    acc_ref[...] = jnp.zeros(acc_ref.shape, F32)

    groups = [slice(g * ATTN_ROWS, (g + 1) * ATTN_ROWS) for g in range(tq // ATTN_ROWS)]

    def update(off, widths, masked):
        scs = []
        for rows, width in zip(groups, widths):
            sc = _dot_nt(q_ref[0, rows, :], k_ref[0, pl.ds(off, width), :])
            if masked:
                qi = lax.broadcasted_iota(jnp.int32, sc.shape, 0) + rows.start
                ki = lax.broadcasted_iota(jnp.int32, sc.shape, 1)
                sc = jnp.where(ki <= qi, sc, -jnp.inf)
            scs.append(sc)
        m_olds = [m_ref[rows] for rows in groups]
        m_news = [jnp.maximum(m_old, jnp.max(sc, -1, keepdims=True)) for m_old, sc in zip(m_olds, scs)]
        ps = [jnp.exp2(sc - pltpu.repeat(m_new, width // LANES, axis=1))
              for sc, m_new, width in zip(scs, m_news, widths)]
        alphas = [jnp.exp2(m_old - m_new) for m_old, m_new in zip(m_olds, m_news)]
        pvs = [_dot(p.astype(BF16), v_ref[0, pl.ds(off, width), :]) for p, width in zip(ps, widths)]
        for rows, m_new, alpha, p, pv in zip(groups, m_news, alphas, ps, pvs):
            l_ref[rows] = alpha * l_ref[rows] + jnp.sum(p, -1, keepdims=True)
            acc_ref[rows] = alpha * acc_ref[rows] + pv
            m_ref[rows] = m_new

    def full_block(j, carry):
        update(pl.multiple_of(j * tq, tq), [tq] * len(groups), False)
        return carry

    lax.fori_loop(0, i, full_block, 0)
    update(pl.multiple_of(i * tq, tq), [rows.stop for rows in groups], True)
    o_ref[0] = (acc_ref[...] / l_ref[...]).astype(BF16)


def attention(q, k, v):
    bsz, s, _ = q.shape
    tq = min(ATTN_TILE, s)
    return pl.pallas_call(
        _attn_kernel,
        grid=(bsz, MLA_HEADS, s // tq),
        in_specs=[pl.BlockSpec((1, tq, MLA_QK), lambda b, h, i: (b, i, h)),
                  pl.BlockSpec((1, s, MLA_QK), lambda b, h, i: (b, 0, h)),
                  pl.BlockSpec((1, s, V_DIM), lambda b, h, i: (b, 0, h))],
        out_specs=pl.BlockSpec((1, tq, V_DIM), lambda b, h, i: (b, i, h)),
        out_shape=jax.ShapeDtypeStruct((bsz, s, MLA_HEADS * V_DIM), BF16),
        scratch_shapes=[pltpu.VMEM((tq, LANES), F32), pltpu.VMEM((tq, LANES), F32), pltpu.VMEM((tq, V_DIM), F32)],
        compiler_params=_cparams("parallel", "parallel", "arbitrary"),
        name="attention",
    )(q, k, v)


def kernel(x, c, positions, ada_w, ada_b, ln_g, ln_b, w_in_e, gdn_conv_w, gdn_a_log, gdn_dt_bias,
           gdn_norm_w, sc_conv_w, w_out_e, w_in_o, q_a_norm, w_q_b, kv_a_norm, w_kv_b, w_out_o,
           router_w, router_bias, w_gate, w_up, w_down):
    bsz, s, d = x.shape
    depth = ada_w.shape[0]
    mod = ada_mod(c, ada_w, ada_b).reshape(depth, 2, bsz, 1, 3 * d)

    def layers(x, mod, positions):
        for i in range(depth):
            j = i // 2
            moda, modb = mod[i, 0][:, None], mod[i, 1][:, None]
            if i % 2 == 0:
                o1, bg, bgt = even_in_proj(x, moda, w_in_e[j], gdn_a_log[j], gdn_dt_bias[j])
                qd, kd, u, w, ic, gl = gdn_prep(o1, bg, bgt, gdn_conv_w[j])
                o = gdn_scan(qd, kd, u, w, ic, gl)
                ymix = even_post(o, o1, gdn_norm_w[j], sc_conv_w[j])
                w_out = w_out_e[j]
            else:
                q, k, v = mla_in(x, moda, positions, w_in_o[j], q_a_norm[j], w_q_b[j], kv_a_norm[j],
                                 w_kv_b[j])
                ymix = attention(q, k, v)
                w_out = w_out_o[j]
            x1, hm, seli, selw, cnt = mix_out(ymix, w_out, x, moda, modb, ln_g[i, 0], ln_b[i, 0],
                                              router_w, router_bias)
            x = moe_layer(x1, hm, seli, selw, cnt, modb, ln_g[i, 1], ln_b[i, 1], w_gate, w_up, w_down, i)
        return x

    ng = BATCH_GROUPS if bsz % BATCH_GROUPS == 0 else 1
    gb = bsz // ng
    outs = [layers(x[g * gb:(g + 1) * gb], mod[:, :, g * gb:(g + 1) * gb], positions[g * gb:(g + 1) * gb])
            for g in range(ng)]
    return outs[0] if ng == 1 else jnp.concatenate(outs, axis=0)
```

```python
import functools

import numpy as np
import jax
import jax.numpy as jnp
from jax import lax
from jax.experimental import pallas as pl
from jax.experimental.pallas import tpu as pltpu

F32 = jnp.float32
BF16 = jnp.bfloat16
HIGHEST = lax.Precision.HIGHEST

D_MODEL = 1024
DEPTH = 2
DN_ALPHA = (2.0 * DEPTH) ** 0.25

GDN_HEADS = 4
GDN_HEAD_DIM = 128
GDN_WIDTH = GDN_HEADS * GDN_HEAD_DIM
GDN_CONV = 4
GDN_CHUNK = 64
SC_WIDTH = 512
SC_CONV = 3

MLA_HEADS = 8
Q_LORA = 384
KV_LORA = 256
QK_NOPE = 128
QK_ROPE = 64
V_DIM = 128
ROPE_THETA = 10000.0

N_EXPERTS = 32
N_GROUPS = 4
EXPERTS_PER_GROUP = N_EXPERTS // N_GROUPS
D_EXPERT = 512

NORM_EPS = 1e-6
LN_EPS = 1e-5

LANES = 128
HALO = 16
TOKEN_TILE = 512
GDN_TILE = 256
ATTN_TILE = 1024
ATTN_ROWS = 256
LOG2_E = 1.4426950408889634
MOE_BLOCK = 512
BATCH_GROUPS = 1
VMEM_LIMIT = 48 * 1024 * 1024


def _cparams(*sem):
    return pltpu.CompilerParams(dimension_semantics=sem, vmem_limit_bytes=VMEM_LIMIT)


def _sigmoid(x):
    return 1.0 / (1.0 + jnp.exp(-x))


def _silu(x):
    return x * _sigmoid(x)


def _dot(a, b):
    return jnp.dot(a, b, preferred_element_type=F32)


def _dot_nt(a, b, precision=None):
    return lax.dot_general(a, b, (((1,), (1,)), ((), ())), precision=precision,
                           preferred_element_type=F32)


def _dot_tn(a, b):
    return lax.dot_general(a, b, (((0,), (0,)), ((), ())), preferred_element_type=F32)


def _ada_kernel(c_ref, w_ref, b_ref, o_ref):
    cond = _silu(c_ref[...])
    o_ref[0] = jnp.dot(cond, w_ref[0], precision=HIGHEST, preferred_element_type=F32) + b_ref[0]


def ada_mod(c, ada_w, ada_b):
    nl = ada_w.shape[0] * ada_w.shape[1]
    bsz, d = c.shape
    w = ada_w.reshape(nl, d, 3 * d)
    b = ada_b.reshape(nl, 1, 3 * d)
    return pl.pallas_call(
        _ada_kernel,
        grid=(nl, 3),
        in_specs=[pl.BlockSpec((bsz, d), lambda l, j: (0, 0)),
                  pl.BlockSpec((1, d, d), lambda l, j: (l, 0, j)),
                  pl.BlockSpec((1, 1, d), lambda l, j: (l, 0, j))],
        out_specs=pl.BlockSpec((1, bsz, d), lambda l, j: (l, 0, j)),
        out_shape=jax.ShapeDtypeStruct((nl, bsz, 3 * d), F32),
        compiler_params=_cparams("parallel", "parallel"),
        name="ada_mod",
    )(c, w, b)


EVEN_MAIN = 3 * GDN_WIDTH + GDN_WIDTH + 3 * SC_WIDTH


def _even_in_kernel(x_ref, mod_ref, w1_ref, w2_ref, alog_ref, dtb_ref, o1_ref, o2_ref, o2t_ref):
    d = D_MODEL
    mod = mod_ref[0, 0]
    h = (x_ref[0] * (1.0 + mod[:, d:2 * d]) + mod[:, :d]).astype(BF16)
    for j in range(EVEN_MAIN // 512):
        o1_ref[0, :, j * 512:(j + 1) * 512] = _dot(h, w1_ref[:, j * 512:(j + 1) * 512]).astype(BF16)
    r = _dot(h, w2_ref[...])
    lane = lax.broadcasted_iota(jnp.int32, r.shape, 1)
    a = r + dtb_ref[...]
    softplus = jnp.maximum(a, 0.0) + jnp.log(1.0 + jnp.exp(-jnp.abs(a)))
    bg = jnp.where(lane < GDN_HEADS, _sigmoid(r), -jnp.exp(alog_ref[...]) * softplus)
    o2_ref[0] = bg
    o2t_ref[0] = bg.T[:8]


def even_in_proj(x, mod, w_in, a_log, dt_bias):
    bsz, s, d = x.shape
    tm = min(TOKEN_TILE, s)
    q_end = 4 * GDN_WIDTH
    w1 = jnp.concatenate([w_in[:, :q_end], w_in[:, q_end + 2 * GDN_HEADS:]], axis=1).astype(BF16)
    w2 = jnp.pad(w_in[:, q_end:q_end + 2 * GDN_HEADS], ((0, 0), (0, LANES - 2 * GDN_HEADS))).astype(BF16)
    alog = jnp.pad(a_log.astype(F32), (GDN_HEADS, LANES - 2 * GDN_HEADS)).reshape(1, LANES)
    dtb = jnp.pad(dt_bias.astype(F32), (GDN_HEADS, LANES - 2 * GDN_HEADS)).reshape(1, LANES)
    const = lambda b, i: (0, 0)
    return pl.pallas_call(
        _even_in_kernel,
        grid=(bsz, s // tm),
        in_specs=[pl.BlockSpec((1, tm, d), lambda b, i: (b, i, 0)),
                  pl.BlockSpec((1, 1, 1, 3 * d), lambda b, i: (b, 0, 0, 0)),
                  pl.BlockSpec((d, EVEN_MAIN), const),
                  pl.BlockSpec((d, LANES), const),
                  pl.BlockSpec((1, LANES), const),
                  pl.BlockSpec((1, LANES), const)],
        out_specs=[pl.BlockSpec((1, tm, EVEN_MAIN), lambda b, i: (b, i, 0)),
                   pl.BlockSpec((1, tm, LANES), lambda b, i: (b, i, 0)),
                   pl.BlockSpec((1, 8, tm), lambda b, i: (b, 0, i))],
        out_shape=[jax.ShapeDtypeStruct((bsz, s, EVEN_MAIN), BF16),
                   jax.ShapeDtypeStruct((bsz, s, LANES), F32),
                   jax.ShapeDtypeStruct((bsz, 8, s), F32)],
        compiler_params=_cparams("parallel", "parallel"),
        name="even_in_proj",
    )(x, mod, w1, w2, alog, dtb)


def _causal_conv(x, halo, w, taps):
    rows = x.shape[0]
    xf = jnp.concatenate([halo, x], axis=0)
    y = w[taps - 1:taps] * x
    for j in range(taps - 1):
        off = HALO - (taps - 1) + j
        y = y + w[j:j + 1] * xf[off:off + rows]
    return y


def _inv_unit_lower(lows, xor_ij, block):
    eye = jnp.where(xor_ij == 0, 1.0, 0.0)
    ms = [eye - jnp.where(xor_ij == 1, low, 0.0) for low in lows]
    s = 2
    while s < block:
        level = (xor_ij >> (s.bit_length() - 1)) == 1
        mbs = [m.astype(BF16) for m in ms]
        cms = [_dot(jnp.where(level, low, 0.0).astype(BF16), mb).astype(BF16) for low, mb in zip(lows, mbs)]
        ms = [m - _dot(mb, cm) for m, mb, cm in zip(ms, mbs, cms)]
        s *= 2
    return ms


def _gdn_prep_kernel(qkv_ref, halo_ref, cw_ref, bg_ref, bgt_ref, ltri_ref, utri_ref, same_ref,
                     qd_ref, kd_ref, u_ref, w_ref, ic_ref, gl_ref):
    ts = qkv_ref.shape[1]
    c = GDN_CHUNK
    hd = GDN_HEAD_DIM
    i = pl.program_id(1)
    x = qkv_ref[0].astype(F32)
    halo = jnp.where(i > 0, halo_ref[0].astype(F32), 0.0)
    y = _silu(_causal_conv(x, halo, cw_ref[...], GDN_CONV))

    bg = bg_ref[0]
    gc_col = jnp.dot(ltri_ref[...], bg, precision=HIGHEST, preferred_element_type=F32)
    gc_row = jnp.dot(bgt_ref[0], utri_ref[...], precision=HIGHEST, preferred_element_type=F32)
    gc_end = jnp.dot(same_ref[...], bg, precision=HIGHEST, preferred_element_type=F32)

    ii = lax.broadcasted_iota(jnp.int32, (ts, ts), 0)
    jj = lax.broadcasted_iota(jnp.int32, (ts, ts), 1)
    xor_ij = ii ^ jj
    causal = (same_ref[...] > 0.0) & (ii >= jj)
    diag = xor_ij == 0

    heads = range(GDN_HEADS)
    gcols = [gc_col[:, GDN_HEADS + h:GDN_HEADS + h + 1] for h in heads]
    gends = [gc_end[:, GDN_HEADS + h:GDN_HEADS + h + 1] for h in heads]
    egcs = [jnp.exp(g) for g in gcols]
    lows, intras, rhss, qs, ks = [], [], [], [], []
    for h in heads:
        q = y[:, h * hd:(h + 1) * hd]
        k = y[:, GDN_WIDTH + h * hd:GDN_WIDTH + (h + 1) * hd]
        v = y[:, 2 * GDN_WIDTH + h * hd:2 * GDN_WIDTH + (h + 1) * hd]
        q = q * lax.rsqrt(jnp.sum(q * q, -1, keepdims=True) + NORM_EPS) * (hd ** -0.5)
        k = k * lax.rsqrt(jnp.sum(k * k, -1, keepdims=True) + NORM_EPS)
        beta = bg[:, h:h + 1]
        grow = gc_row[GDN_HEADS + h:GDN_HEADS + h + 1, :]
        decay = jnp.exp(jnp.where(causal, gcols[h] - grow, -jnp.inf))
        kb = k * beta
        kbf = k.astype(BF16)
        lows.append(jnp.where(diag, 0.0, _dot_nt(kb.astype(BF16), kbf) * decay))
        intras.append(_dot_nt(q.astype(BF16), kbf) * decay)
        rhss.append(jnp.concatenate([v * beta, kb * egcs[h]], axis=-1).astype(BF16))
        qs.append(q)
        ks.append(k)

    invs = _inv_unit_lower(lows, xor_ij, c)
    for h in heads:
        sol = _dot(invs[h].astype(BF16), rhss[h])
        cols = slice(h * hd, (h + 1) * hd)
        u_ref[0, :, cols] = sol[:, :hd]
        w_ref[0, :, cols] = sol[:, hd:].astype(BF16)
        qd_ref[0, :, cols] = (qs[h] * egcs[h]).astype(BF16)
        kd_ref[0, :, cols] = (ks[h] * jnp.exp(gends[h] - gcols[h])).astype(BF16)
        packed = intras[h][:, :c]
        for n in range(1, ts // c):
            packed = packed + intras[h][:, n * c:(n + 1) * c]
        ic_ref[0, :, h * c:(h + 1) * c] = packed.astype(BF16)
        for n in range(ts // c):
            gl_ref[0, n, h:h + 1, :] = jnp.broadcast_to(jnp.exp(gends[h][n * c:n * c + 1]), (1, hd))


def gdn_prep(o1, bg, bgt, conv_w):
    bsz, s, _ = o1.shape
    ts = min(GDN_TILE, s)
    c = GDN_CHUNK
    nc = ts // c
    qkv_w = 3 * GDN_WIDTH
    r = np.arange(ts)
    same = (r[:, None] // c) == (r[None, :] // c)
    ltri = jnp.asarray((same & (r[:, None] >= r[None, :])).astype(np.float32))
    utri = jnp.asarray((same & (r[:, None] <= r[None, :])).astype(np.float32))
    same = jnp.asarray(same.astype(np.float32))
    hb = ts // HALO
    tok = lambda b, i: (b, i, 0)
    const = lambda b, i: (0, 0)
    wide = jax.ShapeDtypeStruct((bsz, s, GDN_WIDTH), BF16)
    return pl.pallas_call(
        _gdn_prep_kernel,
        grid=(bsz, s // ts),
        in_specs=[pl.BlockSpec((1, ts, qkv_w), tok),
                  pl.BlockSpec((1, HALO, qkv_w), lambda b, i: (b, jnp.maximum(i * hb - 1, 0), 0)),
                  pl.BlockSpec((GDN_CONV, qkv_w), const),
                  pl.BlockSpec((1, ts, LANES), tok),
                  pl.BlockSpec((1, 8, ts), lambda b, i: (b, 0, i)),
                  pl.BlockSpec((ts, ts), const),
                  pl.BlockSpec((ts, ts), const),
                  pl.BlockSpec((ts, ts), const)],
        out_specs=[pl.BlockSpec((1, ts, GDN_WIDTH), tok),
                   pl.BlockSpec((1, ts, GDN_WIDTH), tok),
                   pl.BlockSpec((1, ts, GDN_WIDTH), tok),
                   pl.BlockSpec((1, ts, GDN_WIDTH), tok),
                   pl.BlockSpec((1, ts, GDN_HEADS * c), tok),
                   pl.BlockSpec((1, nc, GDN_HEADS, GDN_HEAD_DIM), lambda b, i: (b, i, 0, 0))],
        out_shape=[wide, wide, jax.ShapeDtypeStruct((bsz, s, GDN_WIDTH), F32), wide,
                   jax.ShapeDtypeStruct((bsz, s, GDN_HEADS * c), BF16),
                   jax.ShapeDtypeStruct((bsz, s // c, GDN_HEADS, GDN_HEAD_DIM), F32)],
        compiler_params=_cparams("parallel", "parallel"),
        name="gdn_prep",
    )(o1, o1, conv_w.astype(F32), bg, bgt, ltri, utri, same)


def _gdn_scan_kernel(qd_ref, kd_ref, u_ref, w_ref, ic_ref, gl_ref, o_ref, state_ref):
    ts = qd_ref.shape[1]
    c = GDN_CHUNK
    hd = GDN_HEAD_DIM

    @pl.when(pl.program_id(1) == 0)
    def _():
        state_ref[...] = jnp.zeros_like(state_ref)

    heads = range(GDN_HEADS)
    cols = [slice(h * hd, (h + 1) * hd) for h in heads]
    sts = [state_ref[h] for h in heads]
    for n in range(ts // c):
        rows = slice(n * c, (n + 1) * c)
        stbs = [st.astype(BF16) for st in sts]
        vbs = [(u_ref[0, rows, cols[h]] - _dot(w_ref[0, rows, cols[h]], stbs[h])).astype(BF16) for h in heads]
        for h in heads:
            o_ref[0, rows, cols[h]] = (_dot(qd_ref[0, rows, cols[h]], stbs[h])
                                       + _dot(ic_ref[0, rows, h * c:(h + 1) * c], vbs[h]))
        sts = [sts[h] * gl_ref[0, n, h:h + 1, :] + _dot_tn(kd_ref[0, rows, cols[h]], vbs[h]) for h in heads]
    for h in heads:
        state_ref[h] = sts[h]


def gdn_scan(qd, kd, u, w, ic, gl):
    bsz, s, _ = qd.shape
    ts = min(GDN_TILE, s)
    c = GDN_CHUNK
    tok = lambda b, i: (b, i, 0)
    return pl.pallas_call(
        _gdn_scan_kernel,
        grid=(bsz, s // ts),
        in_specs=[pl.BlockSpec((1, ts, GDN_WIDTH), tok)] * 4 + [
            pl.BlockSpec((1, ts, GDN_HEADS * c), tok),
            pl.BlockSpec((1, ts // c, GDN_HEADS, GDN_HEAD_DIM), lambda b, i: (b, i, 0, 0))],
        out_specs=pl.BlockSpec((1, ts, GDN_WIDTH), tok),
        out_shape=jax.ShapeDtypeStruct((bsz, s, GDN_WIDTH), F32),
        scratch_shapes=[pltpu.VMEM((GDN_HEADS, GDN_HEAD_DIM, GDN_HEAD_DIM), F32)],
        compiler_params=_cparams("parallel", "arbitrary"),
        name="gdn_scan",
    )(qd, kd, u, w, ic, gl)


def _even_post_kernel(o_ref, z_ref, scb_ref, scc_ref, sch_ref, scc_halo_ref, sch_halo_ref,
                      nw_ref, cw_ref, y_ref):
    hd = GDN_HEAD_DIM
    i = pl.program_id(1)
    o = o_ref[0]
    z = z_ref[0].astype(F32)
    nw = nw_ref[...]
    for h in range(GDN_HEADS):
        cols = slice(h * hd, (h + 1) * hd)
        oh = o[:, cols]
        on = oh * lax.rsqrt(jnp.mean(oh * oh, -1, keepdims=True) + NORM_EPS) * nw
        y_ref[0, :, cols] = (on * _silu(z[:, cols])).astype(BF16)
    ch = scc_ref[0].astype(F32) * sch_ref[0].astype(F32)
    ch_halo = jnp.where(i > 0, scc_halo_ref[0].astype(F32) * sch_halo_ref[0].astype(F32), 0.0)
    yb = scb_ref[0].astype(F32) * _causal_conv(ch, ch_halo, cw_ref[...], SC_CONV)
    y_ref[0, :, GDN_WIDTH:] = yb.astype(BF16)


def even_post(o, o1, norm_w, sc_conv_w):
    bsz, s, _ = o.shape
    tm = min(TOKEN_TILE, s)
    hb = tm // HALO
    wb = 512
    blk = lambda j: pl.BlockSpec((1, tm, wb), lambda b, i, j=j: (b, i, j))
    halo = lambda j: pl.BlockSpec((1, HALO, wb), lambda b, i, j=j: (b, jnp.maximum(i * hb - 1, 0), j))
    const = lambda b, i: (0, 0)
    return pl.pallas_call(
        _even_post_kernel,
        grid=(bsz, s // tm),
        in_specs=[pl.BlockSpec((1, tm, GDN_WIDTH), lambda b, i: (b, i, 0)),
                  blk(3), blk(4), blk(5), blk(6), halo(5), halo(6),
                  pl.BlockSpec((1, GDN_HEAD_DIM), const),
                  pl.BlockSpec((SC_CONV, SC_WIDTH), const)],
        out_specs=pl.BlockSpec((1, tm, GDN_WIDTH + SC_WIDTH), lambda b, i: (b, i, 0)),
        out_shape=jax.ShapeDtypeStruct((bsz, s, GDN_WIDTH + SC_WIDTH), BF16),
        compiler_params=_cparams("parallel", "parallel"),
        name="even_post",
    )(o, o1, o1, o1, o1, o1, o1, norm_w.astype(F32).reshape(1, -1), sc_conv_w.astype(F32))


def _layer_norm(r, g, b):
    mu = jnp.mean(r, -1, keepdims=True)
    rc = r - mu
    var = jnp.mean(rc * rc, -1, keepdims=True)
    return rc * lax.rsqrt(var + LN_EPS) * g + b


def _route(logits_t, bias):
    scores = _sigmoid(logits_t)
    biased = scores + bias
    t = logits_t.shape[1]
    epg = EXPERTS_PER_GROUP
    sub = lax.broadcasted_iota(jnp.int32, (epg, t), 0).astype(F32)
    best = None
    for g in range(N_GROUPS):
        bgp = biased[g * epg:(g + 1) * epg]
        m1 = jnp.max(bgp, axis=0, keepdims=True)
        i1 = jnp.min(jnp.where(bgp == m1, sub, float(epg)), axis=0, keepdims=True)
        rest = jnp.where(sub == i1, -jnp.inf, bgp)
        m2 = jnp.max(rest, axis=0, keepdims=True)
        i2 = jnp.min(jnp.where(rest == m2, sub, float(epg)), axis=0, keepdims=True)
        gs = m1 + m2
        if best is None:
            best, e0, e1 = gs, i1, i2
        else:
            better = gs > best
            best = jnp.where(better, gs, best)
            e0 = jnp.where(better, float(g * epg) + i1, e0)
            e1 = jnp.where(better, float(g * epg) + i2, e1)
    eio = lax.broadcasted_iota(jnp.int32, scores.shape, 0).astype(F32)
    hit0 = eio == e0
    hit1 = eio == e1
    s0 = jnp.sum(jnp.where(hit0, scores, 0.0), axis=0, keepdims=True)
    s1 = jnp.sum(jnp.where(hit1, scores, 0.0), axis=0, keepdims=True)
    tot = s0 + s1
    return e0.astype(jnp.int32), e1.astype(jnp.int32), s0 / tot, s1 / tot, hit0, hit1


def _mix_out_kernel(y_ref, wo_ref, x_ref, moda_ref, modb_ref, lng_ref, lnb_ref, rwt_ref, rb_ref,
                    upper_ref, x1_ref, hm_ref, seli_ref, selw_ref, cnt_ref):
    d = D_MODEL
    gate = moda_ref[0, 0][:, 2 * d:]
    modb = modb_ref[0, 0]
    y = _dot(y_ref[0], wo_ref[...])
    x1 = _layer_norm(DN_ALPHA * x_ref[0] + (1.0 + gate) * y, lng_ref[...], lnb_ref[...])
    x1_ref[0] = x1
    hm = x1 * (1.0 + modb[:, d:2 * d]) + modb[:, :d]
    hm_hi = hm.astype(BF16)
    hm_ref[0] = hm_hi

    hm_lo = (hm - hm_hi.astype(F32)).astype(BF16)
    rw = rwt_ref[...]
    hi_lo = _dot_nt(rw, hm_hi)
    logits_t = hi_lo[:N_EXPERTS] + hi_lo[N_EXPERTS:] + _dot_nt(rw[:N_EXPERTS], hm_lo)
    e0, e1, w0, w1, hit0, hit1 = _route(logits_t, rb_ref[...])
    member = jnp.where(hit0 | hit1, 1.0, 0.0)
    before = _dot(member.astype(BF16), upper_ref[...])
    rank0 = jnp.sum(jnp.where(hit0, before, 0.0), axis=0, keepdims=True).astype(jnp.int32)
    rank1 = jnp.sum(jnp.where(hit1, before, 0.0), axis=0, keepdims=True).astype(jnp.int32)
    zi = jnp.zeros_like(e0)
    seli_ref[0] = jnp.concatenate([e0, e1, rank0, rank1, zi, zi, zi, zi], axis=0)
    zf = jnp.zeros_like(w0)
    selw_ref[0] = jnp.concatenate([w0, w1, zf, zf, zf, zf, zf, zf], axis=0)
    cnt_ref[0, 0] = jnp.broadcast_to(jnp.sum(member, axis=1, keepdims=True), (N_EXPERTS, LANES))


def mix_out(ymix, w_out, x, moda, modb, ln_g, ln_b, router_w, router_bias):
    bsz, s, d = x.shape
    tm = min(TOKEN_TILE, s)
    nt = s // tm
    r = np.arange(tm)
    upper = jnp.asarray((r[:, None] < r[None, :]).astype(np.float32), dtype=BF16)
    rw_f32 = router_w.T.astype(F32)
    rw_hi = rw_f32.astype(BF16)
    rwt = jnp.concatenate([rw_hi, (rw_f32 - rw_hi.astype(F32)).astype(BF16)], axis=0)
    tok = lambda b, i: (b, i, 0)
    const = lambda b, i: (0, 0)
    modspec = pl.BlockSpec((1, 1, 1, 3 * d), lambda b, i: (b, 0, 0, 0))
    row8 = pl.BlockSpec((1, 8, tm), lambda b, i: (b, 0, i))
    return pl.pallas_call(
        _mix_out_kernel,
        grid=(bsz, nt),
        in_specs=[pl.BlockSpec((1, tm, d), tok),
                  pl.BlockSpec((d, d), const),
                  pl.BlockSpec((1, tm, d), tok),
                  modspec, modspec,
                  pl.BlockSpec((1, d), const), pl.BlockSpec((1, d), const),
                  pl.BlockSpec((2 * N_EXPERTS, d), const),
                  pl.BlockSpec((N_EXPERTS, 1), const),
                  pl.BlockSpec((tm, tm), const)],
        out_specs=[pl.BlockSpec((1, tm, d), tok),
                   pl.BlockSpec((1, tm, d), tok),
                   row8, row8,
                   pl.BlockSpec((1, 1, N_EXPERTS, LANES), lambda b, i: (b, i, 0, 0))],
        out_shape=[jax.ShapeDtypeStruct((bsz, s, d), F32),
                   jax.ShapeDtypeStruct((bsz, s, d), BF16),
                   jax.ShapeDtypeStruct((bsz, 8, s), jnp.int32),
                   jax.ShapeDtypeStruct((bsz, 8, s), F32),
                   jax.ShapeDtypeStruct((bsz, nt, N_EXPERTS, LANES), F32)],
        compiler_params=_cparams("parallel", "parallel"),
        name="mix_out",
    )(ymix, w_out.astype(BF16), x, moda, modb, ln_g.reshape(1, d), ln_b.reshape(1, d),
      rwt, router_bias.astype(F32).reshape(N_EXPERTS, 1), upper)


def _experts_kernel(be_ref, next_ref, nused_ref, xs_ref, wg_hbm, wu_hbm, wd_hbm, ys_ref,
                    wgf_ref, wuf_ref, wdf_ref, wgb_ref, wub_ref, wdb_ref, sem_ref, slot_ref, *, layer):
    i = pl.program_id(0)

    def weight_copies(e, slot):
        return [pltpu.make_async_copy(src.at[layer, e], dst.at[slot], sem_ref.at[n, slot])
                for n, (src, dst) in enumerate(((wg_hbm, wgf_ref), (wu_hbm, wuf_ref), (wd_hbm, wdf_ref)))]

    @pl.when(i == 0)
    def _():
        slot_ref[0] = 0
        for cp in weight_copies(be_ref[0], 0):
            cp.start()

    @pl.when((i == 0) | (be_ref[i] != be_ref[jnp.maximum(i - 1, 0)]))
    def _():
        slot = slot_ref[0]
        for cp in weight_copies(be_ref[i], slot):
            cp.wait()

        @pl.when(next_ref[i] >= 0)
        def _():
            for cp in weight_copies(next_ref[i], 1 - slot):
                cp.start()

        wgb_ref[...] = wgf_ref[slot].astype(BF16)
        wub_ref[...] = wuf_ref[slot].astype(BF16)
        wdb_ref[...] = wdf_ref[slot].astype(BF16)
        slot_ref[0] = 1 - slot

    @pl.when(i < nused_ref[0])
    def _():
        x = xs_ref[...]
        act = (_silu(_dot(x, wgb_ref[...])) * _dot(x, wub_ref[...])).astype(BF16)
        ys_ref[...] = _dot(act, wdb_ref[...]).astype(BF16)

    @pl.when(i >= nused_ref[0])
    def _():
        ys_ref[...] = jnp.zeros_like(ys_ref)


def moe_experts(xs, block_expert, n_used, w_gate, w_up, w_down, layer):
    rows, d = xs.shape
    nb = rows // MOE_BLOCK
    later = jnp.where(block_expert[None, :] > block_expert[:, None], block_expert[None, :], N_EXPERTS)
    next_expert = jnp.min(later, axis=1)
    next_expert = jnp.where(next_expert == N_EXPERTS, -1, next_expert).astype(jnp.int32)
    hbm = pl.BlockSpec(memory_space=pl.ANY)
    rows_spec = pl.BlockSpec((MOE_BLOCK, d), lambda i, be, nx, nu: (i, 0))
    return pl.pallas_call(
        functools.partial(_experts_kernel, layer=layer),
        grid_spec=pltpu.PrefetchScalarGridSpec(
            num_scalar_prefetch=3,
            grid=(nb,),
            in_specs=[rows_spec, hbm, hbm, hbm],
            out_specs=rows_spec,
            scratch_shapes=[pltpu.VMEM((2, d, D_EXPERT), F32), pltpu.VMEM((2, d, D_EXPERT), F32),
                            pltpu.VMEM((2, D_EXPERT, d), F32),
                            pltpu.VMEM((d, D_EXPERT), BF16), pltpu.VMEM((d, D_EXPERT), BF16),
                            pltpu.VMEM((D_EXPERT, d), BF16),
                            pltpu.SemaphoreType.DMA((3, 2)), pltpu.SMEM((1,), jnp.int32)]),
        out_shape=jax.ShapeDtypeStruct((rows, d), BF16),
        compiler_params=_cparams("arbitrary"),
        name="moe_experts",
    )(block_expert, next_expert, n_used, xs, w_gate, w_up, w_down)


def _plan_kernel(seli_ref, base_ref, dest_ref):
    sel = seli_ref[0]
    base = base_ref[0, 0]
    eio = lax.broadcasted_iota(jnp.int32, (N_EXPERTS, sel.shape[1]), 0)
    out = []
    for kk in range(2):
        first = jnp.sum(jnp.where(eio == sel[kk:kk + 1], base, 0.0), axis=0, keepdims=True)
        out.append(first.astype(jnp.int32) + sel[2 + kk:3 + kk])
    zero = jnp.zeros_like(out[0])
    dest_ref[0] = jnp.concatenate(out + [zero] * 6, axis=0)


def dispatch_plan(seli, base):
    bsz, _, s = seli.shape
    nt = base.shape[1]
    tm = s // nt
    return pl.pallas_call(
        _plan_kernel,
        grid=(bsz, nt),
        in_specs=[pl.BlockSpec((1, 8, tm), lambda b, i: (b, 0, i)),
                  pl.BlockSpec((1, 1, N_EXPERTS, 1), lambda b, i: (b, i, 0, 0))],
        out_specs=pl.BlockSpec((1, 8, tm), lambda b, i: (b, 0, i)),
        out_shape=jax.ShapeDtypeStruct((bsz, 8, s), jnp.int32),
        compiler_params=_cparams("parallel", "parallel"),
        name="dispatch_plan",
    )(seli, base)


def _invert_kernel(dest_ref, pad_lo_ref, pad_hi_ref, slot_tok_ref):
    t = dest_ref.shape[0] // 2

    def clear_range(e, carry):
        def clear(r, c):
            slot_tok_ref[r] = lax.rem(r, t)
            return c
        return lax.fori_loop(pad_lo_ref[e], pad_hi_ref[e], clear, carry)

    def put(a, carry):
        slot_tok_ref[dest_ref[a]] = a
        slot_tok_ref[dest_ref[t + a]] = a
        return carry

    lax.fori_loop(0, pad_lo_ref.shape[0], clear_range, 0)
    lax.fori_loop(0, t, put, 0, unroll=8)


def invert_slots(dest_flat, pad_lo, pad_hi, rows):
    smem = pl.BlockSpec(memory_space=pltpu.SMEM)
    return pl.pallas_call(
        _invert_kernel,
        in_specs=[smem, smem, smem],
        out_specs=smem,
        out_shape=jax.ShapeDtypeStruct((rows,), jnp.int32),
        name="invert_slots",
    )(dest_flat, pad_lo, pad_hi)


def _moe_combine_kernel(g0_ref, g1_ref, wt_ref, x_ref, mod_ref, lng_ref, lnb_ref, o_ref):
    d = D_MODEL
    gate = mod_ref[0, 0][:, 2 * d:]
    wt = wt_ref[0]
    y = g0_ref[0, 0].astype(F32) * wt[:, 0:1] + g1_ref[0, 0].astype(F32) * wt[:, 1:2]
    o_ref[0] = _layer_norm(DN_ALPHA * x_ref[0] + (1.0 + gate) * y, lng_ref[...], lnb_ref[...])


def moe_combine(g, wt, x1, modb, ln_g, ln_b):
    bsz, s, d = x1.shape
    tm = min(TOKEN_TILE, s)
    tok = lambda b, i: (b, i, 0)
    const = lambda b, i: (0, 0)
    return pl.pallas_call(
        _moe_combine_kernel,
        grid=(bsz, s // tm),
        in_specs=[pl.BlockSpec((1, 1, tm, d), lambda b, i: (0, b, i, 0)),
                  pl.BlockSpec((1, 1, tm, d), lambda b, i: (1, b, i, 0)),
                  pl.BlockSpec((1, tm, 8), tok), pl.BlockSpec((1, tm, d), tok),
                  pl.BlockSpec((1, 1, 1, 3 * d), lambda b, i: (b, 0, 0, 0)),
                  pl.BlockSpec((1, d), const), pl.BlockSpec((1, d), const)],
        out_specs=pl.BlockSpec((1, tm, d), tok),
        out_shape=jax.ShapeDtypeStruct((bsz, s, d), F32),
        compiler_params=_cparams("parallel", "parallel"),
        name="moe_combine",
    )(g, g, wt, x1, modb, ln_g.reshape(1, d), ln_b.reshape(1, d))


def moe_layer(x1, hm, seli, selw, cnt, modb, ln_g, ln_b, w_gate, w_up, w_down, layer):
    bsz, s, d = x1.shape
    t = bsz * s
    tm = min(TOKEN_TILE, s)
    blk = MOE_BLOCK
    cnt = cnt[..., 0].reshape(-1, N_EXPERTS).astype(jnp.int32)
    tile_off = jnp.cumsum(cnt, axis=0) - cnt
    counts = cnt.sum(0)
    padded = (counts + blk - 1) // blk * blk
    pend = jnp.cumsum(padded)
    pstart = pend - padded
    nb = -(-(2 * t) // blk) + N_EXPERTS
    rows = nb * blk
    base = (pstart[None, :] + tile_off).astype(F32).reshape(bsz, s // tm, N_EXPERTS, 1)
    dest = dispatch_plan(seli, base)
    dest = jnp.swapaxes(dest[:, 0:2, :], 0, 1).reshape(2 * t)
    pad_lo = jnp.concatenate([pstart + counts, pend[-1:]]).astype(jnp.int32)
    pad_hi = jnp.concatenate([pend, jnp.full((1,), rows)]).astype(jnp.int32)
    slot_tok = invert_slots(dest, pad_lo, pad_hi, rows)
    starts = jnp.arange(nb, dtype=jnp.int32) * blk
    block_expert = jnp.minimum(jnp.sum((pend[None, :] <= starts[:, None]).astype(jnp.int32), axis=1),
                               N_EXPERTS - 1)
    n_used = (pend[-1] // blk).astype(jnp.int32).reshape(1)
    xs = hm.reshape(t, d).at[slot_tok].get(mode="promise_in_bounds")
    ys = moe_experts(xs, block_expert, n_used, w_gate, w_up, w_down, layer)
    g = ys.at[dest].get(mode="promise_in_bounds").reshape(2, bsz, s, d)
    wt = jnp.swapaxes(selw, 1, 2)
    return moe_combine(g, wt, x1, modb, ln_g, ln_b)


MLA_IN_COLS = Q_LORA + KV_LORA + 2 * LANES
MLA_QK = 2 * LANES


def _mla_in_kernel(x_ref, mod_ref, pos_ref, invf_ref, win_ref, qn_ref, kvn_ref, wq_ref, wkv_ref,
                   q_ref, k_ref, v_ref):
    d = D_MODEL
    nh = MLA_HEADS
    mod = mod_ref[0, 0]
    h = (x_ref[0] * (1.0 + mod[:, d:2 * d]) + mod[:, :d]).astype(BF16)
    proj = _dot(h, win_ref[...])
    ang = pos_ref[0].astype(F32) * invf_ref[...]
    lane = lax.broadcasted_iota(jnp.int32, ang.shape, 1)
    cos = jnp.where(lane < QK_ROPE, jnp.cos(ang), 0.0)
    sin = jnp.where(lane < QK_ROPE, jnp.sin(ang), 0.0)

    qa = proj[:, :Q_LORA]
    qa = (qa * lax.rsqrt(jnp.mean(qa * qa, -1, keepdims=True) + NORM_EPS) * qn_ref[...]).astype(BF16)
    kva = proj[:, Q_LORA:Q_LORA + KV_LORA]
    kva = (kva * lax.rsqrt(jnp.mean(kva * kva, -1, keepdims=True) + NORM_EPS) * kvn_ref[...]).astype(BF16)
    kr0 = Q_LORA + KV_LORA
    k_rope = (proj[:, kr0:kr0 + LANES] * cos + proj[:, kr0 + LANES:kr0 + 2 * LANES] * sin).astype(BF16)

    scale = (QK_NOPE + QK_ROPE) ** -0.5 * LOG2_E
    hw = nh * LANES
    q_nope = _dot(qa, wq_ref[:, :hw])
    q_rope = _dot(qa, wq_ref[:, hw:2 * hw])
    q_rot = _dot(qa, wq_ref[:, 2 * hw:])
    k_nope = _dot(kva, wkv_ref[:, :hw])
    v_ref[0] = _dot(kva, wkv_ref[:, hw:]).astype(BF16)
    for hh in range(nh):
        cols = slice(hh * LANES, (hh + 1) * LANES)
        q_ref[0, :, hh * MLA_QK:hh * MLA_QK + LANES] = (q_nope[:, cols] * scale).astype(BF16)
        q_ref[0, :, hh * MLA_QK + LANES:(hh + 1) * MLA_QK] = (
            (q_rope[:, cols] * cos + q_rot[:, cols] * sin) * scale).astype(BF16)
        k_ref[0, :, hh * MLA_QK:hh * MLA_QK + LANES] = k_nope[:, cols].astype(BF16)
        k_ref[0, :, hh * MLA_QK + LANES:(hh + 1) * MLA_QK] = k_rope


def _rope_cols(w):
    half = QK_ROPE // 2
    pad = [(0, 0)] * (w.ndim - 1) + [(0, LANES - QK_ROPE)]
    rot = jnp.concatenate([-w[..., half:], w[..., :half]], axis=-1)
    return jnp.pad(w, pad), jnp.pad(rot, pad)


def mla_in(x, mod, positions, w_in, q_a_norm, w_q_b, kv_a_norm, w_kv_b):
    bsz, s, d = x.shape
    tm = min(TOKEN_TILE, s)
    nh = MLA_HEADS
    kr, kr_rot = _rope_cols(w_in[:, Q_LORA + KV_LORA:])
    win = jnp.concatenate([w_in[:, :Q_LORA + KV_LORA], kr, kr_rot], axis=1).astype(BF16)
    wq = w_q_b.reshape(Q_LORA, nh, QK_NOPE + QK_ROPE)
    qr, qr_rot = _rope_cols(wq[..., QK_NOPE:])
    wq = jnp.concatenate([wq[..., :QK_NOPE].reshape(Q_LORA, -1), qr.reshape(Q_LORA, -1),
                          qr_rot.reshape(Q_LORA, -1)], axis=1).astype(BF16)
    wkv = w_kv_b.reshape(KV_LORA, nh, QK_NOPE + V_DIM)
    wkv = jnp.concatenate([wkv[..., :QK_NOPE].reshape(KV_LORA, -1),
                           wkv[..., QK_NOPE:].reshape(KV_LORA, -1)], axis=1).astype(BF16)
    inv = ROPE_THETA ** (-np.arange(0, QK_ROPE, 2, dtype=np.float32) / QK_ROPE)
    invf = np.zeros((1, LANES), np.float32)
    invf[0, :QK_ROPE] = np.concatenate([inv, inv])
    tok = lambda b, i: (b, i, 0)
    const = lambda b, i: (0, 0)
    return pl.pallas_call(
        _mla_in_kernel,
        grid=(bsz, s // tm),
        in_specs=[pl.BlockSpec((1, tm, d), tok),
                  pl.BlockSpec((1, 1, 1, 3 * d), lambda b, i: (b, 0, 0, 0)),
                  pl.BlockSpec((1, tm, 1), tok),
                  pl.BlockSpec((1, LANES), const),
                  pl.BlockSpec((d, MLA_IN_COLS), const),
                  pl.BlockSpec((1, Q_LORA), const),
                  pl.BlockSpec((1, KV_LORA), const),
                  pl.BlockSpec((Q_LORA, 3 * nh * LANES), const),
                  pl.BlockSpec((KV_LORA, 2 * nh * LANES), const)],
        out_specs=[pl.BlockSpec((1, tm, nh * MLA_QK), tok),
                   pl.BlockSpec((1, tm, nh * MLA_QK), tok),
                   pl.BlockSpec((1, tm, nh * V_DIM), tok)],
        out_shape=[jax.ShapeDtypeStruct((bsz, s, nh * MLA_QK), BF16),
                   jax.ShapeDtypeStruct((bsz, s, nh * MLA_QK), BF16),
                   jax.ShapeDtypeStruct((bsz, s, nh * V_DIM), BF16)],
        compiler_params=_cparams("parallel", "parallel"),
        name="mla_in",
    )(x, mod, positions.reshape(bsz, s, 1), jnp.asarray(invf), win,
      q_a_norm.astype(F32).reshape(1, -1), kv_a_norm.astype(F32).reshape(1, -1), wq, wkv)


def _attn_kernel(q_ref, k_ref, v_ref, o_ref, m_ref, l_ref, acc_ref):
    tq = q_ref.shape[1]
    i = pl.program_id(2)
    m_ref[...] = jnp.full(m_ref.shape, -jnp.inf, F32)
    l_ref[...] = jnp.zeros(l_ref.shape, F32)
    acc_ref[...] = jnp.zeros(acc_ref.shape, F32)

    groups = [slice(g * ATTN_ROWS, (g + 1) * ATTN_ROWS) for g in range(tq // ATTN_ROWS)]

    def update(off, widths, masked):
        scs = []
        for rows, width in zip(groups, widths):
            sc = _dot_nt(q_ref[0, rows, :], k_ref[0, pl.ds(off, width), :])
            if masked:
                qi = lax.broadcasted_iota(jnp.int32, sc.shape, 0) + rows.start
                ki = lax.broadcasted_iota(jnp.int32, sc.shape, 1)
                sc = jnp.where(ki <= qi, sc, -jnp.inf)
            scs.append(sc)
        m_olds = [m_ref[rows] for rows in groups]
        m_news = [jnp.maximum(m_old, jnp.max(sc, -1, keepdims=True)) for m_old, sc in zip(m_olds, scs)]
        ps = [jnp.exp2(sc - jnp.concatenate([m_new] * (width // LANES), axis=1))
              for sc, m_new, width in zip(scs, m_news, widths)]
        alphas = [jnp.exp2(m_old - m_new) for m_old, m_new in zip(m_olds, m_news)]
        pvs = [_dot(p.astype(BF16), v_ref[0, pl.ds(off, width), :]) for p, width in zip(ps, widths)]
        for rows, m_new, alpha, p, pv in zip(groups, m_news, alphas, ps, pvs):
            l_ref[rows] = alpha * l_ref[rows] + jnp.sum(p, -1, keepdims=True)
            acc_ref[rows] = alpha * acc_ref[rows] + pv
            m_ref[rows] = m_new

    def full_block(j, carry):
        update(pl.multiple_of(j * tq, tq), [tq] * len(groups), False)
        return carry

    lax.fori_loop(0, i, full_block, 0)
    update(pl.multiple_of(i * tq, tq), [rows.stop for rows in groups], True)
    o_ref[0] = (acc_ref[...] / l_ref[...]).astype(BF16)


def attention(q, k, v):
    bsz, s, _ = q.shape
    tq = min(ATTN_TILE, s)
    return pl.pallas_call(
        _attn_kernel,
        grid=(bsz, MLA_HEADS, s // tq),
        in_specs=[pl.BlockSpec((1, tq, MLA_QK), lambda b, h, i: (b, i, h)),
                  pl.BlockSpec((1, s, MLA_QK), lambda b, h, i: (b, 0, h)),
                  pl.BlockSpec((1, s, V_DIM), lambda b, h, i: (b, 0, h))],
        out_specs=pl.BlockSpec((1, tq, V_DIM), lambda b, h, i: (b, i, h)),
        out_shape=jax.ShapeDtypeStruct((bsz, s, MLA_HEADS * V_DIM), BF16),
        scratch_shapes=[pltpu.VMEM((tq, LANES), F32), pltpu.VMEM((tq, LANES), F32), pltpu.VMEM((tq, V_DIM), F32)],
        compiler_params=_cparams("parallel", "parallel", "arbitrary"),
        name="attention",
    )(q, k, v)


def kernel(x, c, positions, ada_w, ada_b, ln_g, ln_b, w_in_e, gdn_conv_w, gdn_a_log, gdn_dt_bias,
           gdn_norm_w, sc_conv_w, w_out_e, w_in_o, q_a_norm, w_q_b, kv_a_norm, w_kv_b, w_out_o,
           router_w, router_bias, w_gate, w_up, w_down):
    bsz, s, d = x.shape
    depth = ada_w.shape[0]
    mod = ada_mod(c, ada_w, ada_b).reshape(depth, 2, bsz, 1, 3 * d)

    def layers(x, mod, positions):
        for i in range(depth):
            j = i // 2
            moda, modb = mod[i, 0][:, None], mod[i, 1][:, None]
            if i % 2 == 0:
                o1, bg, bgt = even_in_proj(x, moda, w_in_e[j], gdn_a_log[j], gdn_dt_bias[j])
                qd, kd, u, w, ic, gl = gdn_prep(o1, bg, bgt, gdn_conv_w[j])
                o = gdn_scan(qd, kd, u, w, ic, gl)
                ymix = even_post(o, o1, gdn_norm_w[j], sc_conv_w[j])
                w_out = w_out_e[j]
            else:
                q, k, v = mla_in(x, moda, positions, w_in_o[j], q_a_norm[j], w_q_b[j], kv_a_norm[j],
                                 w_kv_b[j])
                ymix = attention(q, k, v)
                w_out = w_out_o[j]
            x1, hm, seli, selw, cnt = mix_out(ymix, w_out, x, moda, modb, ln_g[i, 0], ln_b[i, 0],
                                              router_w, router_bias)
            x = moe_layer(x1, hm, seli, selw, cnt, modb, ln_g[i, 1], ln_b[i, 1], w_gate, w_up, w_down, i)
        return x

    ng = BATCH_GROUPS if bsz % BATCH_GROUPS == 0 else 1
    gb = bsz // ng
    outs = [layers(x[g * gb:(g + 1) * gb], mod[:, :, g * gb:(g + 1) * gb], positions[g * gb:(g + 1) * gb])
            for g in range(ng)]
    return outs[0] if ng == 1 else jnp.concatenate(outs, axis=0)
```

```python
import functools

import numpy as np
import jax
import jax.numpy as jnp
from jax import lax
from jax.experimental import pallas as pl
from jax.experimental.pallas import tpu as pltpu

F32 = jnp.float32
BF16 = jnp.bfloat16
HIGHEST = lax.Precision.HIGHEST

D_MODEL = 1024
DEPTH = 2
DN_ALPHA = (2.0 * DEPTH) ** 0.25

GDN_HEADS = 4
GDN_HEAD_DIM = 128
GDN_WIDTH = GDN_HEADS * GDN_HEAD_DIM
GDN_CONV = 4
GDN_CHUNK = 64
SC_WIDTH = 512
SC_CONV = 3

MLA_HEADS = 8
Q_LORA = 384
KV_LORA = 256
QK_NOPE = 128
QK_ROPE = 64
V_DIM = 128
ROPE_THETA = 10000.0

N_EXPERTS = 32
N_GROUPS = 4
EXPERTS_PER_GROUP = N_EXPERTS // N_GROUPS
D_EXPERT = 512

NORM_EPS = 1e-6
LN_EPS = 1e-5

LANES = 128
HALO = 16
TOKEN_TILE = 512
GDN_TILE = 256
ATTN_TILE = 1024
ATTN_ROWS = 256
LOG2_E = 1.4426950408889634
MOE_BLOCK = 512
BATCH_GROUPS = 2
CLEAR_CHUNK = 8
VMEM_LIMIT = 48 * 1024 * 1024


def _cparams(*sem):
    return pltpu.CompilerParams(dimension_semantics=sem, vmem_limit_bytes=VMEM_LIMIT)


def _sigmoid(x):
    return 1.0 / (1.0 + jnp.exp(-x))


def _silu(x):
    return x * _sigmoid(x)


def _dot(a, b):
    return jnp.dot(a, b, preferred_element_type=F32)


def _dot_nt(a, b, precision=None):
    return lax.dot_general(a, b, (((1,), (1,)), ((), ())), precision=precision,
                           preferred_element_type=F32)


def _dot_tn(a, b):
    return lax.dot_general(a, b, (((0,), (0,)), ((), ())), preferred_element_type=F32)


def _ada_kernel(c_ref, w_ref, b_ref, o_ref):
    cond = _silu(c_ref[...])
    o_ref[0] = jnp.dot(cond, w_ref[0], precision=HIGHEST, preferred_element_type=F32) + b_ref[0]


def ada_mod(c, ada_w, ada_b):
    nl = ada_w.shape[0] * ada_w.shape[1]
    bsz, d = c.shape
    w = ada_w.reshape(nl, d, 3 * d)
    b = ada_b.reshape(nl, 1, 3 * d)
    return pl.pallas_call(
        _ada_kernel,
        grid=(nl, 3),
        in_specs=[pl.BlockSpec((bsz, d), lambda l, j: (0, 0)),
                  pl.BlockSpec((1, d, d), lambda l, j: (l, 0, j)),
                  pl.BlockSpec((1, 1, d), lambda l, j: (l, 0, j))],
        out_specs=pl.BlockSpec((1, bsz, d), lambda l, j: (l, 0, j)),
        out_shape=jax.ShapeDtypeStruct((nl, bsz, 3 * d), F32),
        compiler_params=_cparams("parallel", "parallel"),
        name="ada_mod",
    )(c, w, b)


EVEN_MAIN = 3 * GDN_WIDTH + GDN_WIDTH + 3 * SC_WIDTH


def _even_in_kernel(x_ref, mod_ref, w1_ref, w2_ref, alog_ref, dtb_ref, o1_ref, o2_ref, o2t_ref):
    d = D_MODEL
    mod = mod_ref[0, 0]
    h = (x_ref[0] * (1.0 + mod[:, d:2 * d]) + mod[:, :d]).astype(BF16)
    for j in range(EVEN_MAIN // 512):
        o1_ref[0, :, j * 512:(j + 1) * 512] = _dot(h, w1_ref[:, j * 512:(j + 1) * 512]).astype(BF16)
    r = _dot(h, w2_ref[...])
    lane = lax.broadcasted_iota(jnp.int32, r.shape, 1)
    a = r + dtb_ref[...]
    softplus = jnp.maximum(a, 0.0) + jnp.log(1.0 + jnp.exp(-jnp.abs(a)))
    bg = jnp.where(lane < GDN_HEADS, _sigmoid(r), -jnp.exp(alog_ref[...]) * softplus)
    o2_ref[0] = bg
    o2t_ref[0] = bg.T[:8]


def even_in_proj(x, mod, w_in, a_log, dt_bias, b0=0):
    _, s, d = x.shape
    bsz = mod.shape[0]
    tm = min(TOKEN_TILE, s)
    q_end = 4 * GDN_WIDTH
    w1 = jnp.concatenate([w_in[:, :q_end], w_in[:, q_end + 2 * GDN_HEADS:]], axis=1).astype(BF16)
    w2 = jnp.pad(w_in[:, q_end:q_end + 2 * GDN_HEADS], ((0, 0), (0, LANES - 2 * GDN_HEADS))).astype(BF16)
    alog = jnp.pad(a_log.astype(F32), (GDN_HEADS, LANES - 2 * GDN_HEADS)).reshape(1, LANES)
    dtb = jnp.pad(dt_bias.astype(F32), (GDN_HEADS, LANES - 2 * GDN_HEADS)).reshape(1, LANES)
    const = lambda b, i: (0, 0)
    return pl.pallas_call(
        _even_in_kernel,
        grid=(bsz, s // tm),
        in_specs=[pl.BlockSpec((1, tm, d), lambda b, i: (b + b0, i, 0)),
                  pl.BlockSpec((1, 1, 1, 3 * d), lambda b, i: (b, 0, 0, 0)),
                  pl.BlockSpec((d, EVEN_MAIN), const),
                  pl.BlockSpec((d, LANES), const),
                  pl.BlockSpec((1, LANES), const),
                  pl.BlockSpec((1, LANES), const)],
        out_specs=[pl.BlockSpec((1, tm, EVEN_MAIN), lambda b, i: (b, i, 0)),
                   pl.BlockSpec((1, tm, LANES), lambda b, i: (b, i, 0)),
                   pl.BlockSpec((1, 8, tm), lambda b, i: (b, 0, i))],
        out_shape=[jax.ShapeDtypeStruct((bsz, s, EVEN_MAIN), BF16),
                   jax.ShapeDtypeStruct((bsz, s, LANES), F32),
                   jax.ShapeDtypeStruct((bsz, 8, s), F32)],
        compiler_params=_cparams("parallel", "parallel"),
        name="even_in_proj",
    )(x, mod, w1, w2, alog, dtb)


def _causal_conv(x, halo, w, taps):
    rows = x.shape[0]
    xf = jnp.concatenate([halo, x], axis=0)
    y = w[taps - 1:taps] * x
    for j in range(taps - 1):
        off = HALO - (taps - 1) + j
        y = y + w[j:j + 1] * xf[off:off + rows]
    return y


def _inv_unit_lower(lows, xor_ij, block):
    eye = jnp.where(xor_ij == 0, 1.0, 0.0)
    ms = [eye - jnp.where(xor_ij == 1, low, 0.0) for low in lows]
    s = 2
    while s < block:
        level = (xor_ij >> (s.bit_length() - 1)) == 1
        mbs = [m.astype(BF16) for m in ms]
        cms = [_dot(jnp.where(level, low, 0.0).astype(BF16), mb).astype(BF16) for low, mb in zip(lows, mbs)]
        ms = [m - _dot(mb, cm) for m, mb, cm in zip(ms, mbs, cms)]
        s *= 2
    return ms


def _gdn_prep_kernel(qkv_ref, halo_ref, cw_ref, bg_ref, bgt_ref, ltri_ref, utri_ref, same_ref,
                     qd_ref, kd_ref, u_ref, w_ref, ic_ref, gl_ref):
    ts = qkv_ref.shape[1]
    c = GDN_CHUNK
    hd = GDN_HEAD_DIM
    i = pl.program_id(1)
    x = qkv_ref[0].astype(F32)
    halo = jnp.where(i > 0, halo_ref[0].astype(F32), 0.0)
    y = _silu(_causal_conv(x, halo, cw_ref[...], GDN_CONV))

    bg = bg_ref[0]
    gc_col = jnp.dot(ltri_ref[...], bg, precision=HIGHEST, preferred_element_type=F32)
    gc_row = jnp.dot(bgt_ref[0], utri_ref[...], precision=HIGHEST, preferred_element_type=F32)
    gc_end = jnp.dot(same_ref[...], bg, precision=HIGHEST, preferred_element_type=F32)

    ii = lax.broadcasted_iota(jnp.int32, (ts, ts), 0)
    jj = lax.broadcasted_iota(jnp.int32, (ts, ts), 1)
    xor_ij = ii ^ jj
    causal = (same_ref[...] > 0.0) & (ii >= jj)
    diag = xor_ij == 0

    heads = range(GDN_HEADS)
    gcols = [gc_col[:, GDN_HEADS + h:GDN_HEADS + h + 1] for h in heads]
    gends = [gc_end[:, GDN_HEADS + h:GDN_HEADS + h + 1] for h in heads]
    egcs = [jnp.exp(g) for g in gcols]
    lows, intras, rhss, qs, ks = [], [], [], [], []
    for h in heads:
        q = y[:, h * hd:(h + 1) * hd]
        k = y[:, GDN_WIDTH + h * hd:GDN_WIDTH + (h + 1) * hd]
        v = y[:, 2 * GDN_WIDTH + h * hd:2 * GDN_WIDTH + (h + 1) * hd]
        q = q * lax.rsqrt(jnp.sum(q * q, -1, keepdims=True) + NORM_EPS) * (hd ** -0.5)
        k = k * lax.rsqrt(jnp.sum(k * k, -1, keepdims=True) + NORM_EPS)
        beta = bg[:, h:h + 1]
        grow = gc_row[GDN_HEADS + h:GDN_HEADS + h + 1, :]
        decay = jnp.exp(jnp.where(causal, gcols[h] - grow, -jnp.inf))
        kb = k * beta
        kbf = k.astype(BF16)
        lows.append(jnp.where(diag, 0.0, _dot_nt(kb.astype(BF16), kbf) * decay))
        intras.append(_dot_nt(q.astype(BF16), kbf) * decay)
        rhss.append(jnp.concatenate([v * beta, kb * egcs[h]], axis=-1).astype(BF16))
        qs.append(q)
        ks.append(k)

    invs = _inv_unit_lower(lows, xor_ij, c)
    for h in heads:
        sol = _dot(invs[h].astype(BF16), rhss[h])
        cols = slice(h * hd, (h + 1) * hd)
        u_ref[0, :, cols] = sol[:, :hd]
        w_ref[0, :, cols] = sol[:, hd:].astype(BF16)
        qd_ref[0, :, cols] = (qs[h] * egcs[h]).astype(BF16)
        kd_ref[0, :, cols] = (ks[h] * jnp.exp(gends[h] - gcols[h])).astype(BF16)
        packed = intras[h][:, :c]
        for n in range(1, ts // c):
            packed = packed + intras[h][:, n * c:(n + 1) * c]
        ic_ref[0, :, h * c:(h + 1) * c] = packed.astype(BF16)
        for n in range(ts // c):
            gl_ref[0, n, h:h + 1, :] = jnp.broadcast_to(jnp.exp(gends[h][n * c:n * c + 1]), (1, hd))


def gdn_prep(o1, bg, bgt, conv_w):
    bsz, s, _ = o1.shape
    ts = min(GDN_TILE, s)
    c = GDN_CHUNK
    nc = ts // c
    qkv_w = 3 * GDN_WIDTH
    r = np.arange(ts)
    same = (r[:, None] // c) == (r[None, :] // c)
    ltri = jnp.asarray((same & (r[:, None] >= r[None, :])).astype(np.float32))
    utri = jnp.asarray((same & (r[:, None] <= r[None, :])).astype(np.float32))
    same = jnp.asarray(same.astype(np.float32))
    hb = ts // HALO
    tok = lambda b, i: (b, i, 0)
    const = lambda b, i: (0, 0)
    wide = jax.ShapeDtypeStruct((bsz, s, GDN_WIDTH), BF16)
    return pl.pallas_call(
        _gdn_prep_kernel,
        grid=(bsz, s // ts),
        in_specs=[pl.BlockSpec((1, ts, qkv_w), tok),
                  pl.BlockSpec((1, HALO, qkv_w), lambda b, i: (b, jnp.maximum(i * hb - 1, 0), 0)),
                  pl.BlockSpec((GDN_CONV, qkv_w), const),
                  pl.BlockSpec((1, ts, LANES), tok),
                  pl.BlockSpec((1, 8, ts), lambda b, i: (b, 0, i)),
                  pl.BlockSpec((ts, ts), const),
                  pl.BlockSpec((ts, ts), const),
                  pl.BlockSpec((ts, ts), const)],
        out_specs=[pl.BlockSpec((1, ts, GDN_WIDTH), tok),
                   pl.BlockSpec((1, ts, GDN_WIDTH), tok),
                   pl.BlockSpec((1, ts, GDN_WIDTH), tok),
                   pl.BlockSpec((1, ts, GDN_WIDTH), tok),
                   pl.BlockSpec((1, ts, GDN_HEADS * c), tok),
                   pl.BlockSpec((1, nc, GDN_HEADS, GDN_HEAD_DIM), lambda b, i: (b, i, 0, 0))],
        out_shape=[wide, wide, jax.ShapeDtypeStruct((bsz, s, GDN_WIDTH), F32), wide,
                   jax.ShapeDtypeStruct((bsz, s, GDN_HEADS * c), BF16),
                   jax.ShapeDtypeStruct((bsz, s // c, GDN_HEADS, GDN_HEAD_DIM), F32)],
        compiler_params=_cparams("parallel", "parallel"),
        name="gdn_prep",
    )(o1, o1, conv_w.astype(F32), bg, bgt, ltri, utri, same)


def _gdn_scan_kernel(qd_ref, kd_ref, u_ref, w_ref, ic_ref, gl_ref, o_ref, state_ref):
    ts = qd_ref.shape[1]
    c = GDN_CHUNK
    hd = GDN_HEAD_DIM

    @pl.when(pl.program_id(1) == 0)
    def _():
        state_ref[...] = jnp.zeros_like(state_ref)

    heads = range(GDN_HEADS)
    cols = [slice(h * hd, (h + 1) * hd) for h in heads]
    sts = [state_ref[h] for h in heads]
    for n in range(ts // c):
        rows = slice(n * c, (n + 1) * c)
        stbs = [st.astype(BF16) for st in sts]
        vbs = [(u_ref[0, rows, cols[h]] - _dot(w_ref[0, rows, cols[h]], stbs[h])).astype(BF16) for h in heads]
        for h in heads:
            o_ref[0, rows, cols[h]] = (_dot(qd_ref[0, rows, cols[h]], stbs[h])
                                       + _dot(ic_ref[0, rows, h * c:(h + 1) * c], vbs[h]))
        sts = [sts[h] * gl_ref[0, n, h:h + 1, :] + _dot_tn(kd_ref[0, rows, cols[h]], vbs[h]) for h in heads]
    for h in heads:
        state_ref[h] = sts[h]


def gdn_scan(qd, kd, u, w, ic, gl):
    bsz, s, _ = qd.shape
    ts = min(GDN_TILE, s)
    c = GDN_CHUNK
    tok = lambda b, i: (b, i, 0)
    return pl.pallas_call(
        _gdn_scan_kernel,
        grid=(bsz, s // ts),
        in_specs=[pl.BlockSpec((1, ts, GDN_WIDTH), tok)] * 4 + [
            pl.BlockSpec((1, ts, GDN_HEADS * c), tok),
            pl.BlockSpec((1, ts // c, GDN_HEADS, GDN_HEAD_DIM), lambda b, i: (b, i, 0, 0))],
        out_specs=pl.BlockSpec((1, ts, GDN_WIDTH), tok),
        out_shape=jax.ShapeDtypeStruct((bsz, s, GDN_WIDTH), F32),
        scratch_shapes=[pltpu.VMEM((GDN_HEADS, GDN_HEAD_DIM, GDN_HEAD_DIM), F32)],
        compiler_params=_cparams("parallel", "arbitrary"),
        name="gdn_scan",
    )(qd, kd, u, w, ic, gl)


def _even_post_kernel(o_ref, z_ref, scb_ref, scc_ref, sch_ref, scc_halo_ref, sch_halo_ref,
                      nw_ref, cw_ref, y_ref):
    hd = GDN_HEAD_DIM
    i = pl.program_id(1)
    o = o_ref[0]
    z = z_ref[0].astype(F32)
    nw = nw_ref[...]
    for h in range(GDN_HEADS):
        cols = slice(h * hd, (h + 1) * hd)
        oh = o[:, cols]
        on = oh * lax.rsqrt(jnp.mean(oh * oh, -1, keepdims=True) + NORM_EPS) * nw
        y_ref[0, :, cols] = (on * _silu(z[:, cols])).astype(BF16)
    ch = scc_ref[0].astype(F32) * sch_ref[0].astype(F32)
    ch_halo = jnp.where(i > 0, scc_halo_ref[0].astype(F32) * sch_halo_ref[0].astype(F32), 0.0)
    yb = scb_ref[0].astype(F32) * _causal_conv(ch, ch_halo, cw_ref[...], SC_CONV)
    y_ref[0, :, GDN_WIDTH:] = yb.astype(BF16)


def even_post(o, o1, norm_w, sc_conv_w):
    bsz, s, _ = o.shape
    tm = min(TOKEN_TILE, s)
    hb = tm // HALO
    wb = 512
    blk = lambda j: pl.BlockSpec((1, tm, wb), lambda b, i, j=j: (b, i, j))
    halo = lambda j: pl.BlockSpec((1, HALO, wb), lambda b, i, j=j: (b, jnp.maximum(i * hb - 1, 0), j))
    const = lambda b, i: (0, 0)
    return pl.pallas_call(
        _even_post_kernel,
        grid=(bsz, s // tm),
        in_specs=[pl.BlockSpec((1, tm, GDN_WIDTH), lambda b, i: (b, i, 0)),
                  blk(3), blk(4), blk(5), blk(6), halo(5), halo(6),
                  pl.BlockSpec((1, GDN_HEAD_DIM), const),
                  pl.BlockSpec((SC_CONV, SC_WIDTH), const)],
        out_specs=pl.BlockSpec((1, tm, GDN_WIDTH + SC_WIDTH), lambda b, i: (b, i, 0)),
        out_shape=jax.ShapeDtypeStruct((bsz, s, GDN_WIDTH + SC_WIDTH), BF16),
        compiler_params=_cparams("parallel", "parallel"),
        name="even_post",
    )(o, o1, o1, o1, o1, o1, o1, norm_w.astype(F32).reshape(1, -1), sc_conv_w.astype(F32))


def _layer_norm(r, g, b):
    mu = jnp.mean(r, -1, keepdims=True)
    rc = r - mu
    var = jnp.mean(rc * rc, -1, keepdims=True)
    return rc * lax.rsqrt(var + LN_EPS) * g + b


def _route(logits_t, bias):
    scores = _sigmoid(logits_t)
    biased = scores + bias
    t = logits_t.shape[1]
    epg = EXPERTS_PER_GROUP
    sub = lax.broadcasted_iota(jnp.int32, (epg, t), 0).astype(F32)
    best = None
    for g in range(N_GROUPS):
        bgp = biased[g * epg:(g + 1) * epg]
        m1 = jnp.max(bgp, axis=0, keepdims=True)
        i1 = jnp.min(jnp.where(bgp == m1, sub, float(epg)), axis=0, keepdims=True)
        rest = jnp.where(sub == i1, -jnp.inf, bgp)
        m2 = jnp.max(rest, axis=0, keepdims=True)
        i2 = jnp.min(jnp.where(rest == m2, sub, float(epg)), axis=0, keepdims=True)
        gs = m1 + m2
        if best is None:
            best, e0, e1 = gs, i1, i2
        else:
            better = gs > best
            best = jnp.where(better, gs, best)
            e0 = jnp.where(better, float(g * epg) + i1, e0)
            e1 = jnp.where(better, float(g * epg) + i2, e1)
    eio = lax.broadcasted_iota(jnp.int32, scores.shape, 0).astype(F32)
    hit0 = eio == e0
    hit1 = eio == e1
    s0 = jnp.sum(jnp.where(hit0, scores, 0.0), axis=0, keepdims=True)
    s1 = jnp.sum(jnp.where(hit1, scores, 0.0), axis=0, keepdims=True)
    tot = s0 + s1
    return e0.astype(jnp.int32), e1.astype(jnp.int32), s0 / tot, s1 / tot, hit0, hit1


def _mix_out_kernel(y_ref, wo_ref, x_ref, moda_ref, modb_ref, lng_ref, lnb_ref, rwt_ref, rb_ref,
                    upper_ref, x1_ref, hm_ref, seli_ref, selw_ref, cnt_ref):
    d = D_MODEL
    gate = moda_ref[0, 0][:, 2 * d:]
    modb = modb_ref[0, 0]
    y = _dot(y_ref[0], wo_ref[...])
    x1 = _layer_norm(DN_ALPHA * x_ref[0] + (1.0 + gate) * y, lng_ref[...], lnb_ref[...])
    x1_ref[0] = x1
    hm = x1 * (1.0 + modb[:, d:2 * d]) + modb[:, :d]
    hm_ref[0] = hm
    hm_hi = hm.astype(BF16)

    hm_lo = (hm - hm_hi.astype(F32)).astype(BF16)
    rw = rwt_ref[...]
    hi_lo = _dot_nt(rw, hm_hi)
    logits_t = hi_lo[:N_EXPERTS] + hi_lo[N_EXPERTS:] + _dot_nt(rw[:N_EXPERTS], hm_lo)
    e0, e1, w0, w1, hit0, hit1 = _route(logits_t, rb_ref[...])
    member = jnp.where(hit0 | hit1, 1.0, 0.0)
    before = _dot(member.astype(BF16), upper_ref[...])
    rank0 = jnp.sum(jnp.where(hit0, before, 0.0), axis=0, keepdims=True).astype(jnp.int32)
    rank1 = jnp.sum(jnp.where(hit1, before, 0.0), axis=0, keepdims=True).astype(jnp.int32)
    zi = jnp.zeros_like(e0)
    seli_ref[0] = jnp.concatenate([e0, e1, rank0, rank1, zi, zi, zi, zi], axis=0)
    zf = jnp.zeros_like(w0)
    selw_ref[0] = jnp.concatenate([w0, w1, zf, zf, zf, zf, zf, zf], axis=0)
    cnt_ref[0, 0] = jnp.broadcast_to(jnp.sum(member, axis=1, keepdims=True), (N_EXPERTS, LANES))


def mix_out(ymix, w_out, x, moda, modb, ln_g, ln_b, router_w, router_bias, b0=0):
    bsz, s, d = ymix.shape
    tm = min(TOKEN_TILE, s)
    nt = s // tm
    r = np.arange(tm)
    upper = jnp.asarray((r[:, None] < r[None, :]).astype(np.float32), dtype=BF16)
    rw_f32 = router_w.T.astype(F32)
    rw_hi = rw_f32.astype(BF16)
    rwt = jnp.concatenate([rw_hi, (rw_f32 - rw_hi.astype(F32)).astype(BF16)], axis=0)
    tok = lambda b, i: (b, i, 0)
    const = lambda b, i: (0, 0)
    modspec = pl.BlockSpec((1, 1, 1, 3 * d), lambda b, i: (b, 0, 0, 0))
    row8 = pl.BlockSpec((1, 8, tm), lambda b, i: (b, 0, i))
    return pl.pallas_call(
        _mix_out_kernel,
        grid=(bsz, nt),
        in_specs=[pl.BlockSpec((1, tm, d), tok),
                  pl.BlockSpec((d, d), const),
                  pl.BlockSpec((1, tm, d), lambda b, i: (b + b0, i, 0)),
                  modspec, modspec,
                  pl.BlockSpec((1, d), const), pl.BlockSpec((1, d), const),
                  pl.BlockSpec((2 * N_EXPERTS, d), const),
                  pl.BlockSpec((N_EXPERTS, 1), const),
                  pl.BlockSpec((tm, tm), const)],
        out_specs=[pl.BlockSpec((1, tm, d), tok),
                   pl.BlockSpec((1, tm, d), tok),
                   row8, row8,
                   pl.BlockSpec((1, 1, N_EXPERTS, LANES), lambda b, i: (b, i, 0, 0))],
        out_shape=[jax.ShapeDtypeStruct((bsz, s, d), F32),
                   jax.ShapeDtypeStruct((bsz, s, d), F32),
                   jax.ShapeDtypeStruct((bsz, 8, s), jnp.int32),
                   jax.ShapeDtypeStruct((bsz, 8, s), F32),
                   jax.ShapeDtypeStruct((bsz, nt, N_EXPERTS, LANES), F32)],
        compiler_params=_cparams("parallel", "parallel"),
        name="mix_out",
    )(ymix, w_out.astype(BF16), x, moda, modb, ln_g.reshape(1, d), ln_b.reshape(1, d),
      rwt, router_bias.astype(F32).reshape(N_EXPERTS, 1), upper)


def _experts_kernel(be_ref, next_ref, nused_ref, xs_ref, wg_hbm, wu_hbm, wd_hbm, ys_ref,
                    wgf_ref, wuf_ref, wdf_ref, wgb_ref, wub_ref, wdb_ref, sem_ref, slot_ref, *, layer):
    i = pl.program_id(0)

    def weight_copies(e, slot):
        return [pltpu.make_async_copy(src.at[layer, e], dst.at[slot], sem_ref.at[n, slot])
                for n, (src, dst) in enumerate(((wg_hbm, wgf_ref), (wu_hbm, wuf_ref), (wd_hbm, wdf_ref)))]

    @pl.when(i == 0)
    def _():
        slot_ref[0] = 0
        for cp in weight_copies(be_ref[0], 0):
            cp.start()

    @pl.when((i == 0) | (be_ref[i] != be_ref[jnp.maximum(i - 1, 0)]))
    def _():
        slot = slot_ref[0]
        for cp in weight_copies(be_ref[i], slot):
            cp.wait()

        @pl.when(next_ref[i] >= 0)
        def _():
            for cp in weight_copies(next_ref[i], 1 - slot):
                cp.start()

        wgb_ref[...] = wgf_ref[slot].astype(BF16)
        wub_ref[...] = wuf_ref[slot].astype(BF16)
        wdb_ref[...] = wdf_ref[slot].astype(BF16)
        slot_ref[0] = 1 - slot

    @pl.when(i < nused_ref[0])
    def _():
        x = xs_ref[...].astype(BF16)
        act =(_silu(_dot(x, wgb_ref[...])) * _dot(x, wub_ref[...])).astype(BF16)
        ys_ref[...] = _dot(act, wdb_ref[...]).astype(BF16)

    @pl.when(i >= nused_ref[0])
    def _():
        ys_ref[...] = jnp.zeros_like(ys_ref)


def moe_experts(xs, block_expert, n_used, w_gate, w_up, w_down, layer):
    rows, d = xs.shape
    nb = rows // MOE_BLOCK
    later = jnp.where(block_expert[None, :] > block_expert[:, None], block_expert[None, :], N_EXPERTS)
    next_expert = jnp.min(later, axis=1)
    next_expert = jnp.where(next_expert == N_EXPERTS, -1, next_expert).astype(jnp.int32)
    hbm = pl.BlockSpec(memory_space=pl.ANY)
    rows_spec = pl.BlockSpec((MOE_BLOCK, d), lambda i, be, nx, nu: (i, 0))
    return pl.pallas_call(
        functools.partial(_experts_kernel, layer=layer),
        grid_spec=pltpu.PrefetchScalarGridSpec(
            num_scalar_prefetch=3,
            grid=(nb,),
            in_specs=[rows_spec, hbm, hbm, hbm],
            out_specs=rows_spec,
            scratch_shapes=[pltpu.VMEM((2, d, D_EXPERT), F32), pltpu.VMEM((2, d, D_EXPERT), F32),
                            pltpu.VMEM((2, D_EXPERT, d), F32),
                            pltpu.VMEM((d, D_EXPERT), BF16), pltpu.VMEM((d, D_EXPERT), BF16),
                            pltpu.VMEM((D_EXPERT, d), BF16),
                            pltpu.SemaphoreType.DMA((3, 2)), pltpu.SMEM((1,), jnp.int32)]),
        out_shape=jax.ShapeDtypeStruct((rows, d), BF16),
        compiler_params=_cparams("arbitrary"),
        name="moe_experts",
    )(block_expert, next_expert, n_used, xs, w_gate, w_up, w_down)


def _plan_kernel(seli_ref, base_ref, dest_ref):
    sel = seli_ref[0]
    base = base_ref[0, 0]
    eio = lax.broadcasted_iota(jnp.int32, (N_EXPERTS, sel.shape[1]), 0)
    out = []
    for kk in range(2):
        first = jnp.sum(jnp.where(eio == sel[kk:kk + 1], base, 0.0), axis=0, keepdims=True)
        out.append(first.astype(jnp.int32) + sel[2 + kk:3 + kk])
    zero = jnp.zeros_like(out[0])
    dest_ref[0] = jnp.concatenate(out + [zero] * 6, axis=0)


def dispatch_plan(seli, base):
    bsz, _, s = seli.shape
    nt = base.shape[1]
    tm = s // nt
    return pl.pallas_call(
        _plan_kernel,
        grid=(bsz, nt),
        in_specs=[pl.BlockSpec((1, 8, tm), lambda b, i: (b, 0, i)),
                  pl.BlockSpec((1, 1, N_EXPERTS, 1), lambda b, i: (b, i, 0, 0))],
        out_specs=pl.BlockSpec((1, 8, tm), lambda b, i: (b, 0, i)),
        out_shape=jax.ShapeDtypeStruct((bsz, 8, s), jnp.int32),
        compiler_params=_cparams("parallel", "parallel"),
        name="dispatch_plan",
    )(seli, base)


def _invert_kernel(dest_ref, pad_lo_ref, pad_hi_ref, slot_tok_ref):
    t = dest_ref.shape[0] // 2

    def clear_range(e, carry):
        hi = pad_hi_ref[e]
        n = (hi - pad_lo_ref[e] + CLEAR_CHUNK - 1) // CLEAR_CHUNK

        def clear(q, c):
            r0 = hi - (q + 1) * CLEAR_CHUNK
            for j in range(CLEAR_CHUNK):
                slot_tok_ref[r0 + j] = lax.rem(r0 + j, t)
            return c
        return lax.fori_loop(0, n, clear, carry)

    def put(a, carry):
        slot_tok_ref[dest_ref[a]] = a
        slot_tok_ref[dest_ref[t + a]] = a
        return carry

    lax.fori_loop(0, pad_lo_ref.shape[0], clear_range, 0)
    lax.fori_loop(0, t, put, 0, unroll=8)


def invert_slots(dest_flat, pad_lo, pad_hi, rows):
    smem = pl.BlockSpec(memory_space=pltpu.SMEM)
    return pl.pallas_call(
        _invert_kernel,
        in_specs=[smem, smem, smem],
        out_specs=smem,
        out_shape=jax.ShapeDtypeStruct((rows,), jnp.int32),
        name="invert_slots",
    )(dest_flat, pad_lo, pad_hi)


def _moe_combine_kernel(g0_ref, g1_ref, wt_ref, x_ref, mod_ref, lng_ref, lnb_ref, o_ref):
    d = D_MODEL
    gate = mod_ref[0, 0][:, 2 * d:]
    wt = wt_ref[0]
    y = g0_ref[0, 0].astype(F32) * wt[:, 0:1] + g1_ref[0, 0].astype(F32) * wt[:, 1:2]
    o_ref[0] = _layer_norm(DN_ALPHA * x_ref[0] + (1.0 + gate) * y, lng_ref[...], lnb_ref[...])


def moe_combine(g, wt, x1, modb, ln_g, ln_b):
    bsz, s, d = x1.shape
    tm = min(TOKEN_TILE, s)
    tok = lambda b, i: (b, i, 0)
    const = lambda b, i: (0, 0)
    return pl.pallas_call(
        _moe_combine_kernel,
        grid=(bsz, s // tm),
        in_specs=[pl.BlockSpec((1, 1, tm, d), lambda b, i: (0, b, i, 0)),
                  pl.BlockSpec((1, 1, tm, d), lambda b, i: (1, b, i, 0)),
                  pl.BlockSpec((1, tm, 8), tok), pl.BlockSpec((1, tm, d), tok),
                  pl.BlockSpec((1, 1, 1, 3 * d), lambda b, i: (b, 0, 0, 0)),
                  pl.BlockSpec((1, d), const), pl.BlockSpec((1, d), const)],
        out_specs=pl.BlockSpec((1, tm, d), tok),
        out_shape=jax.ShapeDtypeStruct((bsz, s, d), F32),
        compiler_params=_cparams("parallel", "parallel"),
        name="moe_combine",
    )(g, g, wt, x1, modb, ln_g.reshape(1, d), ln_b.reshape(1, d))


def moe_layer(x1, hm, seli, selw, cnt, modb, ln_g, ln_b, w_gate, w_up, w_down, layer):
    bsz, s, d = x1.shape
    t = bsz * s
    tm = min(TOKEN_TILE, s)
    blk = MOE_BLOCK
    cnt = cnt[..., 0].reshape(-1, N_EXPERTS).astype(jnp.int32)
    tile_off = jnp.cumsum(cnt, axis=0) - cnt
    counts = cnt.sum(0)
    padded = (counts + blk - 1) // blk * blk
    pend = jnp.cumsum(padded)
    pstart = pend - padded
    nb = -(-(2 * t) // blk) + N_EXPERTS
    rows = nb * blk
    base = (pstart[None, :] + tile_off).astype(F32).reshape(bsz, s // tm, N_EXPERTS, 1)
    dest = dispatch_plan(seli, base)
    dest = jnp.swapaxes(dest[:, 0:2, :], 0, 1).reshape(2 * t)
    pad_lo = jnp.concatenate([pstart + counts, pend[-1:]]).astype(jnp.int32)
    pad_hi = jnp.concatenate([pend, jnp.full((1,), rows)]).astype(jnp.int32)
    slot_tok = invert_slots(dest, pad_lo, pad_hi, rows)
    starts = jnp.arange(nb, dtype=jnp.int32) * blk
    block_expert = jnp.minimum(jnp.sum((pend[None, :] <= starts[:, None]).astype(jnp.int32), axis=1),
                               N_EXPERTS - 1)
    n_used = (pend[-1] // blk).astype(jnp.int32).reshape(1)
    xs = hm.reshape(t, d).at[slot_tok].get(mode="promise_in_bounds")
    ys = moe_experts(xs, block_expert, n_used, w_gate, w_up, w_down, layer)
    g = ys.at[dest].get(mode="promise_in_bounds").reshape(2, bsz, s, d)
    wt = jnp.swapaxes(selw, 1, 2)
    return moe_combine(g, wt, x1, modb, ln_g, ln_b)


MLA_IN_COLS = Q_LORA + KV_LORA + 2 * LANES
MLA_QK = 2 * LANES


def _mla_in_kernel(x_ref, mod_ref, pos_ref, invf_ref, win_ref, qn_ref, kvn_ref, wq_ref, wkv_ref,
                   q_ref, k_ref, v_ref):
    d = D_MODEL
    nh = MLA_HEADS
    mod = mod_ref[0, 0]
    h = (x_ref[0] * (1.0 + mod[:, d:2 * d]) + mod[:, :d]).astype(BF16)
    proj = _dot(h, win_ref[...])
    ang = pos_ref[0].astype(F32) * invf_ref[...]
    lane = lax.broadcasted_iota(jnp.int32, ang.shape, 1)
    cos = jnp.where(lane < QK_ROPE, jnp.cos(ang), 0.0)
    sin = jnp.where(lane < QK_ROPE, jnp.sin(ang), 0.0)

    qa = proj[:, :Q_LORA]
    qa = (qa * lax.rsqrt(jnp.mean(qa * qa, -1, keepdims=True) + NORM_EPS) * qn_ref[...]).astype(BF16)
    kva = proj[:, Q_LORA:Q_LORA + KV_LORA]
    kva = (kva * lax.rsqrt(jnp.mean(kva * kva, -1, keepdims=True) + NORM_EPS) * kvn_ref[...]).astype(BF16)
    kr0 = Q_LORA + KV_LORA
    k_rope = (proj[:, kr0:kr0 + LANES] * cos + proj[:, kr0 + LANES:kr0 + 2 * LANES] * sin).astype(BF16)

    scale = (QK_NOPE + QK_ROPE) ** -0.5 * LOG2_E
    hw = nh * LANES
    q_nope = _dot(qa, wq_ref[:, :hw])
    q_rope = _dot(qa, wq_ref[:, hw:2 * hw])
    q_rot = _dot(qa, wq_ref[:, 2 * hw:])
    k_nope = _dot(kva, wkv_ref[:, :hw])
    v_ref[0] = _dot(kva, wkv_ref[:, hw:]).astype(BF16)
    for hh in range(nh):
        cols = slice(hh * LANES, (hh + 1) * LANES)
        q_ref[0, :, hh * MLA_QK:hh * MLA_QK + LANES] = (q_nope[:, cols] * scale).astype(BF16)
        q_ref[0, :, hh * MLA_QK + LANES:(hh + 1) * MLA_QK] = (
            (q_rope[:, cols] * cos + q_rot[:, cols] * sin) * scale).astype(BF16)
        k_ref[0, :, hh * MLA_QK:hh * MLA_QK + LANES] = k_nope[:, cols].astype(BF16)
        k_ref[0, :, hh * MLA_QK + LANES:(hh + 1) * MLA_QK] = k_rope


def _rope_cols(w):
    half = QK_ROPE // 2
    pad = [(0, 0)] * (w.ndim - 1) + [(0, LANES - QK_ROPE)]
    rot = jnp.concatenate([-w[..., half:], w[..., :half]], axis=-1)
    return jnp.pad(w, pad), jnp.pad(rot, pad)


def mla_in(x, mod, positions, w_in, q_a_norm, w_q_b, kv_a_norm, w_kv_b):
    bsz, s, d = x.shape
    tm = min(TOKEN_TILE, s)
    nh = MLA_HEADS
    kr, kr_rot = _rope_cols(w_in[:, Q_LORA + KV_LORA:])
    win = jnp.concatenate([w_in[:, :Q_LORA + KV_LORA], kr, kr_rot], axis=1).astype(BF16)
    wq = w_q_b.reshape(Q_LORA, nh, QK_NOPE + QK_ROPE)
    qr, qr_rot = _rope_cols(wq[..., QK_NOPE:])
    wq = jnp.concatenate([wq[..., :QK_NOPE].reshape(Q_LORA, -1), qr.reshape(Q_LORA, -1),
                          qr_rot.reshape(Q_LORA, -1)], axis=1).astype(BF16)
    wkv = w_kv_b.reshape(KV_LORA, nh, QK_NOPE + V_DIM)
    wkv = jnp.concatenate([wkv[..., :QK_NOPE].reshape(KV_LORA, -1),
                           wkv[..., QK_NOPE:].reshape(KV_LORA, -1)], axis=1).astype(BF16)
    inv = ROPE_THETA ** (-np.arange(0, QK_ROPE, 2, dtype=np.float32) / QK_ROPE)
    invf = np.zeros((1, LANES), np.float32)
    invf[0, :QK_ROPE] = np.concatenate([inv, inv])
    tok = lambda b, i: (b, i, 0)
    const = lambda b, i: (0, 0)
    return pl.pallas_call(
        _mla_in_kernel,
        grid=(bsz, s // tm),
        in_specs=[pl.BlockSpec((1, tm, d), tok),
                  pl.BlockSpec((1, 1, 1, 3 * d), lambda b, i: (b, 0, 0, 0)),
                  pl.BlockSpec((1, tm, 1), tok),
                  pl.BlockSpec((1, LANES), const),
                  pl.BlockSpec((d, MLA_IN_COLS), const),
                  pl.BlockSpec((1, Q_LORA), const),
                  pl.BlockSpec((1, KV_LORA), const),
                  pl.BlockSpec((Q_LORA, 3 * nh * LANES), const),
                  pl.BlockSpec((KV_LORA, 2 * nh * LANES), const)],
        out_specs=[pl.BlockSpec((1, tm, nh * MLA_QK), tok),
                   pl.BlockSpec((1, tm, nh * MLA_QK), tok),
                   pl.BlockSpec((1, tm, nh * V_DIM), tok)],
        out_shape=[jax.ShapeDtypeStruct((bsz, s, nh * MLA_QK), BF16),
                   jax.ShapeDtypeStruct((bsz, s, nh * MLA_QK), BF16),
                   jax.ShapeDtypeStruct((bsz, s, nh * V_DIM), BF16)],
        compiler_params=_cparams("parallel", "parallel"),
        name="mla_in",
    )(x, mod, positions.reshape(bsz, s, 1), jnp.asarray(invf), win,
      q_a_norm.astype(F32).reshape(1, -1), kv_a_norm.astype(F32).reshape(1, -1), wq, wkv)


def _attn_kernel(q_ref, k_ref, v_ref, o_ref, m_ref, l_ref, acc_ref):
    tq = q_ref.shape[1]
    i = pl.program_id(2)
    m_ref[...] = jnp.full(m_ref.shape, -jnp.inf, F32)
    l_ref[...] = jnp.zeros(l_ref.shape, F32)
    acc_ref[...] = jnp.zeros(acc_ref.shape, F32)

    groups = [slice(g * ATTN_ROWS, (g + 1) * ATTN_ROWS) for g in range(tq // ATTN_ROWS)]

    def update(off, widths, masked):
        scs = []
        for rows, width in zip(groups, widths):
            sc = _dot_nt(q_ref[0, rows, :], k_ref[0, pl.ds(off, width), :])
            if masked:
                qi = lax.broadcasted_iota(jnp.int32, sc.shape, 0) + rows.start
                ki = lax.broadcasted_iota(jnp.int32, sc.shape, 1)
                sc = jnp.where(ki <= qi, sc, -jnp.inf)
            scs.append(sc)
        m_olds = [m_ref[rows] for rows in groups]
        m_news = [jnp.maximum(m_old, jnp.max(sc, -1, keepdims=True)) for m_old, sc in zip(m_olds, scs)]
        ps = [jnp.exp2(sc - jnp.concatenate([m_new] * (width // LANES), axis=1))
              for sc, m_new, width in zip(scs, m_news, widths)]
        alphas = [jnp.exp2(m_old - m_new) for m_old, m_new in zip(m_olds, m_news)]
        pvs = [_dot(p.astype(BF16), v_ref[0, pl.ds(off, width), :]) for p, width in zip(ps, widths)]
        for rows, m_new, alpha, p, pv in zip(groups, m_news, alphas, ps, pvs):
            l_ref[rows] = alpha * l_ref[rows] + jnp.sum(p, -1, keepdims=True)
            acc_ref[rows] = alpha * acc_ref[rows] + pv
            m_ref[rows] = m_new

    def full_block(j, carry):
        update(pl.multiple_of(j * tq, tq), [tq] * len(groups), False)
        return carry

    lax.fori_loop(0, i, full_block, 0)
    update(pl.multiple_of(i * tq, tq), [rows.stop for rows in groups], True)
    o_ref[0] = (acc_ref[...] / l_ref[...]).astype(BF16)


def attention(q, k, v):
    bsz, s, _ = q.shape
    tq = min(ATTN_TILE, s)
    return pl.pallas_call(
        _attn_kernel,
        grid=(bsz, MLA_HEADS, s // tq),
        in_specs=[pl.BlockSpec((1, tq, MLA_QK), lambda b, h, i: (b, i, h)),
                  pl.BlockSpec((1, s, MLA_QK), lambda b, h, i: (b, 0, h)),
                  pl.BlockSpec((1, s, V_DIM), lambda b, h, i: (b, 0, h))],
        out_specs=pl.BlockSpec((1, tq, V_DIM), lambda b, h, i: (b, i, h)),
        out_shape=jax.ShapeDtypeStruct((bsz, s, MLA_HEADS * V_DIM), BF16),
        scratch_shapes=[pltpu.VMEM((tq, LANES), F32), pltpu.VMEM((tq, LANES), F32), pltpu.VMEM((tq, V_DIM), F32)],
        compiler_params=_cparams("parallel", "parallel", "arbitrary"),
        name="attention",
    )(q, k, v)


def kernel(x, c, positions, ada_w, ada_b, ln_g, ln_b, w_in_e, gdn_conv_w, gdn_a_log, gdn_dt_bias,
           gdn_norm_w, sc_conv_w, w_out_e, w_in_o, q_a_norm, w_q_b, kv_a_norm, w_kv_b, w_out_o,
           router_w, router_bias, w_gate, w_up, w_down):
    bsz, s, d = x.shape
    depth = ada_w.shape[0]
    mod = ada_mod(c, ada_w, ada_b).reshape(depth, 2, bsz, 1, 3 * d)

    def layers(x, b0, mod, positions):
        for i in range(depth):
            j = i // 2
            moda, modb = mod[i, 0][:, None], mod[i, 1][:, None]
            if i % 2 == 0:
                o1, bg, bgt = even_in_proj(x, moda, w_in_e[j], gdn_a_log[j], gdn_dt_bias[j], b0)
                qd, kd, u, w, ic, gl = gdn_prep(o1, bg, bgt, gdn_conv_w[j])
                o = gdn_scan(qd, kd, u, w, ic, gl)
                ymix = even_post(o, o1, gdn_norm_w[j], sc_conv_w[j])
                w_out = w_out_e[j]
            else:
                q, k, v = mla_in(x, moda, positions, w_in_o[j], q_a_norm[j], w_q_b[j], kv_a_norm[j],
                                 w_kv_b[j])
                ymix = attention(q, k, v)
                w_out = w_out_o[j]
            x1, hm, seli, selw, cnt = mix_out(ymix, w_out, x, moda, modb, ln_g[i, 0], ln_b[i, 0],
                                              router_w, router_bias, b0)
            x = moe_layer(x1, hm, seli, selw, cnt, modb, ln_g[i, 1], ln_b[i, 1], w_gate, w_up, w_down, i)
            b0 = 0
        return x

    ng = BATCH_GROUPS if bsz % BATCH_GROUPS == 0 else 1
    gb = bsz // ng
    outs = [layers(x, g * gb, mod[:, :, g * gb:(g + 1) * gb], positions[g * gb:(g + 1) * gb])
            for g in range(ng)]
    return outs[0] if ng == 1 else jnp.concatenate(outs, axis=0)
```

```python
import functools

import numpy as np
import jax
import jax.numpy as jnp
from jax import lax
from jax.experimental import pallas as pl
from jax.experimental.pallas import tpu as pltpu

F32 = jnp.float32
BF16 = jnp.bfloat16
HIGHEST = lax.Precision.HIGHEST

D_MODEL = 1024
DEPTH = 2
DN_ALPHA = (2.0 * DEPTH) ** 0.25

GDN_HEADS = 4
GDN_HEAD_DIM = 128
GDN_WIDTH = GDN_HEADS * GDN_HEAD_DIM
GDN_CONV = 4
GDN_CHUNK = 64
SC_WIDTH = 512
SC_CONV = 3

MLA_HEADS = 8
Q_LORA = 384
KV_LORA = 256
QK_NOPE = 128
QK_ROPE = 64
V_DIM = 128
ROPE_THETA = 10000.0

N_EXPERTS = 32
N_GROUPS = 4
EXPERTS_PER_GROUP = N_EXPERTS // N_GROUPS
D_EXPERT = 512

NORM_EPS = 1e-6
LN_EPS = 1e-5

LANES = 128
HALO = 16
TOKEN_TILE = 512
GDN_TILE = 256
ATTN_TILE = 1024
ATTN_ROWS = 256
LOG2_E = 1.4426950408889634
MOE_BLOCK = 512
SCAN_BATCH = 4
BATCH_GROUPS = 1
CLEAR_CHUNK = 8
VMEM_LIMIT = 48 * 1024 * 1024


def _cparams(*sem):
    return pltpu.CompilerParams(dimension_semantics=sem, vmem_limit_bytes=VMEM_LIMIT)


def _sigmoid(x):
    return 1.0 / (1.0 + jnp.exp(-x))


def _silu(x):
    return x * _sigmoid(x)


def _dot(a, b):
    return jnp.dot(a, b, preferred_element_type=F32)


def _dot_nt(a, b, precision=None):
    return lax.dot_general(a, b, (((1,), (1,)), ((), ())), precision=precision,
                           preferred_element_type=F32)


def _dot_tn(a, b):
    return lax.dot_general(a, b, (((0,), (0,)), ((), ())), preferred_element_type=F32)


def _ada_kernel(c_ref, w_ref, b_ref, o_ref):
    cond = _silu(c_ref[...])
    o_ref[0] = jnp.dot(cond, w_ref[0], precision=HIGHEST, preferred_element_type=F32) + b_ref[0]


def ada_mod(c, ada_w, ada_b):
    nl = ada_w.shape[0] * ada_w.shape[1]
    bsz, d = c.shape
    w = ada_w.reshape(nl, d, 3 * d)
    b = ada_b.reshape(nl, 1, 3 * d)
    return pl.pallas_call(
        _ada_kernel,
        grid=(nl, 3),
        in_specs=[pl.BlockSpec((bsz, d), lambda l, j: (0, 0)),
                  pl.BlockSpec((1, d, d), lambda l, j: (l, 0, j)),
                  pl.BlockSpec((1, 1, d), lambda l, j: (l, 0, j))],
        out_specs=pl.BlockSpec((1, bsz, d), lambda l, j: (l, 0, j)),
        out_shape=jax.ShapeDtypeStruct((nl, bsz, 3 * d), F32),
        compiler_params=_cparams("parallel", "parallel"),
        name="ada_mod",
    )(c, w, b)


EVEN_MAIN = 3 * GDN_WIDTH + GDN_WIDTH + 3 * SC_WIDTH


def _even_in_kernel(x_ref, mod_ref, w1_ref, w2_ref, alog_ref, dtb_ref, o1_ref, o2_ref, o2t_ref):
    d = D_MODEL
    mod = mod_ref[0, 0]
    h = (x_ref[0] * (1.0 + mod[:, d:2 * d]) + mod[:, :d]).astype(BF16)
    for j in range(EVEN_MAIN // 512):
        o1_ref[0, :, j * 512:(j + 1) * 512] = _dot(h, w1_ref[:, j * 512:(j + 1) * 512]).astype(BF16)
    r = _dot(h, w2_ref[...])
    lane = lax.broadcasted_iota(jnp.int32, r.shape, 1)
    a = r + dtb_ref[...]
    softplus = jnp.maximum(a, 0.0) + jnp.log(1.0 + jnp.exp(-jnp.abs(a)))
    bg = jnp.where(lane < GDN_HEADS, _sigmoid(r), -jnp.exp(alog_ref[...]) * softplus)
    o2_ref[0] = bg
    o2t_ref[0] = bg.T[:8]


def even_in_proj(x, mod, w_in, a_log, dt_bias, b0=0):
    _, s, d = x.shape
    bsz = mod.shape[0]
    tm = min(TOKEN_TILE, s)
    q_end = 4 * GDN_WIDTH
    w1 = jnp.concatenate([w_in[:, :q_end], w_in[:, q_end + 2 * GDN_HEADS:]], axis=1).astype(BF16)
    w2 = jnp.pad(w_in[:, q_end:q_end + 2 * GDN_HEADS], ((0, 0), (0, LANES - 2 * GDN_HEADS))).astype(BF16)
    alog = jnp.pad(a_log.astype(F32), (GDN_HEADS, LANES - 2 * GDN_HEADS)).reshape(1, LANES)
    dtb = jnp.pad(dt_bias.astype(F32), (GDN_HEADS, LANES - 2 * GDN_HEADS)).reshape(1, LANES)
    const = lambda b, i: (0, 0)
    return pl.pallas_call(
        _even_in_kernel,
        grid=(bsz, s // tm),
        in_specs=[pl.BlockSpec((1, tm, d), lambda b, i: (b + b0, i, 0)),
                  pl.BlockSpec((1, 1, 1, 3 * d), lambda b, i: (b, 0, 0, 0)),
                  pl.BlockSpec((d, EVEN_MAIN), const),
                  pl.BlockSpec((d, LANES), const),
                  pl.BlockSpec((1, LANES), const),
                  pl.BlockSpec((1, LANES), const)],
        out_specs=[pl.BlockSpec((1, tm, EVEN_MAIN), lambda b, i: (b, i, 0)),
                   pl.BlockSpec((1, tm, LANES), lambda b, i: (b, i, 0)),
                   pl.BlockSpec((1, 8, tm), lambda b, i: (b, 0, i))],
        out_shape=[jax.ShapeDtypeStruct((bsz, s, EVEN_MAIN), BF16),
                   jax.ShapeDtypeStruct((bsz, s, LANES), F32),
                   jax.ShapeDtypeStruct((bsz, 8, s), F32)],
        compiler_params=_cparams("parallel", "parallel"),
        name="even_in_proj",
    )(x, mod, w1, w2, alog, dtb)


def _causal_conv(x, halo, w, taps):
    rows = x.shape[0]
    xf = jnp.concatenate([halo, x], axis=0)
    y = w[taps - 1:taps] * x
    for j in range(taps - 1):
        off = HALO - (taps - 1) + j
        y = y + w[j:j + 1] * xf[off:off + rows]
    return y


def _inv_unit_lower(lows, xor_ij, block):
    eye = jnp.where(xor_ij == 0, 1.0, 0.0)
    ms = [eye - jnp.where(xor_ij == 1, low, 0.0) for low in lows]
    s = 2
    while s < block:
        level = (xor_ij >> (s.bit_length() - 1)) == 1
        mbs = [m.astype(BF16) for m in ms]
        cms = [_dot(jnp.where(level, low, 0.0).astype(BF16), mb).astype(BF16) for low, mb in zip(lows, mbs)]
        ms = [m - _dot(mb, cm) for m, mb, cm in zip(ms, mbs, cms)]
        s *= 2
    return ms


def _gdn_prep_kernel(qkv_ref, halo_ref, cw_ref, bg_ref, bgt_ref, ltri_ref, utri_ref, same_ref,
                     qd_ref, kd_ref, u_ref, w_ref, ic_ref, gl_ref):
    ts = qkv_ref.shape[1]
    c = GDN_CHUNK
    hd = GDN_HEAD_DIM
    i = pl.program_id(1)
    x = qkv_ref[0].astype(F32)
    halo = jnp.where(i > 0, halo_ref[0].astype(F32), 0.0)
    y = _silu(_causal_conv(x, halo, cw_ref[...], GDN_CONV))

    bg = bg_ref[0]
    gc_col = jnp.dot(ltri_ref[...], bg, precision=HIGHEST, preferred_element_type=F32)
    gc_row = jnp.dot(bgt_ref[0], utri_ref[...], precision=HIGHEST, preferred_element_type=F32)
    gc_end = jnp.dot(same_ref[...], bg, precision=HIGHEST, preferred_element_type=F32)

    ii = lax.broadcasted_iota(jnp.int32, (ts, ts), 0)
    jj = lax.broadcasted_iota(jnp.int32, (ts, ts), 1)
    xor_ij = ii ^ jj
    causal = (same_ref[...] > 0.0) & (ii >= jj)
    diag = xor_ij == 0

    heads = range(GDN_HEADS)
    gcols = [gc_col[:, GDN_HEADS + h:GDN_HEADS + h + 1] for h in heads]
    gends = [gc_end[:, GDN_HEADS + h:GDN_HEADS + h + 1] for h in heads]
    egcs = [jnp.exp(g) for g in gcols]
    lows, intras, rhss, qs, ks = [], [], [], [], []
    for h in heads:
        q = y[:, h * hd:(h + 1) * hd]
        k = y[:, GDN_WIDTH + h * hd:GDN_WIDTH + (h + 1) * hd]
        v = y[:, 2 * GDN_WIDTH + h * hd:2 * GDN_WIDTH + (h + 1) * hd]
        q = q * lax.rsqrt(jnp.sum(q * q, -1, keepdims=True) + NORM_EPS) * (hd ** -0.5)
        k = k * lax.rsqrt(jnp.sum(k * k, -1, keepdims=True) + NORM_EPS)
        beta = bg[:, h:h + 1]
        grow = gc_row[GDN_HEADS + h:GDN_HEADS + h + 1, :]
        decay = jnp.exp(jnp.where(causal, gcols[h] - grow, -jnp.inf))
        kb = k * beta
        kbf = k.astype(BF16)
        lows.append(jnp.where(diag, 0.0, _dot_nt(kb.astype(BF16), kbf) * decay))
        intras.append(_dot_nt(q.astype(BF16), kbf) * decay)
        rhss.append(jnp.concatenate([v * beta, kb * egcs[h]], axis=-1).astype(BF16))
        qs.append(q)
        ks.append(k)

    invs = _inv_unit_lower(lows, xor_ij, c)
    for h in heads:
        sol = _dot(invs[h].astype(BF16), rhss[h])
        cols = slice(h * hd, (h + 1) * hd)
        u_ref[0, :, cols] = sol[:, :hd]
        w_ref[0, :, cols] = sol[:, hd:].astype(BF16)
        qd_ref[0, :, cols] = (qs[h] * egcs[h]).astype(BF16)
        kd_ref[0, :, cols] = (ks[h] * jnp.exp(gends[h] - gcols[h])).astype(BF16)
        packed = intras[h][:, :c]
        for n in range(1, ts // c):
            packed = packed + intras[h][:, n * c:(n + 1) * c]
        ic_ref[0, :, h * c:(h + 1) * c] = packed.astype(BF16)
        for n in range(ts // c):
            gl_ref[0, n, h:h + 1, :] = jnp.broadcast_to(jnp.exp(gends[h][n * c:n * c + 1]), (1, hd))


def gdn_prep(o1, bg, bgt, conv_w):
    bsz, s, _ = o1.shape
    ts = min(GDN_TILE, s)
    c = GDN_CHUNK
    nc = ts // c
    qkv_w = 3 * GDN_WIDTH
    r = np.arange(ts)
    same = (r[:, None] // c) == (r[None, :] // c)
    ltri = jnp.asarray((same & (r[:, None] >= r[None, :])).astype(np.float32))
    utri = jnp.asarray((same & (r[:, None] <= r[None, :])).astype(np.float32))
    same = jnp.asarray(same.astype(np.float32))
    hb = ts // HALO
    tok = lambda b, i: (b, i, 0)
    const = lambda b, i: (0, 0)
    wide = jax.ShapeDtypeStruct((bsz, s, GDN_WIDTH), BF16)
    return pl.pallas_call(
        _gdn_prep_kernel,
        grid=(bsz, s // ts),
        in_specs=[pl.BlockSpec((1, ts, qkv_w), tok),
                  pl.BlockSpec((1, HALO, qkv_w), lambda b, i: (b, jnp.maximum(i * hb - 1, 0), 0)),
                  pl.BlockSpec((GDN_CONV, qkv_w), const),
                  pl.BlockSpec((1, ts, LANES), tok),
                  pl.BlockSpec((1, 8, ts), lambda b, i: (b, 0, i)),
                  pl.BlockSpec((ts, ts), const),
                  pl.BlockSpec((ts, ts), const),
                  pl.BlockSpec((ts, ts), const)],
        out_specs=[pl.BlockSpec((1, ts, GDN_WIDTH), tok),
                   pl.BlockSpec((1, ts, GDN_WIDTH), tok),
                   pl.BlockSpec((1, ts, GDN_WIDTH), tok),
                   pl.BlockSpec((1, ts, GDN_WIDTH), tok),
                   pl.BlockSpec((1, ts, GDN_HEADS * c), tok),
                   pl.BlockSpec((1, nc, GDN_HEADS, GDN_HEAD_DIM), lambda b, i: (b, i, 0, 0))],
        out_shape=[wide, wide, jax.ShapeDtypeStruct((bsz, s, GDN_WIDTH), F32), wide,
                   jax.ShapeDtypeStruct((bsz, s, GDN_HEADS * c), BF16),
                   jax.ShapeDtypeStruct((bsz, s // c, GDN_HEADS, GDN_HEAD_DIM), F32)],
        compiler_params=_cparams("parallel", "parallel"),
        name="gdn_prep",
    )(o1, o1, conv_w.astype(F32), bg, bgt, ltri, utri, same)


def _gdn_scan_kernel(qd_ref, kd_ref, u_ref, w_ref, ic_ref, gl_ref, o_ref, state_ref):
    nb, ts = qd_ref.shape[0], qd_ref.shape[1]
    c = GDN_CHUNK
    hd = GDN_HEAD_DIM

    @pl.when(pl.program_id(1) == 0)
    def _():
        state_ref[...] = jnp.zeros_like(state_ref)

    chains = [(b, h) for b in range(nb) for h in range(GDN_HEADS)]
    cols = [slice(h * hd, (h + 1) * hd) for _, h in chains]
    sts = [state_ref[b * GDN_HEADS + h] for b, h in chains]
    for n in range(ts // c):
        rows = slice(n * c, (n + 1) * c)
        stbs = [st.astype(BF16) for st in sts]
        vbs = [(u_ref[b, rows, cl] - _dot(w_ref[b, rows, cl], stb)).astype(BF16)
               for (b, _), cl, stb in zip(chains, cols, stbs)]
        for (b, h), cl, stb, vb in zip(chains, cols, stbs, vbs):
            o_ref[b, rows, cl] = _dot(qd_ref[b, rows, cl], stb) + _dot(ic_ref[b, rows, h * c:(h + 1) * c], vb)
        sts = [st * gl_ref[b, n, h:h + 1, :] + _dot_tn(kd_ref[b, rows, cl], vb)
               for (b, h), cl, st, vb in zip(chains, cols, sts, vbs)]
    for (b, h), st in zip(chains, sts):
        state_ref[b * GDN_HEADS + h] = st


def gdn_scan(qd, kd, u, w, ic, gl):
    bsz, s, _ = qd.shape
    ts = min(GDN_TILE, s)
    c = GDN_CHUNK
    nb = SCAN_BATCH if bsz % SCAN_BATCH == 0 else 1
    tok = lambda b, i: (b, i, 0)
    return pl.pallas_call(
        _gdn_scan_kernel,
        grid=(bsz // nb, s // ts),
        in_specs=[pl.BlockSpec((nb, ts, GDN_WIDTH), tok)] * 4 + [
            pl.BlockSpec((nb, ts, GDN_HEADS * c), tok),
            pl.BlockSpec((nb, ts // c, GDN_HEADS, GDN_HEAD_DIM), lambda b, i: (b, i, 0, 0))],
        out_specs=pl.BlockSpec((nb, ts, GDN_WIDTH), tok),
        out_shape=jax.ShapeDtypeStruct((bsz, s, GDN_WIDTH), F32),
        scratch_shapes=[pltpu.VMEM((nb * GDN_HEADS, GDN_HEAD_DIM, GDN_HEAD_DIM), F32)],
        compiler_params=_cparams("parallel", "arbitrary"),
        name="gdn_scan",
    )(qd, kd, u, w, ic, gl)


def _even_post_kernel(o_ref, z_ref, scb_ref, scc_ref, sch_ref, scc_halo_ref, sch_halo_ref,
                      nw_ref, cw_ref, y_ref):
    hd = GDN_HEAD_DIM
    i = pl.program_id(1)
    o = o_ref[0]
    z = z_ref[0].astype(F32)
    nw = nw_ref[...]
    for h in range(GDN_HEADS):
        cols = slice(h * hd, (h + 1) * hd)
        oh = o[:, cols]
        on = oh * lax.rsqrt(jnp.mean(oh * oh, -1, keepdims=True) + NORM_EPS) * nw
        y_ref[0, :, cols] = (on * _silu(z[:, cols])).astype(BF16)
    ch = scc_ref[0].astype(F32) * sch_ref[0].astype(F32)
    ch_halo = jnp.where(i > 0, scc_halo_ref[0].astype(F32) * sch_halo_ref[0].astype(F32), 0.0)
    yb = scb_ref[0].astype(F32) * _causal_conv(ch, ch_halo, cw_ref[...], SC_CONV)
    y_ref[0, :, GDN_WIDTH:] = yb.astype(BF16)


def even_post(o, o1, norm_w, sc_conv_w):
    bsz, s, _ = o.shape
    tm = min(TOKEN_TILE, s)
    hb = tm // HALO
    wb = 512
    blk = lambda j: pl.BlockSpec((1, tm, wb), lambda b, i, j=j: (b, i, j))
    halo = lambda j: pl.BlockSpec((1, HALO, wb), lambda b, i, j=j: (b, jnp.maximum(i * hb - 1, 0), j))
    const = lambda b, i: (0, 0)
    return pl.pallas_call(
        _even_post_kernel,
        grid=(bsz, s // tm),
        in_specs=[pl.BlockSpec((1, tm, GDN_WIDTH), lambda b, i: (b, i, 0)),
                  blk(3), blk(4), blk(5), blk(6), halo(5), halo(6),
                  pl.BlockSpec((1, GDN_HEAD_DIM), const),
                  pl.BlockSpec((SC_CONV, SC_WIDTH), const)],
        out_specs=pl.BlockSpec((1, tm, GDN_WIDTH + SC_WIDTH), lambda b, i: (b, i, 0)),
        out_shape=jax.ShapeDtypeStruct((bsz, s, GDN_WIDTH + SC_WIDTH), BF16),
        compiler_params=_cparams("parallel", "parallel"),
        name="even_post",
    )(o, o1, o1, o1, o1, o1, o1, norm_w.astype(F32).reshape(1, -1), sc_conv_w.astype(F32))


def _layer_norm(r, g, b):
    mu = jnp.mean(r, -1, keepdims=True)
    rc = r - mu
    var = jnp.mean(rc * rc, -1, keepdims=True)
    return rc * lax.rsqrt(var + LN_EPS) * g + b


def _route(logits_t, bias):
    scores = _sigmoid(logits_t)
    biased = scores + bias
    t = logits_t.shape[1]
    epg = EXPERTS_PER_GROUP
    sub = lax.broadcasted_iota(jnp.int32, (epg, t), 0).astype(F32)
    best = None
    for g in range(N_GROUPS):
        bgp = biased[g * epg:(g + 1) * epg]
        m1 = jnp.max(bgp, axis=0, keepdims=True)
        i1 = jnp.min(jnp.where(bgp == m1, sub, float(epg)), axis=0, keepdims=True)
        rest = jnp.where(sub == i1, -jnp.inf, bgp)
        m2 = jnp.max(rest, axis=0, keepdims=True)
        i2 = jnp.min(jnp.where(rest == m2, sub, float(epg)), axis=0, keepdims=True)
        gs = m1 + m2
        if best is None:
            best, e0, e1 = gs, i1, i2
        else:
            better = gs > best
            best = jnp.where(better, gs, best)
            e0 = jnp.where(better, float(g * epg) + i1, e0)
            e1 = jnp.where(better, float(g * epg) + i2, e1)
    eio = lax.broadcasted_iota(jnp.int32, scores.shape, 0).astype(F32)
    hit0 = eio == e0
    hit1 = eio == e1
    s0 = jnp.sum(jnp.where(hit0, scores, 0.0), axis=0, keepdims=True)
    s1 = jnp.sum(jnp.where(hit1, scores, 0.0), axis=0, keepdims=True)
    tot = s0 + s1
    return e0.astype(jnp.int32), e1.astype(jnp.int32), s0 / tot, s1 / tot, hit0, hit1


def _mix_out_kernel(y_ref, wo_ref, x_ref, moda_ref, modb_ref, lng_ref, lnb_ref, rwt_ref, rb_ref,
                    upper_ref, x1_ref, hm_ref, seli_ref, selw_ref, cnt_ref):
    d = D_MODEL
    gate = moda_ref[0, 0][:, 2 * d:]
    modb = modb_ref[0, 0]
    y = _dot(y_ref[0], wo_ref[...])
    x1 = _layer_norm(DN_ALPHA * x_ref[0] + (1.0 + gate) * y, lng_ref[...], lnb_ref[...])
    x1_ref[0] = x1
    hm = x1 * (1.0 + modb[:, d:2 * d]) + modb[:, :d]
    hm_hi = hm.astype(BF16)
    hm_ref[0] = hm_hi

    hm_lo = (hm - hm_hi.astype(F32)).astype(BF16)
    rw = rwt_ref[...]
    hi_lo = _dot_nt(rw, hm_hi)
    logits_t = hi_lo[:N_EXPERTS] + hi_lo[N_EXPERTS:] + _dot_nt(rw[:N_EXPERTS], hm_lo)
    e0, e1, w0, w1, hit0, hit1 = _route(logits_t, rb_ref[...])
    member = jnp.where(hit0 | hit1, 1.0, 0.0)
    before = _dot(member.astype(BF16), upper_ref[...])
    rank0 = jnp.sum(jnp.where(hit0, before, 0.0), axis=0, keepdims=True).astype(jnp.int32)
    rank1 = jnp.sum(jnp.where(hit1, before, 0.0), axis=0, keepdims=True).astype(jnp.int32)
    zi = jnp.zeros_like(e0)
    seli_ref[0] = jnp.concatenate([e0, e1, rank0, rank1, zi, zi, zi, zi], axis=0)
    zf = jnp.zeros_like(w0)
    selw_ref[0] = jnp.concatenate([w0, w1, zf, zf, zf, zf, zf, zf], axis=0)
    cnt_ref[0, 0] = jnp.broadcast_to(jnp.sum(member, axis=1, keepdims=True), (N_EXPERTS, LANES))


def mix_out(ymix, w_out, x, moda, modb, ln_g, ln_b, router_w, router_bias, b0=0):
    bsz, s, d = ymix.shape
    tm = min(TOKEN_TILE, s)
    nt = s // tm
    r = np.arange(tm)
    upper = jnp.asarray((r[:, None] < r[None, :]).astype(np.float32), dtype=BF16)
    rw_f32 = router_w.T.astype(F32)
    rw_hi = rw_f32.astype(BF16)
    rwt = jnp.concatenate([rw_hi, (rw_f32 - rw_hi.astype(F32)).astype(BF16)], axis=0)
    tok = lambda b, i: (b, i, 0)
    const = lambda b, i: (0, 0)
    modspec = pl.BlockSpec((1, 1, 1, 3 * d), lambda b, i: (b, 0, 0, 0))
    row8 = pl.BlockSpec((1, 8, tm), lambda b, i: (b, 0, i))
    return pl.pallas_call(
        _mix_out_kernel,
        grid=(bsz, nt),
        in_specs=[pl.BlockSpec((1, tm, d), tok),
                  pl.BlockSpec((d, d), const),
                  pl.BlockSpec((1, tm, d), lambda b, i: (b + b0, i, 0)),
                  modspec, modspec,
                  pl.BlockSpec((1, d), const), pl.BlockSpec((1, d), const),
                  pl.BlockSpec((2 * N_EXPERTS, d), const),
                  pl.BlockSpec((N_EXPERTS, 1), const),
                  pl.BlockSpec((tm, tm), const)],
        out_specs=[pl.BlockSpec((1, tm, d), tok),
                   pl.BlockSpec((1, tm, d), tok),
                   row8, row8,
                   pl.BlockSpec((1, 1, N_EXPERTS, LANES), lambda b, i: (b, i, 0, 0))],
        out_shape=[jax.ShapeDtypeStruct((bsz, s, d), F32),
                   jax.ShapeDtypeStruct((bsz, s, d), BF16),
                   jax.ShapeDtypeStruct((bsz, 8, s), jnp.int32),
                   jax.ShapeDtypeStruct((bsz, 8, s), F32),
                   jax.ShapeDtypeStruct((bsz, nt, N_EXPERTS, LANES), F32)],
        compiler_params=_cparams("parallel", "parallel"),
        name="mix_out",
    )(ymix, w_out.astype(BF16), x, moda, modb, ln_g.reshape(1, d), ln_b.reshape(1, d),
      rwt, router_bias.astype(F32).reshape(N_EXPERTS, 1), upper)


def _experts_kernel(be_ref, next_ref, nused_ref, xs_ref, wg_hbm, wu_hbm, wd_hbm, ys_ref,
                    wgf_ref, wuf_ref, wdf_ref, wgb_ref, wub_ref, wdb_ref, sem_ref, slot_ref, *, layer):
    i = pl.program_id(0)

    def weight_copies(e, slot):
        return [pltpu.make_async_copy(src.at[layer, e], dst.at[slot], sem_ref.at[n, slot])
                for n, (src, dst) in enumerate(((wg_hbm, wgf_ref), (wu_hbm, wuf_ref), (wd_hbm, wdf_ref)))]

    @pl.when(i == 0)
    def _():
        slot_ref[0] = 0
        for cp in weight_copies(be_ref[0], 0):
            cp.start()

    @pl.when((i == 0) | (be_ref[i] != be_ref[jnp.maximum(i - 1, 0)]))
    def _():
        slot = slot_ref[0]
        for cp in weight_copies(be_ref[i], slot):
            cp.wait()

        @pl.when(next_ref[i] >= 0)
        def _():
            for cp in weight_copies(next_ref[i], 1 - slot):
                cp.start()

        wgb_ref[...] = wgf_ref[slot].astype(BF16)
        wub_ref[...] = wuf_ref[slot].astype(BF16)
        wdb_ref[...] = wdf_ref[slot].astype(BF16)
        slot_ref[0] = 1 - slot

    @pl.when(i < nused_ref[0])
    def _():
        x = xs_ref[...]
        act =(_silu(_dot(x, wgb_ref[...])) * _dot(x, wub_ref[...])).astype(BF16)
        ys_ref[...] = _dot(act, wdb_ref[...]).astype(BF16)

    @pl.when(i >= nused_ref[0])
    def _():
        ys_ref[...] = jnp.zeros_like(ys_ref)


def moe_experts(xs, block_expert, n_used, w_gate, w_up, w_down, layer):
    rows, d = xs.shape
    nb = rows // MOE_BLOCK
    later = jnp.where(block_expert[None, :] > block_expert[:, None], block_expert[None, :], N_EXPERTS)
    next_expert = jnp.min(later, axis=1)
    next_expert = jnp.where(next_expert == N_EXPERTS, -1, next_expert).astype(jnp.int32)
    hbm = pl.BlockSpec(memory_space=pl.ANY)
    rows_spec = pl.BlockSpec((MOE_BLOCK, d), lambda i, be, nx, nu: (i, 0))
    return pl.pallas_call(
        functools.partial(_experts_kernel, layer=layer),
        grid_spec=pltpu.PrefetchScalarGridSpec(
            num_scalar_prefetch=3,
            grid=(nb,),
            in_specs=[rows_spec, hbm, hbm, hbm],
            out_specs=rows_spec,
            scratch_shapes=[pltpu.VMEM((2, d, D_EXPERT), F32), pltpu.VMEM((2, d, D_EXPERT), F32),
                            pltpu.VMEM((2, D_EXPERT, d), F32),
                            pltpu.VMEM((d, D_EXPERT), BF16), pltpu.VMEM((d, D_EXPERT), BF16),
                            pltpu.VMEM((D_EXPERT, d), BF16),
                            pltpu.SemaphoreType.DMA((3, 2)), pltpu.SMEM((1,), jnp.int32)]),
        out_shape=jax.ShapeDtypeStruct((rows, d), BF16),
        compiler_params=_cparams("arbitrary"),
        name="moe_experts",
    )(block_expert, next_expert, n_used, xs, w_gate, w_up, w_down)


def _plan_kernel(seli_ref, base_ref, dest_ref):
    sel = seli_ref[0]
    base = base_ref[0, 0]
    eio = lax.broadcasted_iota(jnp.int32, (N_EXPERTS, sel.shape[1]), 0)
    out = []
    for kk in range(2):
        first = jnp.sum(jnp.where(eio == sel[kk:kk + 1], base, 0.0), axis=0, keepdims=True)
        out.append(first.astype(jnp.int32) + sel[2 + kk:3 + kk])
    zero = jnp.zeros_like(out[0])
    dest_ref[0] = jnp.concatenate(out + [zero] * 6, axis=0)


def dispatch_plan(seli, base):
    bsz, _, s = seli.shape
    nt = base.shape[1]
    tm = s // nt
    return pl.pallas_call(
        _plan_kernel,
        grid=(bsz, nt),
        in_specs=[pl.BlockSpec((1, 8, tm), lambda b, i: (b, 0, i)),
                  pl.BlockSpec((1, 1, N_EXPERTS, 1), lambda b, i: (b, i, 0, 0))],
        out_specs=pl.BlockSpec((1, 8, tm), lambda b, i: (b, 0, i)),
        out_shape=jax.ShapeDtypeStruct((bsz, 8, s), jnp.int32),
        compiler_params=_cparams("parallel", "parallel"),
        name="dispatch_plan",
    )(seli, base)


def _invert_kernel(dest_ref, pad_lo_ref, pad_hi_ref, slot_tok_ref):
    t = dest_ref.shape[0] // 2

    def clear_range(e, carry):
        hi = pad_hi_ref[e]
        n = (hi - pad_lo_ref[e] + CLEAR_CHUNK - 1) // CLEAR_CHUNK

        def clear(q, c):
            r0 = hi - (q + 1) * CLEAR_CHUNK
            for j in range(CLEAR_CHUNK):
                slot_tok_ref[r0 + j] = lax.rem(r0 + j, t)
            return c
        return lax.fori_loop(0, n, clear, carry)

    def put(a, carry):
        slot_tok_ref[dest_ref[a]] = a
        slot_tok_ref[dest_ref[t + a]] = a
        return carry

    lax.fori_loop(0, pad_lo_ref.shape[0], clear_range, 0)
    lax.fori_loop(0, t, put, 0, unroll=8)


def invert_slots(dest_flat, pad_lo, pad_hi, rows):
    smem = pl.BlockSpec(memory_space=pltpu.SMEM)
    return pl.pallas_call(
        _invert_kernel,
        in_specs=[smem, smem, smem],
        out_specs=smem,
        out_shape=jax.ShapeDtypeStruct((rows,), jnp.int32),
        name="invert_slots",
    )(dest_flat, pad_lo, pad_hi)


def _moe_combine_kernel(g0_ref, g1_ref, wt_ref, x_ref, mod_ref, lng_ref, lnb_ref, o_ref):
    d = D_MODEL
    gate = mod_ref[0, 0][:, 2 * d:]
    wt = wt_ref[0]
    y = g0_ref[0, 0].astype(F32) * wt[:, 0:1] + g1_ref[0, 0].astype(F32) * wt[:, 1:2]
    o_ref[0] = _layer_norm(DN_ALPHA * x_ref[0] + (1.0 + gate) * y, lng_ref[...], lnb_ref[...])


def moe_combine(g, wt, x1, modb, ln_g, ln_b):
    bsz, s, d = x1.shape
    tm = min(TOKEN_TILE, s)
    tok = lambda b, i: (b, i, 0)
    const = lambda b, i: (0, 0)
    return pl.pallas_call(
        _moe_combine_kernel,
        grid=(bsz, s // tm),
        in_specs=[pl.BlockSpec((1, 1, tm, d), lambda b, i: (0, b, i, 0)),
                  pl.BlockSpec((1, 1, tm, d), lambda b, i: (1, b, i, 0)),
                  pl.BlockSpec((1, tm, 8), tok), pl.BlockSpec((1, tm, d), tok),
                  pl.BlockSpec((1, 1, 1, 3 * d), lambda b, i: (b, 0, 0, 0)),
                  pl.BlockSpec((1, d), const), pl.BlockSpec((1, d), const)],
        out_specs=pl.BlockSpec((1, tm, d), tok),
        out_shape=jax.ShapeDtypeStruct((bsz, s, d), F32),
        compiler_params=_cparams("parallel", "parallel"),
        name="moe_combine",
    )(g, g, wt, x1, modb, ln_g.reshape(1, d), ln_b.reshape(1, d))


def moe_layer(x1, hm, seli, selw, cnt, modb, ln_g, ln_b, w_gate, w_up, w_down, layer):
    bsz, s, d = x1.shape
    t = bsz * s
    tm = min(TOKEN_TILE, s)
    blk = MOE_BLOCK
    cnt = cnt[..., 0].reshape(-1, N_EXPERTS).astype(jnp.int32)
    tile_off = jnp.cumsum(cnt, axis=0) - cnt
    counts = cnt.sum(0)
    padded = (counts + blk - 1) // blk * blk
    pend = jnp.cumsum(padded)
    pstart = pend - padded
    nb = -(-(2 * t) // blk) + N_EXPERTS
    rows = nb * blk
    base = (pstart[None, :] + tile_off).astype(F32).reshape(bsz, s // tm, N_EXPERTS, 1)
    dest = dispatch_plan(seli, base)
    dest = jnp.swapaxes(dest[:, 0:2, :], 0, 1).reshape(2 * t)
    pad_lo = jnp.concatenate([pstart + counts, pend[-1:]]).astype(jnp.int32)
    pad_hi = jnp.concatenate([pend, jnp.full((1,), rows)]).astype(jnp.int32)
    slot_tok = invert_slots(dest, pad_lo, pad_hi, rows)
    starts = jnp.arange(nb, dtype=jnp.int32) * blk
    block_expert = jnp.minimum(jnp.sum((pend[None, :] <= starts[:, None]).astype(jnp.int32), axis=1),
                               N_EXPERTS - 1)
    n_used = (pend[-1] // blk).astype(jnp.int32).reshape(1)
    xs = hm.reshape(t, d).at[slot_tok].get(mode="promise_in_bounds")
    ys = moe_experts(xs, block_expert, n_used, w_gate, w_up, w_down, layer)
    g = ys.at[dest].get(mode="promise_in_bounds").reshape(2, bsz, s, d)
    wt = jnp.swapaxes(selw, 1, 2)
    return moe_combine(g, wt, x1, modb, ln_g, ln_b)


MLA_IN_COLS = Q_LORA + KV_LORA + 2 * LANES
MLA_QK = 2 * LANES


def _mla_in_kernel(x_ref, mod_ref, pos_ref, invf_ref, win_ref, qn_ref, kvn_ref, wq_ref, wkv_ref,
                   q_ref, k_ref, v_ref):
    d = D_MODEL
    nh = MLA_HEADS
    mod = mod_ref[0, 0]
    h = (x_ref[0] * (1.0 + mod[:, d:2 * d]) + mod[:, :d]).astype(BF16)
    proj = _dot(h, win_ref[...])
    ang = pos_ref[0].astype(F32) * invf_ref[...]
    lane = lax.broadcasted_iota(jnp.int32, ang.shape, 1)
    cos = jnp.where(lane < QK_ROPE, jnp.cos(ang), 0.0)
    sin = jnp.where(lane < QK_ROPE, jnp.sin(ang), 0.0)

    qa = proj[:, :Q_LORA]
    qa = (qa * lax.rsqrt(jnp.mean(qa * qa, -1, keepdims=True) + NORM_EPS) * qn_ref[...]).astype(BF16)
    kva = proj[:, Q_LORA:Q_LORA + KV_LORA]
    kva = (kva * lax.rsqrt(jnp.mean(kva * kva, -1, keepdims=True) + NORM_EPS) * kvn_ref[...]).astype(BF16)
    kr0 = Q_LORA + KV_LORA
    k_rope = (proj[:, kr0:kr0 + LANES] * cos + proj[:, kr0 + LANES:kr0 + 2 * LANES] * sin).astype(BF16)

    scale = (QK_NOPE + QK_ROPE) ** -0.5 * LOG2_E
    hw = nh * LANES
    q_nope = _dot(qa, wq_ref[:, :hw])
    q_rope = _dot(qa, wq_ref[:, hw:2 * hw])
    q_rot = _dot(qa, wq_ref[:, 2 * hw:])
    k_nope = _dot(kva, wkv_ref[:, :hw])
    v_ref[0] = _dot(kva, wkv_ref[:, hw:]).astype(BF16)
    for hh in range(nh):
        cols = slice(hh * LANES, (hh + 1) * LANES)
        q_ref[0, :, hh * MLA_QK:hh * MLA_QK + LANES] = (q_nope[:, cols] * scale).astype(BF16)
        q_ref[0, :, hh * MLA_QK + LANES:(hh + 1) * MLA_QK] = (
            (q_rope[:, cols] * cos + q_rot[:, cols] * sin) * scale).astype(BF16)
        k_ref[0, :, hh * MLA_QK:hh * MLA_QK + LANES] = k_nope[:, cols].astype(BF16)
        k_ref[0, :, hh * MLA_QK + LANES:(hh + 1) * MLA_QK] = k_rope


def _rope_cols(w):
    half = QK_ROPE // 2
    pad = [(0, 0)] * (w.ndim - 1) + [(0, LANES - QK_ROPE)]
    rot = jnp.concatenate([-w[..., half:], w[..., :half]], axis=-1)
    return jnp.pad(w, pad), jnp.pad(rot, pad)


def mla_in(x, mod, positions, w_in, q_a_norm, w_q_b, kv_a_norm, w_kv_b):
    bsz, s, d = x.shape
    tm = min(TOKEN_TILE, s)
    nh = MLA_HEADS
    kr, kr_rot = _rope_cols(w_in[:, Q_LORA + KV_LORA:])
    win = jnp.concatenate([w_in[:, :Q_LORA + KV_LORA], kr, kr_rot], axis=1).astype(BF16)
    wq = w_q_b.reshape(Q_LORA, nh, QK_NOPE + QK_ROPE)
    qr, qr_rot = _rope_cols(wq[..., QK_NOPE:])
    wq = jnp.concatenate([wq[..., :QK_NOPE].reshape(Q_LORA, -1), qr.reshape(Q_LORA, -1),
                          qr_rot.reshape(Q_LORA, -1)], axis=1).astype(BF16)
    wkv = w_kv_b.reshape(KV_LORA, nh, QK_NOPE + V_DIM)
    wkv = jnp.concatenate([wkv[..., :QK_NOPE].reshape(KV_LORA, -1),
                           wkv[..., QK_NOPE:].reshape(KV_LORA, -1)], axis=1).astype(BF16)
    inv = ROPE_THETA ** (-np.arange(0, QK_ROPE, 2, dtype=np.float32) / QK_ROPE)
    invf = np.zeros((1, LANES), np.float32)
    invf[0, :QK_ROPE] = np.concatenate([inv, inv])
    tok = lambda b, i: (b, i, 0)
    const = lambda b, i: (0, 0)
    return pl.pallas_call(
        _mla_in_kernel,
        grid=(bsz, s // tm),
        in_specs=[pl.BlockSpec((1, tm, d), tok),
                  pl.BlockSpec((1, 1, 1, 3 * d), lambda b, i: (b, 0, 0, 0)),
                  pl.BlockSpec((1, tm, 1), tok),
                  pl.BlockSpec((1, LANES), const),
                  pl.BlockSpec((d, MLA_IN_COLS), const),
                  pl.BlockSpec((1, Q_LORA), const),
                  pl.BlockSpec((1, KV_LORA), const),
                  pl.BlockSpec((Q_LORA, 3 * nh * LANES), const),
                  pl.BlockSpec((KV_LORA, 2 * nh * LANES), const)],
        out_specs=[pl.BlockSpec((1, tm, nh * MLA_QK), tok),
                   pl.BlockSpec((1, tm, nh * MLA_QK), tok),
                   pl.BlockSpec((1, tm, nh * V_DIM), tok)],
        out_shape=[jax.ShapeDtypeStruct((bsz, s, nh * MLA_QK), BF16),
                   jax.ShapeDtypeStruct((bsz, s, nh * MLA_QK), BF16),
                   jax.ShapeDtypeStruct((bsz, s, nh * V_DIM), BF16)],
        compiler_params=_cparams("parallel", "parallel"),
        name="mla_in",
    )(x, mod, positions.reshape(bsz, s, 1), jnp.asarray(invf), win,
      q_a_norm.astype(F32).reshape(1, -1), kv_a_norm.astype(F32).reshape(1, -1), wq, wkv)


def _attn_kernel(q_ref, k_ref, v_ref, o_ref, m_ref, l_ref, acc_ref):
    tq = q_ref.shape[1]
    i = pl.program_id(2)
    m_ref[...] = jnp.full(m_ref.shape, -jnp.inf, F32)
    l_ref[...] = jnp.zeros(l_ref.shape, F32)
    acc_ref[...] = jnp.zeros(acc_ref.shape, F32)

    groups = [slice(g * ATTN_ROWS, (g + 1) * ATTN_ROWS) for g in range(tq // ATTN_ROWS)]

    def update(off, widths, masked):
        scs = []
        for rows, width in zip(groups, widths):
            sc = _dot_nt(q_ref[0, rows, :], k_ref[0, pl.ds(off, width), :])
            if masked:
                qi = lax.broadcasted_iota(jnp.int32, sc.shape, 0) + rows.start
                ki = lax.broadcasted_iota(jnp.int32, sc.shape, 1)
                sc = jnp.where(ki <= qi, sc, -jnp.inf)
            scs.append(sc)
        m_olds = [m_ref[rows] for rows in groups]
        m_news = [jnp.maximum(m_old, jnp.max(sc, -1, keepdims=True)) for m_old, sc in zip(m_olds, scs)]
        ps = [jnp.exp2(sc - jnp.concatenate([m_new] * (width // LANES), axis=1))
              for sc, m_new, width in zip(scs, m_news, widths)]
        alphas = [jnp.exp2(m_old - m_new) for m_old, m_new in zip(m_olds, m_news)]
        pvs = [_dot(p.astype(BF16), v_ref[0, pl.ds(off, width), :]) for p, width in zip(ps, widths)]
        for rows, m_new, alpha, p, pv in zip(groups, m_news, alphas, ps, pvs):
            l_ref[rows] = alpha * l_ref[rows] + jnp.sum(p, -1, keepdims=True)
            acc_ref[rows] = alpha * acc_ref[rows] + pv
            m_ref[rows] = m_new

    def full_block(j, carry):
        update(pl.multiple_of(j * tq, tq), [tq] * len(groups), False)
        return carry

    lax.fori_loop(0, i, full_block, 0)
    update(pl.multiple_of(i * tq, tq), [rows.stop for rows in groups], True)
    o_ref[0] = (acc_ref[...] / l_ref[...]).astype(BF16)


def attention(q, k, v):
    bsz, s, _ = q.shape
    tq = min(ATTN_TILE, s)
    return pl.pallas_call(
        _attn_kernel,
        grid=(bsz, MLA_HEADS, s // tq),
        in_specs=[pl.BlockSpec((1, tq, MLA_QK), lambda b, h, i: (b, i, h)),
                  pl.BlockSpec((1, s, MLA_QK), lambda b, h, i: (b, 0, h)),
                  pl.BlockSpec((1, s, V_DIM), lambda b, h, i: (b, 0, h))],
        out_specs=pl.BlockSpec((1, tq, V_DIM), lambda b, h, i: (b, i, h)),
        out_shape=jax.ShapeDtypeStruct((bsz, s, MLA_HEADS * V_DIM), BF16),
        scratch_shapes=[pltpu.VMEM((tq, LANES), F32), pltpu.VMEM((tq, LANES), F32), pltpu.VMEM((tq, V_DIM), F32)],
        compiler_params=_cparams("parallel", "parallel", "arbitrary"),
        name="attention",
    )(q, k, v)


def kernel(x, c, positions, ada_w, ada_b, ln_g, ln_b, w_in_e, gdn_conv_w, gdn_a_log, gdn_dt_bias,
           gdn_norm_w, sc_conv_w, w_out_e, w_in_o, q_a_norm, w_q_b, kv_a_norm, w_kv_b, w_out_o,
           router_w, router_bias, w_gate, w_up, w_down):
    bsz, s, d = x.shape
    depth = ada_w.shape[0]
    mod = ada_mod(c, ada_w, ada_b).reshape(depth, 2, bsz, 1, 3 * d)

    def layers(x, b0, mod, positions):
        for i in range(depth):
            j = i // 2
            moda, modb = mod[i, 0][:, None], mod[i, 1][:, None]
            if i % 2 == 0:
                o1, bg, bgt = even_in_proj(x, moda, w_in_e[j], gdn_a_log[j], gdn_dt_bias[j], b0)
                qd, kd, u, w, ic, gl = gdn_prep(o1, bg, bgt, gdn_conv_w[j])
                o = gdn_scan(qd, kd, u, w, ic, gl)
                ymix = even_post(o, o1, gdn_norm_w[j], sc_conv_w[j])
                w_out = w_out_e[j]
            else:
                q, k, v = mla_in(x, moda, positions, w_in_o[j], q_a_norm[j], w_q_b[j], kv_a_norm[j],
                                 w_kv_b[j])
                ymix = attention(q, k, v)
                w_out = w_out_o[j]
            x1, hm, seli, selw, cnt = mix_out(ymix, w_out, x, moda, modb, ln_g[i, 0], ln_b[i, 0],
                                              router_w, router_bias, b0)
            x = moe_layer(x1, hm, seli, selw, cnt, modb, ln_g[i, 1], ln_b[i, 1], w_gate, w_up, w_down, i)
            b0 = 0
        return x

    ng = BATCH_GROUPS if bsz % BATCH_GROUPS == 0 else 1
    gb = bsz // ng
    outs = [layers(x, g * gb, mod[:, :, g * gb:(g + 1) * gb], positions[g * gb:(g + 1) * gb])
            for g in range(ng)]
    return outs[0] if ng == 1 else jnp.concatenate(outs, axis=0)
```

```python
import functools

import numpy as np
import jax
import jax.numpy as jnp
from jax import lax
from jax.experimental import pallas as pl
from jax.experimental.pallas import tpu as pltpu

F32 = jnp.float32
BF16 = jnp.bfloat16
HIGHEST = lax.Precision.HIGHEST

D_MODEL = 1024
DEPTH = 2
DN_ALPHA = (2.0 * DEPTH) ** 0.25

GDN_HEADS = 4
GDN_HEAD_DIM = 128
GDN_WIDTH = GDN_HEADS * GDN_HEAD_DIM
GDN_CONV = 4
GDN_CHUNK = 64
SC_WIDTH = 512
SC_CONV = 3

MLA_HEADS = 8
Q_LORA = 384
KV_LORA = 256
QK_NOPE = 128
QK_ROPE = 64
V_DIM = 128
ROPE_THETA = 10000.0

N_EXPERTS = 32
N_GROUPS = 4
EXPERTS_PER_GROUP = N_EXPERTS // N_GROUPS
D_EXPERT = 512

NORM_EPS = 1e-6
LN_EPS = 1e-5

LANES = 128
HALO = 16
TOKEN_TILE = 512
GDN_TILE = 256
ATTN_TILE = 1024
ATTN_ROWS = 256
LOG2_E = 1.4426950408889634
MOE_BLOCK = 512
SCAN_BATCH = 4
BATCH_GROUPS = 1
CLEAR_CHUNK = 8
VMEM_LIMIT = 48 * 1024 * 1024


def _cparams(*sem):
    return pltpu.CompilerParams(dimension_semantics=sem, vmem_limit_bytes=VMEM_LIMIT)


def _sigmoid(x):
    return 1.0 / (1.0 + jnp.exp(-x))


def _silu(x):
    return x * _sigmoid(x)


def _dot(a, b):
    return jnp.dot(a, b, preferred_element_type=F32)


def _dot_nt(a, b, precision=None):
    return lax.dot_general(a, b, (((1,), (1,)), ((), ())), precision=precision,
                           preferred_element_type=F32)


def _dot_tn(a, b):
    return lax.dot_general(a, b, (((0,), (0,)), ((), ())), preferred_element_type=F32)


def _ada_kernel(c_ref, w_ref, b_ref, o_ref):
    cond = _silu(c_ref[...])
    o_ref[0] = jnp.dot(cond, w_ref[0], precision=HIGHEST, preferred_element_type=F32) + b_ref[0]


def ada_mod(c, ada_w, ada_b):
    nl = ada_w.shape[0] * ada_w.shape[1]
    bsz, d = c.shape
    w = ada_w.reshape(nl, d, 3 * d)
    b = ada_b.reshape(nl, 1, 3 * d)
    return pl.pallas_call(
        _ada_kernel,
        grid=(nl, 3),
        in_specs=[pl.BlockSpec((bsz, d), lambda l, j: (0, 0)),
                  pl.BlockSpec((1, d, d), lambda l, j: (l, 0, j)),
                  pl.BlockSpec((1, 1, d), lambda l, j: (l, 0, j))],
        out_specs=pl.BlockSpec((1, bsz, d), lambda l, j: (l, 0, j)),
        out_shape=jax.ShapeDtypeStruct((nl, bsz, 3 * d), F32),
        compiler_params=_cparams("parallel", "parallel"),
        name="ada_mod",
    )(c, w, b)


EVEN_MAIN = 3 * GDN_WIDTH + GDN_WIDTH + 3 * SC_WIDTH
EVEN_OUT = 3 * GDN_WIDTH + GDN_WIDTH + SC_WIDTH


def _causal_conv(x, halo, w, taps):
    rows = x.shape[0]
    nh = halo.shape[0]
    xf = jnp.concatenate([halo, x], axis=0)
    y = w[taps - 1:taps] * x
    for j in range(taps - 1):
        off = nh - (taps - 1) + j
        y = y + w[j:j + 1] * xf[off:off + rows]
    return y


def _even_in_kernel(x_ref, mod_ref, w1_ref, w2_ref, alog_ref, dtb_ref, gcw_ref, scw_ref,
                    o1_ref, o2_ref, o2t_ref, qkv_tail_ref, ch_tail_ref):
    d = D_MODEL
    cw = 512
    first = pl.program_id(1) == 0
    mod = mod_ref[0, 0]
    h = (x_ref[0] * (1.0 + mod[:, d:2 * d]) + mod[:, :d]).astype(BF16)
    tm = h.shape[0]
    gcw = gcw_ref[...]

    def project(j):
        return _dot(h, w1_ref[:, j * cw:(j + 1) * cw])

    def finish_qkv(j, r):
        cols = slice(j * cw, (j + 1) * cw)
        halo = jnp.where(first, 0.0, qkv_tail_ref[:, cols])
        o1_ref[0, :, cols] = _silu(_causal_conv(r, halo, gcw[:, cols], GDN_CONV)).astype(BF16)
        qkv_tail_ref[:, cols] = r[tm - HALO:]

    r0 = project(0)
    r1 = project(1)
    finish_qkv(0, r0)
    r2 = project(2)
    finish_qkv(1, r1)
    z = project(3)
    finish_qkv(2, r2)
    sc_c = project(5)
    o1_ref[0, :, 3 * cw:4 * cw] = _silu(z).astype(BF16)
    sc_h = project(6)
    sc_b = project(4)
    ch = sc_c * sc_h
    halo = jnp.where(first, 0.0, ch_tail_ref[...])
    r = _dot(h, w2_ref[...])
    o1_ref[0, :, 4 * cw:5 * cw] = (sc_b * _causal_conv(ch, halo, scw_ref[...], SC_CONV)).astype(BF16)
    ch_tail_ref[...] = ch[tm - HALO:]
    lane = lax.broadcasted_iota(jnp.int32, r.shape, 1)
    a = r + dtb_ref[...]
    softplus = jnp.maximum(a, 0.0) + jnp.log(1.0 + jnp.exp(-jnp.abs(a)))
    bg = jnp.where(lane < GDN_HEADS, _sigmoid(r), -jnp.exp(alog_ref[...]) * softplus)
    o2_ref[0] = bg
    o2t_ref[0] = bg.T[:8]


def even_in_proj(x, mod, w_in, a_log, dt_bias, gdn_conv_w, sc_conv_w, b0=0):
    _, s, d = x.shape
    bsz = mod.shape[0]
    tm = min(TOKEN_TILE, s)
    q_end = 4 * GDN_WIDTH
    w1 = jnp.concatenate([w_in[:, :q_end], w_in[:, q_end + 2 * GDN_HEADS:]], axis=1).astype(BF16)
    w2 = jnp.pad(w_in[:, q_end:q_end + 2 * GDN_HEADS], ((0, 0), (0, LANES - 2 * GDN_HEADS))).astype(BF16)
    alog = jnp.pad(a_log.astype(F32), (GDN_HEADS, LANES - 2 * GDN_HEADS)).reshape(1, LANES)
    dtb = jnp.pad(dt_bias.astype(F32), (GDN_HEADS, LANES - 2 * GDN_HEADS)).reshape(1, LANES)
    const = lambda b, i: (0, 0)
    return pl.pallas_call(
        _even_in_kernel,
        grid=(bsz, s // tm),
        in_specs=[pl.BlockSpec((1, tm, d), lambda b, i: (b + b0, i, 0)),
                  pl.BlockSpec((1, 1, 1, 3 * d), lambda b, i: (b, 0, 0, 0)),
                  pl.BlockSpec((d, EVEN_MAIN), const),
                  pl.BlockSpec((d, LANES), const),
                  pl.BlockSpec((1, LANES), const),
                  pl.BlockSpec((1, LANES), const),
                  pl.BlockSpec((GDN_CONV, 3 * GDN_WIDTH), const),
                  pl.BlockSpec((SC_CONV, SC_WIDTH), const)],
        out_specs=[pl.BlockSpec((1, tm, EVEN_OUT), lambda b, i: (b, i, 0)),
                   pl.BlockSpec((1, tm, LANES), lambda b, i: (b, i, 0)),
                   pl.BlockSpec((1, 8, tm), lambda b, i: (b, 0, i))],
        out_shape=[jax.ShapeDtypeStruct((bsz, s, EVEN_OUT), BF16),
                   jax.ShapeDtypeStruct((bsz, s, LANES), F32),
                   jax.ShapeDtypeStruct((bsz, 8, s), F32)],
        scratch_shapes=[pltpu.VMEM((HALO, 3 * GDN_WIDTH), F32), pltpu.VMEM((HALO, SC_WIDTH), F32)],
        compiler_params=_cparams("parallel", "arbitrary"),
        name="even_in_proj",
    )(x, mod, w1, w2, alog, dtb, gdn_conv_w.astype(F32), sc_conv_w.astype(F32))


def _inv_unit_lower(lows, xor_ij, block):
    eye = jnp.where(xor_ij == 0, 1.0, 0.0)
    ms = [eye - jnp.where(xor_ij == 1, low, 0.0) for low in lows]
    s = 2
    while s < block:
        level = (xor_ij >> (s.bit_length() - 1)) == 1
        mbs = [m.astype(BF16) for m in ms]
        cms = [_dot(jnp.where(level, low, 0.0).astype(BF16), mb).astype(BF16) for low, mb in zip(lows, mbs)]
        ms = [m - _dot(mb, cm) for m, mb, cm in zip(ms, mbs, cms)]
        s *= 2
    return ms


def _gdn_prep_kernel(qkv_ref, bg_ref, bgt_ref, ltri_ref, utri_ref, same_ref,
                     qd_ref, kd_ref, u_ref, w_ref, ic_ref, gl_ref):
    ts = qkv_ref.shape[1]
    c = GDN_CHUNK
    hd = GDN_HEAD_DIM
    y = qkv_ref[0].astype(F32)

    bg = bg_ref[0]
    gc_col = jnp.dot(ltri_ref[...], bg, precision=HIGHEST, preferred_element_type=F32)
    gc_row = jnp.dot(bgt_ref[0], utri_ref[...], precision=HIGHEST, preferred_element_type=F32)
    gc_end = jnp.dot(same_ref[...], bg, precision=HIGHEST, preferred_element_type=F32)

    ii = lax.broadcasted_iota(jnp.int32, (ts, ts), 0)
    jj = lax.broadcasted_iota(jnp.int32, (ts, ts), 1)
    xor_ij = ii ^ jj
    causal = (same_ref[...] > 0.0) & (ii >= jj)
    diag = xor_ij == 0

    heads = range(GDN_HEADS)
    gcols = [gc_col[:, GDN_HEADS + h:GDN_HEADS + h + 1] for h in heads]
    gends = [gc_end[:, GDN_HEADS + h:GDN_HEADS + h + 1] for h in heads]
    egcs = [jnp.exp(g) for g in gcols]
    lows, intras, rhss, qs, ks = [], [], [], [], []
    for h in heads:
        q = y[:, h * hd:(h + 1) * hd]
        k = y[:, GDN_WIDTH + h * hd:GDN_WIDTH + (h + 1) * hd]
        v = y[:, 2 * GDN_WIDTH + h * hd:2 * GDN_WIDTH + (h + 1) * hd]
        q = q * lax.rsqrt(jnp.sum(q * q, -1, keepdims=True) + NORM_EPS) * (hd ** -0.5)
        k = k * lax.rsqrt(jnp.sum(k * k, -1, keepdims=True) + NORM_EPS)
        beta = bg[:, h:h + 1]
        grow = gc_row[GDN_HEADS + h:GDN_HEADS + h + 1, :]
        decay = jnp.exp(jnp.where(causal, gcols[h] - grow, -jnp.inf))
        kb = k * beta
        kbf = k.astype(BF16)
        lows.append(jnp.where(diag, 0.0, _dot_nt(kb.astype(BF16), kbf) * decay))
        intras.append(_dot_nt(q.astype(BF16), kbf) * decay)
        rhss.append(jnp.concatenate([v * beta, kb * egcs[h]], axis=-1).astype(BF16))
        qs.append(q)
        ks.append(k)

    invs = _inv_unit_lower(lows, xor_ij, c)
    for h in heads:
        sol = _dot(invs[h].astype(BF16), rhss[h])
        cols = slice(h * hd, (h + 1) * hd)
        u_ref[0, :, cols] = sol[:, :hd]
        w_ref[0, :, cols] = sol[:, hd:].astype(BF16)
        qd_ref[0, :, cols] = (qs[h] * egcs[h]).astype(BF16)
        kd_ref[0, :, cols] = (ks[h] * jnp.exp(gends[h] - gcols[h])).astype(BF16)
        packed = intras[h][:, :c]
        for n in range(1, ts // c):
            packed = packed + intras[h][:, n * c:(n + 1) * c]
        ic_ref[0, :, h * c:(h + 1) * c] = packed.astype(BF16)
        for n in range(ts // c):
            gl_ref[0, n, h:h + 1, :] = jnp.broadcast_to(jnp.exp(gends[h][n * c:n * c + 1]), (1, hd))


def gdn_prep(o1, bg, bgt):
    bsz, s, _ = o1.shape
    ts = min(GDN_TILE, s)
    c = GDN_CHUNK
    nc = ts // c
    qkv_w = 3 * GDN_WIDTH
    r = np.arange(ts)
    same = (r[:, None] // c) == (r[None, :] // c)
    ltri = jnp.asarray((same & (r[:, None] >= r[None, :])).astype(np.float32))
    utri = jnp.asarray((same & (r[:, None] <= r[None, :])).astype(np.float32))
    same = jnp.asarray(same.astype(np.float32))
    tok = lambda b, i: (b, i, 0)
    const = lambda b, i: (0, 0)
    wide = jax.ShapeDtypeStruct((bsz, s, GDN_WIDTH), BF16)
    return pl.pallas_call(
        _gdn_prep_kernel,
        grid=(bsz, s // ts),
        in_specs=[pl.BlockSpec((1, ts, qkv_w), tok),
                  pl.BlockSpec((1, ts, LANES), tok),
                  pl.BlockSpec((1, 8, ts), lambda b, i: (b, 0, i)),
                  pl.BlockSpec((ts, ts), const),
                  pl.BlockSpec((ts, ts), const),
                  pl.BlockSpec((ts, ts), const)],
        out_specs=[pl.BlockSpec((1, ts, GDN_WIDTH), tok),
                   pl.BlockSpec((1, ts, GDN_WIDTH), tok),
                   pl.BlockSpec((1, ts, GDN_WIDTH), tok),
                   pl.BlockSpec((1, ts, GDN_WIDTH), tok),
                   pl.BlockSpec((1, ts, GDN_HEADS * c), tok),
                   pl.BlockSpec((1, nc, GDN_HEADS, GDN_HEAD_DIM), lambda b, i: (b, i, 0, 0))],
        out_shape=[wide, wide, jax.ShapeDtypeStruct((bsz, s, GDN_WIDTH), F32), wide,
                   jax.ShapeDtypeStruct((bsz, s, GDN_HEADS * c), BF16),
                   jax.ShapeDtypeStruct((bsz, s // c, GDN_HEADS, GDN_HEAD_DIM), F32)],
        compiler_params=_cparams("parallel", "parallel"),
        name="gdn_prep",
    )(o1, bg, bgt, ltri, utri, same)


def _gdn_scan_kernel(qd_ref, kd_ref, u_ref, w_ref, ic_ref, gl_ref, o_ref, state_ref):
    nb, ts = qd_ref.shape[0], qd_ref.shape[1]
    c = GDN_CHUNK
    hd = GDN_HEAD_DIM

    @pl.when(pl.program_id(1) == 0)
    def _():
        state_ref[...] = jnp.zeros_like(state_ref)

    chains = [(b, h) for b in range(nb) for h in range(GDN_HEADS)]
    cols = [slice(h * hd, (h + 1) * hd) for _, h in chains]
    sts = [state_ref[b * GDN_HEADS + h] for b, h in chains]
    for n in range(ts // c):
        rows = slice(n * c, (n + 1) * c)
        stbs = [st.astype(BF16) for st in sts]
        vbs = [(u_ref[b, rows, cl] - _dot(w_ref[b, rows, cl], stb)).astype(BF16)
               for (b, _), cl, stb in zip(chains, cols, stbs)]
        for (b, h), cl, stb, vb in zip(chains, cols, stbs, vbs):
            o_ref[b, rows, cl] = _dot(qd_ref[b, rows, cl], stb) + _dot(ic_ref[b, rows, h * c:(h + 1) * c], vb)
        sts = [st * gl_ref[b, n, h:h + 1, :] + _dot_tn(kd_ref[b, rows, cl], vb)
               for (b, h), cl, st, vb in zip(chains, cols, sts, vbs)]
    for (b, h), st in zip(chains, sts):
        state_ref[b * GDN_HEADS + h] = st


def gdn_scan(qd, kd, u, w, ic, gl):
    bsz, s, _ = qd.shape
    ts = min(GDN_TILE, s)
    c = GDN_CHUNK
    nb = SCAN_BATCH if bsz % SCAN_BATCH == 0 else 1
    tok = lambda b, i: (b, i, 0)
    return pl.pallas_call(
        _gdn_scan_kernel,
        grid=(bsz // nb, s // ts),
        in_specs=[pl.BlockSpec((nb, ts, GDN_WIDTH), tok)] * 4 + [
            pl.BlockSpec((nb, ts, GDN_HEADS * c), tok),
            pl.BlockSpec((nb, ts // c, GDN_HEADS, GDN_HEAD_DIM), lambda b, i: (b, i, 0, 0))],
        out_specs=pl.BlockSpec((nb, ts, GDN_WIDTH), tok),
        out_shape=jax.ShapeDtypeStruct((bsz, s, GDN_WIDTH), F32),
        scratch_shapes=[pltpu.VMEM((nb * GDN_HEADS, GDN_HEAD_DIM, GDN_HEAD_DIM), F32)],
        compiler_params=_cparams("parallel", "arbitrary"),
        name="gdn_scan",
    )(qd, kd, u, w, ic, gl)


def _even_post_kernel(o_ref, zs_ref, yb_ref, nw_ref, y_ref):
    hd = GDN_HEAD_DIM
    o = o_ref[0]
    zs = zs_ref[0].astype(F32)
    nw = nw_ref[...]
    for h in range(GDN_HEADS):
        cols = slice(h * hd, (h + 1) * hd)
        oh = o[:, cols]
        on = oh * lax.rsqrt(jnp.mean(oh * oh, -1, keepdims=True) + NORM_EPS) * nw
        y_ref[0, :, cols] = (on * zs[:, cols]).astype(BF16)
    y_ref[0, :, GDN_WIDTH:] = yb_ref[0]


def even_post(o, o1, norm_w):
    bsz, s, _ = o.shape
    tm = min(TOKEN_TILE, s)
    wb = 512
    blk = lambda j: pl.BlockSpec((1, tm, wb), lambda b, i, j=j: (b, i, j))
    const = lambda b, i: (0, 0)
    return pl.pallas_call(
        _even_post_kernel,
        grid=(bsz, s // tm),
        in_specs=[pl.BlockSpec((1, tm, GDN_WIDTH), lambda b, i: (b, i, 0)),
                  blk(3), blk(4),
                  pl.BlockSpec((1, GDN_HEAD_DIM), const)],
        out_specs=pl.BlockSpec((1, tm, GDN_WIDTH + SC_WIDTH), lambda b, i: (b, i, 0)),
        out_shape=jax.ShapeDtypeStruct((bsz, s, GDN_WIDTH + SC_WIDTH), BF16),
        compiler_params=_cparams("parallel", "parallel"),
        name="even_post",
    )(o, o1, o1, norm_w.astype(F32).reshape(1, -1))


def _layer_norm(r, g, b):
    mu = jnp.mean(r, -1, keepdims=True)
    rc = r - mu
    var = jnp.mean(rc * rc, -1, keepdims=True)
    return rc * lax.rsqrt(var + LN_EPS) * g + b


def _route(logits_t, bias):
    scores = _sigmoid(logits_t)
    biased = scores + bias
    t = logits_t.shape[1]
    epg = EXPERTS_PER_GROUP
    sub = lax.broadcasted_iota(jnp.int32, (epg, t), 0).astype(F32)
    best = None
    for g in range(N_GROUPS):
        bgp = biased[g * epg:(g + 1) * epg]
        m1 = jnp.max(bgp, axis=0, keepdims=True)
        i1 = jnp.min(jnp.where(bgp == m1, sub, float(epg)), axis=0, keepdims=True)
        rest = jnp.where(sub == i1, -jnp.inf, bgp)
        m2 = jnp.max(rest, axis=0, keepdims=True)
        i2 = jnp.min(jnp.where(rest == m2, sub, float(epg)), axis=0, keepdims=True)
        gs = m1 + m2
        if best is None:
            best, e0, e1 = gs, i1, i2
        else:
            better = gs > best
            best = jnp.where(better, gs, best)
            e0 = jnp.where(better, float(g * epg) + i1, e0)
            e1 = jnp.where(better, float(g * epg) + i2, e1)
    eio = lax.broadcasted_iota(jnp.int32, scores.shape, 0).astype(F32)
    hit0 = eio == e0
    hit1 = eio == e1
    s0 = jnp.sum(jnp.where(hit0, scores, 0.0), axis=0, keepdims=True)
    s1 = jnp.sum(jnp.where(hit1, scores, 0.0), axis=0, keepdims=True)
    tot = s0 + s1
    return e0.astype(jnp.int32), e1.astype(jnp.int32), s0 / tot, s1 / tot, hit0, hit1


def _mix_out_kernel(y_ref, wo_ref, x_ref, moda_ref, modb_ref, lng_ref, lnb_ref, rwt_ref, rb_ref,
                    upper_ref, x1_ref, hm_ref, seli_ref, selw_ref, cnt_ref):
    d = D_MODEL
    gate = moda_ref[0, 0][:, 2 * d:]
    modb = modb_ref[0, 0]
    y = _dot(y_ref[0], wo_ref[...])
    x1 = _layer_norm(DN_ALPHA * x_ref[0] + (1.0 + gate) * y, lng_ref[...], lnb_ref[...])
    x1_ref[0] = x1
    hm = x1 * (1.0 + modb[:, d:2 * d]) + modb[:, :d]
    hm_hi = hm.astype(BF16)
    hm_ref[0] = hm_hi

    hm_lo = (hm - hm_hi.astype(F32)).astype(BF16)
    rw = rwt_ref[...]
    hi_lo = _dot_nt(rw, hm_hi)
    logits_t = hi_lo[:N_EXPERTS] + hi_lo[N_EXPERTS:] + _dot_nt(rw[:N_EXPERTS], hm_lo)
    e0, e1, w0, w1, hit0, hit1 = _route(logits_t, rb_ref[...])
    member = jnp.where(hit0 | hit1, 1.0, 0.0)
    before = _dot(member.astype(BF16), upper_ref[...])
    rank0 = jnp.sum(jnp.where(hit0, before, 0.0), axis=0, keepdims=True).astype(jnp.int32)
    rank1 = jnp.sum(jnp.where(hit1, before, 0.0), axis=0, keepdims=True).astype(jnp.int32)
    zi = jnp.zeros_like(e0)
    seli_ref[0] = jnp.concatenate([e0, e1, rank0, rank1, zi, zi, zi, zi], axis=0)
    zf = jnp.zeros_like(w0)
    selw_ref[0] = jnp.concatenate([w0, w1, zf, zf, zf, zf, zf, zf], axis=0)
    cnt_ref[0, 0] = jnp.broadcast_to(jnp.sum(member, axis=1, keepdims=True), (N_EXPERTS, LANES))


def mix_out(ymix, w_out, x, moda, modb, ln_g, ln_b, router_w, router_bias, b0=0):
    bsz, s, d = ymix.shape
    tm = min(TOKEN_TILE, s)
    nt = s // tm
    r = np.arange(tm)
    upper = jnp.asarray((r[:, None] < r[None, :]).astype(np.float32), dtype=BF16)
    rw_f32 = router_w.T.astype(F32)
    rw_hi = rw_f32.astype(BF16)
    rwt = jnp.concatenate([rw_hi, (rw_f32 - rw_hi.astype(F32)).astype(BF16)], axis=0)
    tok = lambda b, i: (b, i, 0)
    const = lambda b, i: (0, 0)
    modspec = pl.BlockSpec((1, 1, 1, 3 * d), lambda b, i: (b, 0, 0, 0))
    row8 = pl.BlockSpec((1, 8, tm), lambda b, i: (b, 0, i))
    return pl.pallas_call(
        _mix_out_kernel,
        grid=(bsz, nt),
        in_specs=[pl.BlockSpec((1, tm, d), tok),
                  pl.BlockSpec((d, d), const),
                  pl.BlockSpec((1, tm, d), lambda b, i: (b + b0, i, 0)),
                  modspec, modspec,
                  pl.BlockSpec((1, d), const), pl.BlockSpec((1, d), const),
                  pl.BlockSpec((2 * N_EXPERTS, d), const),
                  pl.BlockSpec((N_EXPERTS, 1), const),
                  pl.BlockSpec((tm, tm), const)],
        out_specs=[pl.BlockSpec((1, tm, d), tok),
                   pl.BlockSpec((1, tm, d), tok),
                   row8, row8,
                   pl.BlockSpec((1, 1, N_EXPERTS, LANES), lambda b, i: (b, i, 0, 0))],
        out_shape=[jax.ShapeDtypeStruct((bsz, s, d), F32),
                   jax.ShapeDtypeStruct((bsz, s, d), BF16),
                   jax.ShapeDtypeStruct((bsz, 8, s), jnp.int32),
                   jax.ShapeDtypeStruct((bsz, 8, s), F32),
                   jax.ShapeDtypeStruct((bsz, nt, N_EXPERTS, LANES), F32)],
        compiler_params=_cparams("parallel", "parallel"),
        name="mix_out",
    )(ymix, w_out.astype(BF16), x, moda, modb, ln_g.reshape(1, d), ln_b.reshape(1, d),
      rwt, router_bias.astype(F32).reshape(N_EXPERTS, 1), upper)


def _experts_kernel(be_ref, next_ref, nused_ref, xs_ref, wg_hbm, wu_hbm, wd_hbm, ys_ref,
                    wgf_ref, wuf_ref, wdf_ref, wgb_ref, wub_ref, wdb_ref, sem_ref, slot_ref, *, layer):
    i = pl.program_id(0)

    def weight_copies(e, slot):
        return [pltpu.make_async_copy(src.at[layer, e], dst.at[slot], sem_ref.at[n, slot])
                for n, (src, dst) in enumerate(((wg_hbm, wgf_ref), (wu_hbm, wuf_ref), (wd_hbm, wdf_ref)))]

    @pl.when(i == 0)
    def _():
        slot_ref[0] = 0
        for cp in weight_copies(be_ref[0], 0):
            cp.start()

    @pl.when((i == 0) | (be_ref[i] != be_ref[jnp.maximum(i - 1, 0)]))
    def _():
        slot = slot_ref[0]
        for cp in weight_copies(be_ref[i], slot):
            cp.wait()

        @pl.when(next_ref[i] >= 0)
        def _():
            for cp in weight_copies(next_ref[i], 1 - slot):
                cp.start()

        wgb_ref[...] = wgf_ref[slot].astype(BF16)
        wub_ref[...] = wuf_ref[slot].astype(BF16)
        wdb_ref[...] = wdf_ref[slot].astype(BF16)
        slot_ref[0] = 1 - slot

    @pl.when(i < nused_ref[0])
    def _():
        x = xs_ref[...]
        act =(_silu(_dot(x, wgb_ref[...])) * _dot(x, wub_ref[...])).astype(BF16)
        ys_ref[...] = _dot(act, wdb_ref[...]).astype(BF16)

    @pl.when(i >= nused_ref[0])
    def _():
        ys_ref[...] = jnp.zeros_like(ys_ref)


def moe_experts(xs, block_expert, n_used, w_gate, w_up, w_down, layer):
    rows, d = xs.shape
    nb = rows // MOE_BLOCK
    later = jnp.where(block_expert[None, :] > block_expert[:, None], block_expert[None, :], N_EXPERTS)
    next_expert = jnp.min(later, axis=1)
    next_expert = jnp.where(next_expert == N_EXPERTS, -1, next_expert).astype(jnp.int32)
    hbm = pl.BlockSpec(memory_space=pl.ANY)
    rows_spec = pl.BlockSpec((MOE_BLOCK, d), lambda i, be, nx, nu: (i, 0))
    return pl.pallas_call(
        functools.partial(_experts_kernel, layer=layer),
        grid_spec=pltpu.PrefetchScalarGridSpec(
            num_scalar_prefetch=3,
            grid=(nb,),
            in_specs=[rows_spec, hbm, hbm, hbm],
            out_specs=rows_spec,
            scratch_shapes=[pltpu.VMEM((2, d, D_EXPERT), F32), pltpu.VMEM((2, d, D_EXPERT), F32),
                            pltpu.VMEM((2, D_EXPERT, d), F32),
                            pltpu.VMEM((d, D_EXPERT), BF16), pltpu.VMEM((d, D_EXPERT), BF16),
                            pltpu.VMEM((D_EXPERT, d), BF16),
                            pltpu.SemaphoreType.DMA((3, 2)), pltpu.SMEM((1,), jnp.int32)]),
        out_shape=jax.ShapeDtypeStruct((rows, d), BF16),
        compiler_params=_cparams("arbitrary"),
        name="moe_experts",
    )(block_expert, next_expert, n_used, xs, w_gate, w_up, w_down)


def _plan_kernel(seli_ref, base_ref, dest_ref):
    sel = seli_ref[0]
    base = base_ref[0, 0]
    eio = lax.broadcasted_iota(jnp.int32, (N_EXPERTS, sel.shape[1]), 0)
    out = []
    for kk in range(2):
        first = jnp.sum(jnp.where(eio == sel[kk:kk + 1], base, 0.0), axis=0, keepdims=True)
        out.append(first.astype(jnp.int32) + sel[2 + kk:3 + kk])
    zero = jnp.zeros_like(out[0])
    dest_ref[0] = jnp.concatenate(out + [zero] * 6, axis=0)


def dispatch_plan(seli, base):
    bsz, _, s = seli.shape
    nt = base.shape[1]
    tm = s // nt
    return pl.pallas_call(
        _plan_kernel,
        grid=(bsz, nt),
        in_specs=[pl.BlockSpec((1, 8, tm), lambda b, i: (b, 0, i)),
                  pl.BlockSpec((1, 1, N_EXPERTS, 1), lambda b, i: (b, i, 0, 0))],
        out_specs=pl.BlockSpec((1, 8, tm), lambda b, i: (b, 0, i)),
        out_shape=jax.ShapeDtypeStruct((bsz, 8, s), jnp.int32),
        compiler_params=_cparams("parallel", "parallel"),
        name="dispatch_plan",
    )(seli, base)


def _invert_kernel(dest_ref, pad_lo_ref, pad_hi_ref, slot_tok_ref):
    t = dest_ref.shape[0] // 2

    def clear_range(e, carry):
        hi = pad_hi_ref[e]
        n = (hi - pad_lo_ref[e] + CLEAR_CHUNK - 1) // CLEAR_CHUNK

        def clear(q, c):
            r0 = hi - (q + 1) * CLEAR_CHUNK
            for j in range(CLEAR_CHUNK):
                slot_tok_ref[r0 + j] = lax.rem(r0 + j, t)
            return c
        return lax.fori_loop(0, n, clear, carry)

    def put(a, carry):
        slot_tok_ref[dest_ref[a]] = a
        slot_tok_ref[dest_ref[t + a]] = a
        return carry

    lax.fori_loop(0, pad_lo_ref.shape[0], clear_range, 0)
    lax.fori_loop(0, t, put, 0, unroll=8)


def invert_slots(dest_flat, pad_lo, pad_hi, rows):
    smem = pl.BlockSpec(memory_space=pltpu.SMEM)
    return pl.pallas_call(
        _invert_kernel,
        in_specs=[smem, smem, smem],
        out_specs=smem,
        out_shape=jax.ShapeDtypeStruct((rows,), jnp.int32),
        name="invert_slots",
    )(dest_flat, pad_lo, pad_hi)


def _moe_combine_kernel(g0_ref, g1_ref, wt_ref, x_ref, mod_ref, lng_ref, lnb_ref, o_ref):
    d = D_MODEL
    gate = mod_ref[0, 0][:, 2 * d:]
    wt = wt_ref[0]
    y = g0_ref[0, 0].astype(F32) * wt[:, 0:1] + g1_ref[0, 0].astype(F32) * wt[:, 1:2]
    o_ref[0] = _layer_norm(DN_ALPHA * x_ref[0] + (1.0 + gate) * y, lng_ref[...], lnb_ref[...])


def moe_combine(g, wt, x1, modb, ln_g, ln_b):
    bsz, s, d = x1.shape
    tm = min(TOKEN_TILE, s)
    tok = lambda b, i: (b, i, 0)
    const = lambda b, i: (0, 0)
    return pl.pallas_call(
        _moe_combine_kernel,
        grid=(bsz, s // tm),
        in_specs=[pl.BlockSpec((1, 1, tm, d), lambda b, i: (0, b, i, 0)),
                  pl.BlockSpec((1, 1, tm, d), lambda b, i: (1, b, i, 0)),
                  pl.BlockSpec((1, tm, 8), tok), pl.BlockSpec((1, tm, d), tok),
                  pl.BlockSpec((1, 1, 1, 3 * d), lambda b, i: (b, 0, 0, 0)),
                  pl.BlockSpec((1, d), const), pl.BlockSpec((1, d), const)],
        out_specs=pl.BlockSpec((1, tm, d), tok),
        out_shape=jax.ShapeDtypeStruct((bsz, s, d), F32),
        compiler_params=_cparams("parallel", "parallel"),
        name="moe_combine",
    )(g, g, wt, x1, modb, ln_g.reshape(1, d), ln_b.reshape(1, d))


def moe_layer(x1, hm, seli, selw, cnt, modb, ln_g, ln_b, w_gate, w_up, w_down, layer):
    bsz, s, d = x1.shape
    t = bsz * s
    tm = min(TOKEN_TILE, s)
    blk = MOE_BLOCK
    cnt = cnt[..., 0].reshape(-1, N_EXPERTS).astype(jnp.int32)
    tile_off = jnp.cumsum(cnt, axis=0) - cnt
    counts = cnt.sum(0)
    padded = (counts + blk - 1) // blk * blk
    pend = jnp.cumsum(padded)
    pstart = pend - padded
    nb = -(-(2 * t) // blk) + N_EXPERTS
    rows = nb * blk
    base = (pstart[None, :] + tile_off).astype(F32).reshape(bsz, s // tm, N_EXPERTS, 1)
    dest = dispatch_plan(seli, base)
    dest = jnp.swapaxes(dest[:, 0:2, :], 0, 1).reshape(2 * t)
    pad_lo = jnp.concatenate([pstart + counts, pend[-1:]]).astype(jnp.int32)
    pad_hi = jnp.concatenate([pend, jnp.full((1,), rows)]).astype(jnp.int32)
    slot_tok = invert_slots(dest, pad_lo, pad_hi, rows)
    starts = jnp.arange(nb, dtype=jnp.int32) * blk
    block_expert = jnp.minimum(jnp.sum((pend[None, :] <= starts[:, None]).astype(jnp.int32), axis=1),
                               N_EXPERTS - 1)
    n_used = (pend[-1] // blk).astype(jnp.int32).reshape(1)
    xs = hm.reshape(t, d).at[slot_tok].get(mode="promise_in_bounds")
    ys = moe_experts(xs, block_expert, n_used, w_gate, w_up, w_down, layer)
    g = ys.at[dest].get(mode="promise_in_bounds").reshape(2, bsz, s, d)
    wt = jnp.swapaxes(selw, 1, 2)
    return moe_combine(g, wt, x1, modb, ln_g, ln_b)


MLA_IN_COLS = Q_LORA + KV_LORA + 2 * LANES
MLA_QK = 2 * LANES


def _mla_in_kernel(x_ref, mod_ref, pos_ref, invf_ref, win_ref, qn_ref, kvn_ref, wq_ref, wkv_ref,
                   q_ref, k_ref, v_ref):
    d = D_MODEL
    nh = MLA_HEADS
    mod = mod_ref[0, 0]
    h = (x_ref[0] * (1.0 + mod[:, d:2 * d]) + mod[:, :d]).astype(BF16)
    proj = _dot(h, win_ref[...])
    ang = pos_ref[0].astype(F32) * invf_ref[...]
    lane = lax.broadcasted_iota(jnp.int32, ang.shape, 1)
    cos = jnp.where(lane < QK_ROPE, jnp.cos(ang), 0.0)
    sin = jnp.where(lane < QK_ROPE, jnp.sin(ang), 0.0)

    qa = proj[:, :Q_LORA]
    qa = (qa * lax.rsqrt(jnp.mean(qa * qa, -1, keepdims=True) + NORM_EPS) * qn_ref[...]).astype(BF16)
    kva = proj[:, Q_LORA:Q_LORA + KV_LORA]
    kva = (kva * lax.rsqrt(jnp.mean(kva * kva, -1, keepdims=True) + NORM_EPS) * kvn_ref[...]).astype(BF16)
    kr0 = Q_LORA + KV_LORA
    k_rope = (proj[:, kr0:kr0 + LANES] * cos + proj[:, kr0 + LANES:kr0 + 2 * LANES] * sin).astype(BF16)

    scale = (QK_NOPE + QK_ROPE) ** -0.5 * LOG2_E
    hw = nh * LANES
    q_nope = _dot(qa, wq_ref[:, :hw])
    q_rope = _dot(qa, wq_ref[:, hw:2 * hw])
    q_rot = _dot(qa, wq_ref[:, 2 * hw:])
    k_nope = _dot(kva, wkv_ref[:, :hw])
    v_ref[0] = _dot(kva, wkv_ref[:, hw:]).astype(BF16)
    for hh in range(nh):
        cols = slice(hh * LANES, (hh + 1) * LANES)
        q_ref[0, :, hh * MLA_QK:hh * MLA_QK + LANES] = (q_nope[:, cols] * scale).astype(BF16)
        q_ref[0, :, hh * MLA_QK + LANES:(hh + 1) * MLA_QK] = (
            (q_rope[:, cols] * cos + q_rot[:, cols] * sin) * scale).astype(BF16)
        k_ref[0, :, hh * MLA_QK:hh * MLA_QK + LANES] = k_nope[:, cols].astype(BF16)
        k_ref[0, :, hh * MLA_QK + LANES:(hh + 1) * MLA_QK] = k_rope


def _rope_cols(w):
    half = QK_ROPE // 2
    pad = [(0, 0)] * (w.ndim - 1) + [(0, LANES - QK_ROPE)]
    rot = jnp.concatenate([-w[..., half:], w[..., :half]], axis=-1)
    return jnp.pad(w, pad), jnp.pad(rot, pad)


def mla_in(x, mod, positions, w_in, q_a_norm, w_q_b, kv_a_norm, w_kv_b):
    bsz, s, d = x.shape
    tm = min(TOKEN_TILE, s)
    nh = MLA_HEADS
    kr, kr_rot = _rope_cols(w_in[:, Q_LORA + KV_LORA:])
    win = jnp.concatenate([w_in[:, :Q_LORA + KV_LORA], kr, kr_rot], axis=1).astype(BF16)
    wq = w_q_b.reshape(Q_LORA, nh, QK_NOPE + QK_ROPE)
    qr, qr_rot = _rope_cols(wq[..., QK_NOPE:])
    wq = jnp.concatenate([wq[..., :QK_NOPE].reshape(Q_LORA, -1), qr.reshape(Q_LORA, -1),
                          qr_rot.reshape(Q_LORA, -1)], axis=1).astype(BF16)
    wkv = w_kv_b.reshape(KV_LORA, nh, QK_NOPE + V_DIM)
    wkv = jnp.concatenate([wkv[..., :QK_NOPE].reshape(KV_LORA, -1),
                           wkv[..., QK_NOPE:].reshape(KV_LORA, -1)], axis=1).astype(BF16)
    inv = ROPE_THETA ** (-np.arange(0, QK_ROPE, 2, dtype=np.float32) / QK_ROPE)
    invf = np.zeros((1, LANES), np.float32)
    invf[0, :QK_ROPE] = np.concatenate([inv, inv])
    tok = lambda b, i: (b, i, 0)
    const = lambda b, i: (0, 0)
    return pl.pallas_call(
        _mla_in_kernel,
        grid=(bsz, s // tm),
        in_specs=[pl.BlockSpec((1, tm, d), tok),
                  pl.BlockSpec((1, 1, 1, 3 * d), lambda b, i: (b, 0, 0, 0)),
                  pl.BlockSpec((1, tm, 1), tok),
                  pl.BlockSpec((1, LANES), const),
                  pl.BlockSpec((d, MLA_IN_COLS), const),
                  pl.BlockSpec((1, Q_LORA), const),
                  pl.BlockSpec((1, KV_LORA), const),
                  pl.BlockSpec((Q_LORA, 3 * nh * LANES), const),
                  pl.BlockSpec((KV_LORA, 2 * nh * LANES), const)],
        out_specs=[pl.BlockSpec((1, tm, nh * MLA_QK), tok),
                   pl.BlockSpec((1, tm, nh * MLA_QK), tok),
                   pl.BlockSpec((1, tm, nh * V_DIM), tok)],
        out_shape=[jax.ShapeDtypeStruct((bsz, s, nh * MLA_QK), BF16),
                   jax.ShapeDtypeStruct((bsz, s, nh * MLA_QK), BF16),
                   jax.ShapeDtypeStruct((bsz, s, nh * V_DIM), BF16)],
        compiler_params=_cparams("parallel", "parallel"),
        name="mla_in",
    )(x, mod, positions.reshape(bsz, s, 1), jnp.asarray(invf), win,
      q_a_norm.astype(F32).reshape(1, -1), kv_a_norm.astype(F32).reshape(1, -1), wq, wkv)


def _attn_kernel(q_ref, k_ref, v_ref, o_ref, m_ref, l_ref, acc_ref):
    tq = q_ref.shape[1]
    i = pl.program_id(2)
    m_ref[...] = jnp.full(m_ref.shape, -jnp.inf, F32)
    l_ref[...] = jnp.zeros(l_ref.shape, F32)
    acc_ref[...] = jnp.zeros(acc_ref.shape, F32)

    groups = [slice(g * ATTN_ROWS, (g + 1) * ATTN_ROWS) for g in range(tq // ATTN_ROWS)]

    def update(off, widths, masked):
        scs = []
        for rows, width in zip(groups, widths):
            sc = _dot_nt(q_ref[0, rows, :], k_ref[0, pl.ds(off, width), :])
            if masked:
                qi = lax.broadcasted_iota(jnp.int32, sc.shape, 0) + rows.start
                ki = lax.broadcasted_iota(jnp.int32, sc.shape, 1)
                sc = jnp.where(ki <= qi, sc, -jnp.inf)
            scs.append(sc)
        m_olds = [m_ref[rows] for rows in groups]
        m_news = [jnp.maximum(m_old, jnp.max(sc, -1, keepdims=True)) for m_old, sc in zip(m_olds, scs)]
        ps = [jnp.exp2(sc - jnp.concatenate([m_new] * (width // LANES), axis=1))
              for sc, m_new, width in zip(scs, m_news, widths)]
        alphas = [jnp.exp2(m_old - m_new) for m_old, m_new in zip(m_olds, m_news)]
        pvs = [_dot(p.astype(BF16), v_ref[0, pl.ds(off, width), :]) for p, width in zip(ps, widths)]
        for rows, m_new, alpha, p, pv in zip(groups, m_news, alphas, ps, pvs):
            l_ref[rows] = alpha * l_ref[rows] + jnp.sum(p, -1, keepdims=True)
            acc_ref[rows] = alpha * acc_ref[rows] + pv
            m_ref[rows] = m_new

    def full_block(j, carry):
        update(pl.multiple_of(j * tq, tq), [tq] * len(groups), False)
        return carry

    lax.fori_loop(0, i, full_block, 0)
    update(pl.multiple_of(i * tq, tq), [rows.stop for rows in groups], True)
    o_ref[0] = (acc_ref[...] / l_ref[...]).astype(BF16)


def attention(q, k, v):
    bsz, s, _ = q.shape
    tq = min(ATTN_TILE, s)
    return pl.pallas_call(
        _attn_kernel,
        grid=(bsz, MLA_HEADS, s // tq),
        in_specs=[pl.BlockSpec((1, tq, MLA_QK), lambda b, h, i: (b, i, h)),
                  pl.BlockSpec((1, s, MLA_QK), lambda b, h, i: (b, 0, h)),
                  pl.BlockSpec((1, s, V_DIM), lambda b, h, i: (b, 0, h))],
        out_specs=pl.BlockSpec((1, tq, V_DIM), lambda b, h, i: (b, i, h)),
        out_shape=jax.ShapeDtypeStruct((bsz, s, MLA_HEADS * V_DIM), BF16),
        scratch_shapes=[pltpu.VMEM((tq, LANES), F32), pltpu.VMEM((tq, LANES), F32), pltpu.VMEM((tq, V_DIM), F32)],
        compiler_params=_cparams("parallel", "parallel", "arbitrary"),
        name="attention",
    )(q, k, v)


def kernel(x, c, positions, ada_w, ada_b, ln_g, ln_b, w_in_e, gdn_conv_w, gdn_a_log, gdn_dt_bias,
           gdn_norm_w, sc_conv_w, w_out_e, w_in_o, q_a_norm, w_q_b, kv_a_norm, w_kv_b, w_out_o,
           router_w, router_bias, w_gate, w_up, w_down):
    bsz, s, d = x.shape
    depth = ada_w.shape[0]
    mod = ada_mod(c, ada_w, ada_b).reshape(depth, 2, bsz, 1, 3 * d)

    def layers(x, b0, mod, positions):
        for i in range(depth):
            j = i // 2
            moda, modb = mod[i, 0][:, None], mod[i, 1][:, None]
            if i % 2 == 0:
                o1, bg, bgt = even_in_proj(x, moda, w_in_e[j], gdn_a_log[j], gdn_dt_bias[j],
                                           gdn_conv_w[j], sc_conv_w[j], b0)
                qd, kd, u, w, ic, gl = gdn_prep(o1, bg, bgt)
                o = gdn_scan(qd, kd, u, w, ic, gl)
                ymix = even_post(o, o1, gdn_norm_w[j])
                w_out = w_out_e[j]
            else:
                q, k, v = mla_in(x, moda, positions, w_in_o[j], q_a_norm[j], w_q_b[j], kv_a_norm[j],
                                 w_kv_b[j])
                ymix = attention(q, k, v)
                w_out = w_out_o[j]
            x1, hm, seli, selw, cnt = mix_out(ymix, w_out, x, moda, modb, ln_g[i, 0], ln_b[i, 0],
                                              router_w, router_bias, b0)
            x = moe_layer(x1, hm, seli, selw, cnt, modb, ln_g[i, 1], ln_b[i, 1], w_gate, w_up, w_down, i)
            b0 = 0
        return x

    ng = BATCH_GROUPS if bsz % BATCH_GROUPS == 0 else 1
    gb = bsz // ng
    outs = [layers(x, g * gb, mod[:, :, g * gb:(g + 1) * gb], positions[g * gb:(g + 1) * gb])
            for g in range(ng)]
    return outs[0] if ng == 1 else jnp.concatenate(outs, axis=0)
```

```python
import functools

import numpy as np
import jax
import jax.numpy as jnp
from jax import lax
from jax.experimental import pallas as pl
from jax.experimental.pallas import tpu as pltpu

F32 = jnp.float32
BF16 = jnp.bfloat16
HIGHEST = lax.Precision.HIGHEST

D_MODEL = 1024
DEPTH = 2
DN_ALPHA = (2.0 * DEPTH) ** 0.25

GDN_HEADS = 4
GDN_HEAD_DIM = 128
GDN_WIDTH = GDN_HEADS * GDN_HEAD_DIM
GDN_CONV = 4
GDN_CHUNK = 64
SC_WIDTH = 512
SC_CONV = 3

MLA_HEADS = 8
Q_LORA = 384
KV_LORA = 256
QK_NOPE = 128
QK_ROPE = 64
V_DIM = 128
ROPE_THETA = 10000.0

N_EXPERTS = 32
N_GROUPS = 4
EXPERTS_PER_GROUP = N_EXPERTS // N_GROUPS
D_EXPERT = 512

NORM_EPS = 1e-6
LN_EPS = 1e-5

LANES = 128
HALO = 16
TOKEN_TILE = 512
GDN_TILE = 256
ATTN_TILE = 1024
ATTN_ROWS = 256
LOG2_E = 1.4426950408889634
MOE_BLOCK = 512
SCAN_BATCH = 4
BATCH_GROUPS = 1
CLEAR_CHUNK = 8
VMEM_LIMIT = 48 * 1024 * 1024


def _cparams(*sem):
    return pltpu.CompilerParams(dimension_semantics=sem, vmem_limit_bytes=VMEM_LIMIT)


def _sigmoid(x):
    return 1.0 / (1.0 + jnp.exp(-x))


def _silu(x):
    return x * _sigmoid(x)


def _dot(a, b):
    return jnp.dot(a, b, preferred_element_type=F32)


def _dot_nt(a, b, precision=None):
    return lax.dot_general(a, b, (((1,), (1,)), ((), ())), precision=precision,
                           preferred_element_type=F32)


def _dot_tn(a, b):
    return lax.dot_general(a, b, (((0,), (0,)), ((), ())), preferred_element_type=F32)


def _ada_kernel(c_ref, w_ref, b_ref, o_ref):
    cond = _silu(c_ref[...])
    o_ref[0] = jnp.dot(cond, w_ref[0], precision=HIGHEST, preferred_element_type=F32) + b_ref[0]


def ada_mod(c, ada_w, ada_b):
    nl = ada_w.shape[0] * ada_w.shape[1]
    bsz, d = c.shape
    w = ada_w.reshape(nl, d, 3 * d)
    b = ada_b.reshape(nl, 1, 3 * d)
    return pl.pallas_call(
        _ada_kernel,
        grid=(nl, 3),
        in_specs=[pl.BlockSpec((bsz, d), lambda l, j: (0, 0)),
                  pl.BlockSpec((1, d, d), lambda l, j: (l, 0, j)),
                  pl.BlockSpec((1, 1, d), lambda l, j: (l, 0, j))],
        out_specs=pl.BlockSpec((1, bsz, d), lambda l, j: (l, 0, j)),
        out_shape=jax.ShapeDtypeStruct((nl, bsz, 3 * d), F32),
        compiler_params=_cparams("parallel", "parallel"),
        name="ada_mod",
    )(c, w, b)


EVEN_MAIN = 3 * GDN_WIDTH + GDN_WIDTH + 3 * SC_WIDTH
EVEN_OUT = 3 * GDN_WIDTH + GDN_WIDTH + SC_WIDTH


def _causal_conv(x, halo, w, taps):
    rows = x.shape[0]
    nh = halo.shape[0]
    xf = jnp.concatenate([halo, x], axis=0)
    y = w[taps - 1:taps] * x
    for j in range(taps - 1):
        off = nh - (taps - 1) + j
        y = y + w[j:j + 1] * xf[off:off + rows]
    return y


def _even_in_kernel(x_ref, mod_ref, w1_ref, w2_ref, alog_ref, dtb_ref, gcw_ref, scw_ref,
                    o1_ref, o2_ref, o2t_ref, qkv_tail_ref, ch_tail_ref):
    d = D_MODEL
    cw = 512
    first = pl.program_id(1) == 0
    mod = mod_ref[0, 0]
    h = (x_ref[0] * (1.0 + mod[:, d:2 * d]) + mod[:, :d]).astype(BF16)
    tm = h.shape[0]
    gcw = gcw_ref[...]

    def project(j):
        return _dot(h, w1_ref[:, j * cw:(j + 1) * cw])

    def finish_qkv(j, r):
        cols = slice(j * cw, (j + 1) * cw)
        halo = jnp.where(first, 0.0, qkv_tail_ref[:, cols])
        o1_ref[0, :, cols] = _silu(_causal_conv(r, halo, gcw[:, cols], GDN_CONV)).astype(BF16)
        qkv_tail_ref[:, cols] = r[tm - HALO:]

    r0 = project(0)
    r1 = project(1)
    finish_qkv(0, r0)
    r2 = project(2)
    finish_qkv(1, r1)
    z = project(3)
    finish_qkv(2, r2)
    sc_c = project(5)
    o1_ref[0, :, 3 * cw:4 * cw] = _silu(z).astype(BF16)
    sc_h = project(6)
    sc_b = project(4)
    ch = sc_c * sc_h
    halo = jnp.where(first, 0.0, ch_tail_ref[...])
    r = _dot(h, w2_ref[...])
    o1_ref[0, :, 4 * cw:5 * cw] = (sc_b * _causal_conv(ch, halo, scw_ref[...], SC_CONV)).astype(BF16)
    ch_tail_ref[...] = ch[tm - HALO:]
    lane = lax.broadcasted_iota(jnp.int32, r.shape, 1)
    a = r + dtb_ref[...]
    softplus = jnp.maximum(a, 0.0) + jnp.log(1.0 + jnp.exp(-jnp.abs(a)))
    bg = jnp.where(lane < GDN_HEADS, _sigmoid(r), -jnp.exp(alog_ref[...]) * softplus)
    o2_ref[0] = bg
    o2t_ref[0] = bg.T[:8]


def even_in_proj(x, mod, w_in, a_log, dt_bias, gdn_conv_w, sc_conv_w, b0=0):
    _, s, d = x.shape
    bsz = mod.shape[0]
    tm = min(TOKEN_TILE, s)
    q_end = 4 * GDN_WIDTH
    w1 = jnp.concatenate([w_in[:, :q_end], w_in[:, q_end + 2 * GDN_HEADS:]], axis=1).astype(BF16)
    w2 = jnp.pad(w_in[:, q_end:q_end + 2 * GDN_HEADS], ((0, 0), (0, LANES - 2 * GDN_HEADS))).astype(BF16)
    alog = jnp.pad(a_log.astype(F32), (GDN_HEADS, LANES - 2 * GDN_HEADS)).reshape(1, LANES)
    dtb = jnp.pad(dt_bias.astype(F32), (GDN_HEADS, LANES - 2 * GDN_HEADS)).reshape(1, LANES)
    const = lambda b, i: (0, 0)
    return pl.pallas_call(
        _even_in_kernel,
        grid=(bsz, s // tm),
        in_specs=[pl.BlockSpec((1, tm, d), lambda b, i: (b + b0, i, 0)),
                  pl.BlockSpec((1, 1, 1, 3 * d), lambda b, i: (b, 0, 0, 0)),
                  pl.BlockSpec((d, EVEN_MAIN), const),
                  pl.BlockSpec((d, LANES), const),
                  pl.BlockSpec((1, LANES), const),
                  pl.BlockSpec((1, LANES), const),
                  pl.BlockSpec((GDN_CONV, 3 * GDN_WIDTH), const),
                  pl.BlockSpec((SC_CONV, SC_WIDTH), const)],
        out_specs=[pl.BlockSpec((1, tm, EVEN_OUT), lambda b, i: (b, i, 0)),
                   pl.BlockSpec((1, tm, LANES), lambda b, i: (b, i, 0)),
                   pl.BlockSpec((1, 8, tm), lambda b, i: (b, 0, i))],
        out_shape=[jax.ShapeDtypeStruct((bsz, s, EVEN_OUT), BF16),
                   jax.ShapeDtypeStruct((bsz, s, LANES), F32),
                   jax.ShapeDtypeStruct((bsz, 8, s), F32)],
        scratch_shapes=[pltpu.VMEM((HALO, 3 * GDN_WIDTH), F32), pltpu.VMEM((HALO, SC_WIDTH), F32)],
        compiler_params=_cparams("parallel", "arbitrary"),
        name="even_in_proj",
    )(x, mod, w1, w2, alog, dtb, gdn_conv_w.astype(F32), sc_conv_w.astype(F32))


def _inv_unit_lower(lows, xor_ij, block):
    eye = jnp.where(xor_ij == 0, 1.0, 0.0)
    ms = [eye - jnp.where(xor_ij == 1, low, 0.0) for low in lows]
    s = 2
    while s < block:
        level = (xor_ij >> (s.bit_length() - 1)) == 1
        mbs = [m.astype(BF16) for m in ms]
        cms = [_dot(jnp.where(level, low, 0.0).astype(BF16), mb).astype(BF16) for low, mb in zip(lows, mbs)]
        ms = [m - _dot(mb, cm) for m, mb, cm in zip(ms, mbs, cms)]
        s *= 2
    return ms


def _gdn_prep_kernel(qkv_ref, bg_ref, bgt_ref, ltri_ref, utri_ref, same_ref,
                     qd_ref, kd_ref, u_ref, w_ref, ic_ref, gl_ref):
    ts = qkv_ref.shape[1]
    c = GDN_CHUNK
    hd = GDN_HEAD_DIM
    y = qkv_ref[0].astype(F32)

    bg = bg_ref[0]
    gc_col = jnp.dot(ltri_ref[...], bg, precision=HIGHEST, preferred_element_type=F32)
    gc_row = jnp.dot(bgt_ref[0], utri_ref[...], precision=HIGHEST, preferred_element_type=F32)
    gc_end = jnp.dot(same_ref[...], bg, precision=HIGHEST, preferred_element_type=F32)

    ii = lax.broadcasted_iota(jnp.int32, (ts, ts), 0)
    jj = lax.broadcasted_iota(jnp.int32, (ts, ts), 1)
    xor_ij = ii ^ jj
    causal = (same_ref[...] > 0.0) & (ii >= jj)
    diag = xor_ij == 0

    heads = range(GDN_HEADS)
    gcols = [gc_col[:, GDN_HEADS + h:GDN_HEADS + h + 1] for h in heads]
    gends = [gc_end[:, GDN_HEADS + h:GDN_HEADS + h + 1] for h in heads]
    egcs = [jnp.exp(g) for g in gcols]
    lows, intras, rhss, qs, ks = [], [], [], [], []
    for h in heads:
        q = y[:, h * hd:(h + 1) * hd]
        k = y[:, GDN_WIDTH + h * hd:GDN_WIDTH + (h + 1) * hd]
        v = y[:, 2 * GDN_WIDTH + h * hd:2 * GDN_WIDTH + (h + 1) * hd]
        q = q * lax.rsqrt(jnp.sum(q * q, -1, keepdims=True) + NORM_EPS) * (hd ** -0.5)
        k = k * lax.rsqrt(jnp.sum(k * k, -1, keepdims=True) + NORM_EPS)
        beta = bg[:, h:h + 1]
        grow = gc_row[GDN_HEADS + h:GDN_HEADS + h + 1, :]
        decay = jnp.exp(jnp.where(causal, gcols[h] - grow, -jnp.inf))
        kb = k * beta
        kbf = k.astype(BF16)
        lows.append(jnp.where(diag, 0.0, _dot_nt(kb.astype(BF16), kbf) * decay))
        intras.append(_dot_nt(q.astype(BF16), kbf) * decay)
        rhss.append(jnp.concatenate([v * beta, kb * egcs[h]], axis=-1).astype(BF16))
        qs.append(q)
        ks.append(k)

    invs = _inv_unit_lower(lows, xor_ij, c)
    for h in heads:
        sol = _dot(invs[h].astype(BF16), rhss[h])
        cols = slice(h * hd, (h + 1) * hd)
        u_ref[0, :, cols] = sol[:, :hd]
        w_ref[0, :, cols] = sol[:, hd:].astype(BF16)
        qd_ref[0, :, cols] = (qs[h] * egcs[h]).astype(BF16)
        kd_ref[0, :, cols] = (ks[h] * jnp.exp(gends[h] - gcols[h])).astype(BF16)
        packed = intras[h][:, :c]
        for n in range(1, ts // c):
            packed = packed + intras[h][:, n * c:(n + 1) * c]
        ic_ref[0, :, h * c:(h + 1) * c] = packed.astype(BF16)
        for n in range(ts // c):
            gl_ref[0, n, h:h + 1, :] = jnp.broadcast_to(jnp.exp(gends[h][n * c:n * c + 1]), (1, hd))


def gdn_prep(o1, bg, bgt):
    bsz, s, _ = o1.shape
    ts = min(GDN_TILE, s)
    c = GDN_CHUNK
    nc = ts // c
    qkv_w = 3 * GDN_WIDTH
    r = np.arange(ts)
    same = (r[:, None] // c) == (r[None, :] // c)
    ltri = jnp.asarray((same & (r[:, None] >= r[None, :])).astype(np.float32))
    utri = jnp.asarray((same & (r[:, None] <= r[None, :])).astype(np.float32))
    same = jnp.asarray(same.astype(np.float32))
    tok = lambda b, i: (b, i, 0)
    const = lambda b, i: (0, 0)
    wide = jax.ShapeDtypeStruct((bsz, s, GDN_WIDTH), BF16)
    return pl.pallas_call(
        _gdn_prep_kernel,
        grid=(bsz, s // ts),
        in_specs=[pl.BlockSpec((1, ts, qkv_w), tok),
                  pl.BlockSpec((1, ts, LANES), tok),
                  pl.BlockSpec((1, 8, ts), lambda b, i: (b, 0, i)),
                  pl.BlockSpec((ts, ts), const),
                  pl.BlockSpec((ts, ts), const),
                  pl.BlockSpec((ts, ts), const)],
        out_specs=[pl.BlockSpec((1, ts, GDN_WIDTH), tok),
                   pl.BlockSpec((1, ts, GDN_WIDTH), tok),
                   pl.BlockSpec((1, ts, GDN_WIDTH), tok),
                   pl.BlockSpec((1, ts, GDN_WIDTH), tok),
                   pl.BlockSpec((1, ts, GDN_HEADS * c), tok),
                   pl.BlockSpec((1, nc, GDN_HEADS, GDN_HEAD_DIM), lambda b, i: (b, i, 0, 0))],
        out_shape=[wide, wide, jax.ShapeDtypeStruct((bsz, s, GDN_WIDTH), F32), wide,
                   jax.ShapeDtypeStruct((bsz, s, GDN_HEADS * c), BF16),
                   jax.ShapeDtypeStruct((bsz, s // c, GDN_HEADS, GDN_HEAD_DIM), F32)],
        compiler_params=_cparams("parallel", "parallel"),
        name="gdn_prep",
    )(o1, bg, bgt, ltri, utri, same)


def _gdn_scan_kernel(qd_ref, kd_ref, u_ref, w_ref, ic_ref, gl_ref, o_ref, state_ref):
    nb, ts = qd_ref.shape[0], qd_ref.shape[1]
    c = GDN_CHUNK
    hd = GDN_HEAD_DIM

    @pl.when(pl.program_id(1) == 0)
    def _():
        state_ref[...] = jnp.zeros_like(state_ref)

    chains = [(b, h) for b in range(nb) for h in range(GDN_HEADS)]
    cols = [slice(h * hd, (h + 1) * hd) for _, h in chains]
    sts = [state_ref[b * GDN_HEADS + h] for b, h in chains]
    for n in range(ts // c):
        rows = slice(n * c, (n + 1) * c)
        stbs = [st.astype(BF16) for st in sts]
        vbs = [(u_ref[b, rows, cl] - _dot(w_ref[b, rows, cl], stb)).astype(BF16)
               for (b, _), cl, stb in zip(chains, cols, stbs)]
        for (b, h), cl, stb, vb in zip(chains, cols, stbs, vbs):
            o_ref[b, rows, cl] = _dot(qd_ref[b, rows, cl], stb) + _dot(ic_ref[b, rows, h * c:(h + 1) * c], vb)
        sts = [st * gl_ref[b, n, h:h + 1, :] + _dot_tn(kd_ref[b, rows, cl], vb)
               for (b, h), cl, st, vb in zip(chains, cols, sts, vbs)]
    for (b, h), st in zip(chains, sts):
        state_ref[b * GDN_HEADS + h] = st


def gdn_scan(qd, kd, u, w, ic, gl):
    bsz, s, _ = qd.shape
    ts = min(GDN_TILE, s)
    c = GDN_CHUNK
    nb = SCAN_BATCH if bsz % SCAN_BATCH == 0 else 1
    tok = lambda b, i: (b, i, 0)
    return pl.pallas_call(
        _gdn_scan_kernel,
        grid=(bsz // nb, s // ts),
        in_specs=[pl.BlockSpec((nb, ts, GDN_WIDTH), tok)] * 4 + [
            pl.BlockSpec((nb, ts, GDN_HEADS * c), tok),
            pl.BlockSpec((nb, ts // c, GDN_HEADS, GDN_HEAD_DIM), lambda b, i: (b, i, 0, 0))],
        out_specs=pl.BlockSpec((nb, ts, GDN_WIDTH), tok),
        out_shape=jax.ShapeDtypeStruct((bsz, s, GDN_WIDTH), F32),
        scratch_shapes=[pltpu.VMEM((nb * GDN_HEADS, GDN_HEAD_DIM, GDN_HEAD_DIM), F32)],
        compiler_params=_cparams("parallel", "arbitrary"),
        name="gdn_scan",
    )(qd, kd, u, w, ic, gl)


def _even_post_kernel(o_ref, zs_ref, yb_ref, nw_ref, y_ref):
    hd = GDN_HEAD_DIM
    o = o_ref[0]
    zs = zs_ref[0].astype(F32)
    nw = nw_ref[...]
    for h in range(GDN_HEADS):
        cols = slice(h * hd, (h + 1) * hd)
        oh = o[:, cols]
        on = oh * lax.rsqrt(jnp.mean(oh * oh, -1, keepdims=True) + NORM_EPS) * nw
        y_ref[0, :, cols] = (on * zs[:, cols]).astype(BF16)
    y_ref[0, :, GDN_WIDTH:] = yb_ref[0]


def even_post(o, o1, norm_w):
    bsz, s, _ = o.shape
    tm = min(TOKEN_TILE, s)
    wb = 512
    blk = lambda j: pl.BlockSpec((1, tm, wb), lambda b, i, j=j: (b, i, j))
    const = lambda b, i: (0, 0)
    return pl.pallas_call(
        _even_post_kernel,
        grid=(bsz, s // tm),
        in_specs=[pl.BlockSpec((1, tm, GDN_WIDTH), lambda b, i: (b, i, 0)),
                  blk(3), blk(4),
                  pl.BlockSpec((1, GDN_HEAD_DIM), const)],
        out_specs=pl.BlockSpec((1, tm, GDN_WIDTH + SC_WIDTH), lambda b, i: (b, i, 0)),
        out_shape=jax.ShapeDtypeStruct((bsz, s, GDN_WIDTH + SC_WIDTH), BF16),
        compiler_params=_cparams("parallel", "parallel"),
        name="even_post",
    )(o, o1, o1, norm_w.astype(F32).reshape(1, -1))


def _layer_norm(r, g, b):
    mu = jnp.mean(r, -1, keepdims=True)
    rc = r - mu
    var = jnp.mean(rc * rc, -1, keepdims=True)
    return rc * lax.rsqrt(var + LN_EPS) * g + b


def _route(logits_t, bias):
    scores = _sigmoid(logits_t)
    biased = scores + bias
    t = logits_t.shape[1]
    epg = EXPERTS_PER_GROUP
    sub = lax.broadcasted_iota(jnp.int32, (epg, t), 0).astype(F32)
    best = None
    for g in range(N_GROUPS):
        bgp = biased[g * epg:(g + 1) * epg]
        m1 = jnp.max(bgp, axis=0, keepdims=True)
        i1 = jnp.min(jnp.where(bgp == m1, sub, float(epg)), axis=0, keepdims=True)
        rest = jnp.where(sub == i1, -jnp.inf, bgp)
        m2 = jnp.max(rest, axis=0, keepdims=True)
        i2 = jnp.min(jnp.where(rest == m2, sub, float(epg)), axis=0, keepdims=True)
        gs = m1 + m2
        if best is None:
            best, e0, e1 = gs, i1, i2
        else:
            better = gs > best
            best = jnp.where(better, gs, best)
            e0 = jnp.where(better, float(g * epg) + i1, e0)
            e1 = jnp.where(better, float(g * epg) + i2, e1)
    eio = lax.broadcasted_iota(jnp.int32, scores.shape, 0).astype(F32)
    hit0 = eio == e0
    hit1 = eio == e1
    s0 = jnp.sum(jnp.where(hit0, scores, 0.0), axis=0, keepdims=True)
    s1 = jnp.sum(jnp.where(hit1, scores, 0.0), axis=0, keepdims=True)
    tot = s0 + s1
    return e0.astype(jnp.int32), e1.astype(jnp.int32), s0 / tot, s1 / tot, hit0, hit1


def _mix_out_kernel(y_ref, wo_ref, x_ref, moda_ref, modb_ref, lng_ref, lnb_ref, rwt_ref, rb_ref,
                    upper_ref, x1_ref, hm_ref, seli_ref, selw_ref, cnt_ref):
    d = D_MODEL
    gate = moda_ref[0, 0][:, 2 * d:]
    modb = modb_ref[0, 0]
    y = _dot(y_ref[0], wo_ref[...])
    x1 = _layer_norm(DN_ALPHA * x_ref[0] + (1.0 + gate) * y, lng_ref[...], lnb_ref[...])
    x1_ref[0] = x1
    hm = x1 * (1.0 + modb[:, d:2 * d]) + modb[:, :d]
    hm_hi = hm.astype(BF16)
    hm_ref[0] = hm_hi

    hm_lo = (hm - hm_hi.astype(F32)).astype(BF16)
    rw = rwt_ref[...]
    hi_lo = _dot_nt(rw, hm_hi)
    logits_t = hi_lo[:N_EXPERTS] + hi_lo[N_EXPERTS:] + _dot_nt(rw[:N_EXPERTS], hm_lo)
    e0, e1, w0, w1, hit0, hit1 = _route(logits_t, rb_ref[...])
    member = jnp.where(hit0 | hit1, 1.0, 0.0)
    before = _dot(member.astype(BF16), upper_ref[...])
    rank0 = jnp.sum(jnp.where(hit0, before, 0.0), axis=0, keepdims=True).astype(jnp.int32)
    rank1 = jnp.sum(jnp.where(hit1, before, 0.0), axis=0, keepdims=True).astype(jnp.int32)
    zi = jnp.zeros_like(e0)
    seli_ref[0] = jnp.concatenate([e0, e1, rank0, rank1, zi, zi, zi, zi], axis=0)
    wrows = jnp.concatenate([w0, w1, jnp.zeros((LANES - 2, w0.shape[1]), F32)], axis=0)
    selw_ref[0] = wrows.T[:, :8]
    cnt_ref[0, 0] = jnp.broadcast_to(jnp.sum(member, axis=1, keepdims=True), (N_EXPERTS, LANES))


def mix_out(ymix, w_out, x, moda, modb, ln_g, ln_b, router_w, router_bias, b0=0):
    bsz, s, d = ymix.shape
    tm = min(TOKEN_TILE, s)
    nt = s // tm
    r = np.arange(tm)
    upper = jnp.asarray((r[:, None] < r[None, :]).astype(np.float32), dtype=BF16)
    rw_f32 = router_w.T.astype(F32)
    rw_hi = rw_f32.astype(BF16)
    rwt = jnp.concatenate([rw_hi, (rw_f32 - rw_hi.astype(F32)).astype(BF16)], axis=0)
    tok = lambda b, i: (b, i, 0)
    const = lambda b, i: (0, 0)
    modspec = pl.BlockSpec((1, 1, 1, 3 * d), lambda b, i: (b, 0, 0, 0))
    row8 = pl.BlockSpec((1, 8, tm), lambda b, i: (b, 0, i))
    return pl.pallas_call(
        _mix_out_kernel,
        grid=(bsz, nt),
        in_specs=[pl.BlockSpec((1, tm, d), tok),
                  pl.BlockSpec((d, d), const),
                  pl.BlockSpec((1, tm, d), lambda b, i: (b + b0, i, 0)),
                  modspec, modspec,
                  pl.BlockSpec((1, d), const), pl.BlockSpec((1, d), const),
                  pl.BlockSpec((2 * N_EXPERTS, d), const),
                  pl.BlockSpec((N_EXPERTS, 1), const),
                  pl.BlockSpec((tm, tm), const)],
        out_specs=[pl.BlockSpec((1, tm, d), tok),
                   pl.BlockSpec((1, tm, d), tok),
                   row8, pl.BlockSpec((1, tm, 8), tok),
                   pl.BlockSpec((1, 1, N_EXPERTS, LANES), lambda b, i: (b, i, 0, 0))],
        out_shape=[jax.ShapeDtypeStruct((bsz, s, d), F32),
                   jax.ShapeDtypeStruct((bsz, s, d), BF16),
                   jax.ShapeDtypeStruct((bsz, 8, s), jnp.int32),
                   jax.ShapeDtypeStruct((bsz, s, 8), F32),
                   jax.ShapeDtypeStruct((bsz, nt, N_EXPERTS, LANES), F32)],
        compiler_params=_cparams("parallel", "parallel"),
        name="mix_out",
    )(ymix, w_out.astype(BF16), x, moda, modb, ln_g.reshape(1, d), ln_b.reshape(1, d),
      rwt, router_bias.astype(F32).reshape(N_EXPERTS, 1), upper)


def _experts_kernel(be_ref, next_ref, nused_ref, xs_ref, wg_hbm, wu_hbm, wd_hbm, ys_ref,
                    wgf_ref, wuf_ref, wdf_ref, wgb_ref, wub_ref, wdb_ref, sem_ref, slot_ref, *, layer):
    i = pl.program_id(0)

    def weight_copies(e, slot):
        return [pltpu.make_async_copy(src.at[layer, e], dst.at[slot], sem_ref.at[n, slot])
                for n, (src, dst) in enumerate(((wg_hbm, wgf_ref), (wu_hbm, wuf_ref), (wd_hbm, wdf_ref)))]

    @pl.when(i == 0)
    def _():
        slot_ref[0] = 0
        for cp in weight_copies(be_ref[0], 0):
            cp.start()

    @pl.when((i == 0) | (be_ref[i] != be_ref[jnp.maximum(i - 1, 0)]))
    def _():
        slot = slot_ref[0]
        for cp in weight_copies(be_ref[i], slot):
            cp.wait()

        @pl.when(next_ref[i] >= 0)
        def _():
            for cp in weight_copies(next_ref[i], 1 - slot):
                cp.start()

        wgb_ref[...] = wgf_ref[slot].astype(BF16)
        wub_ref[...] = wuf_ref[slot].astype(BF16)
        wdb_ref[...] = wdf_ref[slot].astype(BF16)
        slot_ref[0] = 1 - slot

    @pl.when(i < nused_ref[0])
    def _():
        x = xs_ref[...]
        act =(_silu(_dot(x, wgb_ref[...])) * _dot(x, wub_ref[...])).astype(BF16)
        ys_ref[...] = _dot(act, wdb_ref[...]).astype(BF16)

    @pl.when(i >= nused_ref[0])
    def _():
        ys_ref[...] = jnp.zeros_like(ys_ref)


def moe_experts(xs, block_expert, n_used, w_gate, w_up, w_down, layer):
    rows, d = xs.shape
    nb = rows // MOE_BLOCK
    later = jnp.where(block_expert[None, :] > block_expert[:, None], block_expert[None, :], N_EXPERTS)
    next_expert = jnp.min(later, axis=1)
    next_expert = jnp.where(next_expert == N_EXPERTS, -1, next_expert).astype(jnp.int32)
    hbm = pl.BlockSpec(memory_space=pl.ANY)
    rows_spec = pl.BlockSpec((MOE_BLOCK, d), lambda i, be, nx, nu: (i, 0))
    return pl.pallas_call(
        functools.partial(_experts_kernel, layer=layer),
        grid_spec=pltpu.PrefetchScalarGridSpec(
            num_scalar_prefetch=3,
            grid=(nb,),
            in_specs=[rows_spec, hbm, hbm, hbm],
            out_specs=rows_spec,
            scratch_shapes=[pltpu.VMEM((2, d, D_EXPERT), F32), pltpu.VMEM((2, d, D_EXPERT), F32),
                            pltpu.VMEM((2, D_EXPERT, d), F32),
                            pltpu.VMEM((d, D_EXPERT), BF16), pltpu.VMEM((d, D_EXPERT), BF16),
                            pltpu.VMEM((D_EXPERT, d), BF16),
                            pltpu.SemaphoreType.DMA((3, 2)), pltpu.SMEM((1,), jnp.int32)]),
        out_shape=jax.ShapeDtypeStruct((rows, d), BF16),
        compiler_params=_cparams("arbitrary"),
        name="moe_experts",
    )(block_expert, next_expert, n_used, xs, w_gate, w_up, w_down)


def _plan_kernel(seli_ref, base_ref, dest_ref):
    sel = seli_ref[0]
    base = base_ref[0, 0]
    eio = lax.broadcasted_iota(jnp.int32, (N_EXPERTS, sel.shape[1]), 0)
    out = []
    for kk in range(2):
        first = jnp.sum(jnp.where(eio == sel[kk:kk + 1], base, 0.0), axis=0, keepdims=True)
        out.append(first.astype(jnp.int32) + sel[2 + kk:3 + kk])
    zero = jnp.zeros_like(out[0])
    dest_ref[0] = jnp.concatenate(out + [zero] * 6, axis=0)


def dispatch_plan(seli, base):
    bsz, _, s = seli.shape
    nt = base.shape[1]
    tm = s // nt
    return pl.pallas_call(
        _plan_kernel,
        grid=(bsz, nt),
        in_specs=[pl.BlockSpec((1, 8, tm), lambda b, i: (b, 0, i)),
                  pl.BlockSpec((1, 1, N_EXPERTS, 1), lambda b, i: (b, i, 0, 0))],
        out_specs=pl.BlockSpec((1, 8, tm), lambda b, i: (b, 0, i)),
        out_shape=jax.ShapeDtypeStruct((bsz, 8, s), jnp.int32),
        compiler_params=_cparams("parallel", "parallel"),
        name="dispatch_plan",
    )(seli, base)


def _invert_kernel(dest_ref, pad_lo_ref, pad_hi_ref, slot_tok_ref):
    t = dest_ref.shape[0] // 2

    def clear_range(e, carry):
        hi = pad_hi_ref[e]
        n = (hi - pad_lo_ref[e] + CLEAR_CHUNK - 1) // CLEAR_CHUNK

        def clear(q, c):
            r0 = hi - (q + 1) * CLEAR_CHUNK
            tok0 = lax.rem(r0, t)
            for j in range(CLEAR_CHUNK):
                tok = tok0 + j
                slot_tok_ref[r0 + j] = jnp.where(tok >= t, tok - t, tok)
            return c
        return lax.fori_loop(0, n, clear, carry)

    def put(a, carry):
        slot_tok_ref[dest_ref[a]] = a
        slot_tok_ref[dest_ref[t + a]] = a
        return carry

    lax.fori_loop(0, pad_lo_ref.shape[0], clear_range, 0)
    lax.fori_loop(0, t, put, 0, unroll=16)


def invert_slots(dest_flat, pad_lo, pad_hi, rows):
    smem = pl.BlockSpec(memory_space=pltpu.SMEM)
    return pl.pallas_call(
        _invert_kernel,
        in_specs=[smem, smem, smem],
        out_specs=smem,
        out_shape=jax.ShapeDtypeStruct((rows,), jnp.int32),
        name="invert_slots",
    )(dest_flat, pad_lo, pad_hi)


def _moe_combine_kernel(g0_ref, g1_ref, wt_ref, x_ref, mod_ref, lng_ref, lnb_ref, o_ref):
    d = D_MODEL
    gate = mod_ref[0, 0][:, 2 * d:]
    wt = wt_ref[0]
    y = g0_ref[0, 0].astype(F32) * wt[:, 0:1] + g1_ref[0, 0].astype(F32) * wt[:, 1:2]
    o_ref[0] = _layer_norm(DN_ALPHA * x_ref[0] + (1.0 + gate) * y, lng_ref[...], lnb_ref[...])


def moe_combine(g, wt, x1, modb, ln_g, ln_b):
    bsz, s, d = x1.shape
    tm = min(TOKEN_TILE, s)
    tok = lambda b, i: (b, i, 0)
    const = lambda b, i: (0, 0)
    return pl.pallas_call(
        _moe_combine_kernel,
        grid=(bsz, s // tm),
        in_specs=[pl.BlockSpec((1, 1, tm, d), lambda b, i: (0, b, i, 0)),
                  pl.BlockSpec((1, 1, tm, d), lambda b, i: (1, b, i, 0)),
                  pl.BlockSpec((1, tm, 8), tok), pl.BlockSpec((1, tm, d), tok),
                  pl.BlockSpec((1, 1, 1, 3 * d), lambda b, i: (b, 0, 0, 0)),
                  pl.BlockSpec((1, d), const), pl.BlockSpec((1, d), const)],
        out_specs=pl.BlockSpec((1, tm, d), tok),
        out_shape=jax.ShapeDtypeStruct((bsz, s, d), F32),
        compiler_params=_cparams("parallel", "parallel"),
        name="moe_combine",
    )(g, g, wt, x1, modb, ln_g.reshape(1, d), ln_b.reshape(1, d))


def moe_layer(x1, hm, seli, selw, cnt, modb, ln_g, ln_b, w_gate, w_up, w_down, layer):
    bsz, s, d = x1.shape
    t = bsz * s
    tm = min(TOKEN_TILE, s)
    blk = MOE_BLOCK
    cnt = cnt[..., 0].reshape(-1, N_EXPERTS).astype(jnp.int32)
    tile_off = jnp.cumsum(cnt, axis=0) - cnt
    counts = cnt.sum(0)
    padded = (counts + blk - 1) // blk * blk
    pend = jnp.cumsum(padded)
    pstart = pend - padded
    nb = -(-(2 * t) // blk) + N_EXPERTS
    rows = nb * blk
    base = (pstart[None, :] + tile_off).astype(F32).reshape(bsz, s // tm, N_EXPERTS, 1)
    dest = dispatch_plan(seli, base)
    dest = jnp.swapaxes(dest[:, 0:2, :], 0, 1).reshape(2 * t)
    pad_lo = jnp.concatenate([pstart + counts, pend[-1:]]).astype(jnp.int32)
    pad_hi = jnp.concatenate([pend, jnp.full((1,), rows)]).astype(jnp.int32)
    slot_tok = invert_slots(dest, pad_lo, pad_hi, rows)
    starts = jnp.arange(nb, dtype=jnp.int32) * blk
    block_expert = jnp.minimum(jnp.sum((pend[None, :] <= starts[:, None]).astype(jnp.int32), axis=1),
                               N_EXPERTS - 1)
    n_used = (pend[-1] // blk).astype(jnp.int32).reshape(1)
    xs = hm.reshape(t, d).at[slot_tok].get(mode="promise_in_bounds")
    ys = moe_experts(xs, block_expert, n_used, w_gate, w_up, w_down, layer)
    g = ys.at[dest].get(mode="promise_in_bounds").reshape(2, bsz, s, d)
    return moe_combine(g, selw, x1, modb, ln_g, ln_b)


MLA_IN_COLS = Q_LORA + KV_LORA + 2 * LANES
MLA_QK = 2 * LANES


def _mla_in_kernel(x_ref, mod_ref, pos_ref, invf_ref, win_ref, qn_ref, kvn_ref, wq_ref, wkv_ref,
                   q_ref, k_ref, v_ref):
    d = D_MODEL
    nh = MLA_HEADS
    mod = mod_ref[0, 0]
    h = (x_ref[0] * (1.0 + mod[:, d:2 * d]) + mod[:, :d]).astype(BF16)
    proj = _dot(h, win_ref[...])
    ang = pos_ref[0].astype(F32) * invf_ref[...]
    lane = lax.broadcasted_iota(jnp.int32, ang.shape, 1)
    cos = jnp.where(lane < QK_ROPE, jnp.cos(ang), 0.0)
    sin = jnp.where(lane < QK_ROPE, jnp.sin(ang), 0.0)

    qa = proj[:, :Q_LORA]
    qa = (qa * lax.rsqrt(jnp.mean(qa * qa, -1, keepdims=True) + NORM_EPS) * qn_ref[...]).astype(BF16)
    kva = proj[:, Q_LORA:Q_LORA + KV_LORA]
    kva = (kva * lax.rsqrt(jnp.mean(kva * kva, -1, keepdims=True) + NORM_EPS) * kvn_ref[...]).astype(BF16)
    kr0 = Q_LORA + KV_LORA
    k_rope = (proj[:, kr0:kr0 + LANES] * cos + proj[:, kr0 + LANES:kr0 + 2 * LANES] * sin).astype(BF16)

    scale = (QK_NOPE + QK_ROPE) ** -0.5 * LOG2_E
    hw = nh * LANES
    q_nope = _dot(qa, wq_ref[:, :hw])
    q_rope = _dot(qa, wq_ref[:, hw:2 * hw])
    q_rot = _dot(qa, wq_ref[:, 2 * hw:])
    k_nope = _dot(kva, wkv_ref[:, :hw])
    v_ref[0] = _dot(kva, wkv_ref[:, hw:]).astype(BF16)
    for hh in range(nh):
        cols = slice(hh * LANES, (hh + 1) * LANES)
        q_ref[0, :, hh * MLA_QK:hh * MLA_QK + LANES] = (q_nope[:, cols] * scale).astype(BF16)
        q_ref[0, :, hh * MLA_QK + LANES:(hh + 1) * MLA_QK] = (
            (q_rope[:, cols] * cos + q_rot[:, cols] * sin) * scale).astype(BF16)
        k_ref[0, :, hh * MLA_QK:hh * MLA_QK + LANES] = k_nope[:, cols].astype(BF16)
        k_ref[0, :, hh * MLA_QK + LANES:(hh + 1) * MLA_QK] = k_rope


def _rope_cols(w):
    half = QK_ROPE // 2
    pad = [(0, 0)] * (w.ndim - 1) + [(0, LANES - QK_ROPE)]
    rot = jnp.concatenate([-w[..., half:], w[..., :half]], axis=-1)
    return jnp.pad(w, pad), jnp.pad(rot, pad)


def mla_in(x, mod, positions, w_in, q_a_norm, w_q_b, kv_a_norm, w_kv_b):
    bsz, s, d = x.shape
    tm = min(TOKEN_TILE, s)
    nh = MLA_HEADS
    kr, kr_rot = _rope_cols(w_in[:, Q_LORA + KV_LORA:])
    win = jnp.concatenate([w_in[:, :Q_LORA + KV_LORA], kr, kr_rot], axis=1).astype(BF16)
    wq = w_q_b.reshape(Q_LORA, nh, QK_NOPE + QK_ROPE)
    qr, qr_rot = _rope_cols(wq[..., QK_NOPE:])
    wq = jnp.concatenate([wq[..., :QK_NOPE].reshape(Q_LORA, -1), qr.reshape(Q_LORA, -1),
                          qr_rot.reshape(Q_LORA, -1)], axis=1).astype(BF16)
    wkv = w_kv_b.reshape(KV_LORA, nh, QK_NOPE + V_DIM)
    wkv = jnp.concatenate([wkv[..., :QK_NOPE].reshape(KV_LORA, -1),
                           wkv[..., QK_NOPE:].reshape(KV_LORA, -1)], axis=1).astype(BF16)
    inv = ROPE_THETA ** (-np.arange(0, QK_ROPE, 2, dtype=np.float32) / QK_ROPE)
    invf = np.zeros((1, LANES), np.float32)
    invf[0, :QK_ROPE] = np.concatenate([inv, inv])
    tok = lambda b, i: (b, i, 0)
    const = lambda b, i: (0, 0)
    return pl.pallas_call(
        _mla_in_kernel,
        grid=(bsz, s // tm),
        in_specs=[pl.BlockSpec((1, tm, d), tok),
                  pl.BlockSpec((1, 1, 1, 3 * d), lambda b, i: (b, 0, 0, 0)),
                  pl.BlockSpec((1, tm, 1), tok),
                  pl.BlockSpec((1, LANES), const),
                  pl.BlockSpec((d, MLA_IN_COLS), const),
                  pl.BlockSpec((1, Q_LORA), const),
                  pl.BlockSpec((1, KV_LORA), const),
                  pl.BlockSpec((Q_LORA, 3 * nh * LANES), const),
                  pl.BlockSpec((KV_LORA, 2 * nh * LANES), const)],
        out_specs=[pl.BlockSpec((1, tm, nh * MLA_QK), tok),
                   pl.BlockSpec((1, tm, nh * MLA_QK), tok),
                   pl.BlockSpec((1, tm, nh * V_DIM), tok)],
        out_shape=[jax.ShapeDtypeStruct((bsz, s, nh * MLA_QK), BF16),
                   jax.ShapeDtypeStruct((bsz, s, nh * MLA_QK), BF16),
                   jax.ShapeDtypeStruct((bsz, s, nh * V_DIM), BF16)],
        compiler_params=_cparams("parallel", "parallel"),
        name="mla_in",
    )(x, mod, positions.reshape(bsz, s, 1), jnp.asarray(invf), win,
      q_a_norm.astype(F32).reshape(1, -1), kv_a_norm.astype(F32).reshape(1, -1), wq, wkv)


def _attn_kernel(q_ref, k_ref, v_ref, o_ref, m_ref, l_ref, acc_ref):
    tq = q_ref.shape[1]
    i = pl.program_id(2)
    m_ref[...] = jnp.full(m_ref.shape, -jnp.inf, F32)
    l_ref[...] = jnp.zeros(l_ref.shape, F32)
    acc_ref[...] = jnp.zeros(acc_ref.shape, F32)

    groups = [slice(g * ATTN_ROWS, (g + 1) * ATTN_ROWS) for g in range(tq // ATTN_ROWS)]

    def update(off, widths, masked):
        scs = []
        for rows, width in zip(groups, widths):
            sc = _dot_nt(q_ref[0, rows, :], k_ref[0, pl.ds(off, width), :])
            if masked:
                qi = lax.broadcasted_iota(jnp.int32, sc.shape, 0) + rows.start
                ki = lax.broadcasted_iota(jnp.int32, sc.shape, 1)
                sc = jnp.where(ki <= qi, sc, -jnp.inf)
            scs.append(sc)
        m_olds = [m_ref[rows] for rows in groups]
        m_news = [jnp.maximum(m_old, jnp.max(sc, -1, keepdims=True)) for m_old, sc in zip(m_olds, scs)]
        ps = [jnp.exp2(sc - jnp.concatenate([m_new] * (width // LANES), axis=1))
              for sc, m_new, width in zip(scs, m_news, widths)]
        alphas = [jnp.exp2(m_old - m_new) for m_old, m_new in zip(m_olds, m_news)]
        pvs = [_dot(p.astype(BF16), v_ref[0, pl.ds(off, width), :]) for p, width in zip(ps, widths)]
        for rows, m_new, alpha, p, pv in zip(groups, m_news, alphas, ps, pvs):
            l_ref[rows] = alpha * l_ref[rows] + jnp.sum(p, -1, keepdims=True)
            acc_ref[rows] = alpha * acc_ref[rows] + pv
            m_ref[rows] = m_new

    def full_pair(j, carry):
        update(pl.multiple_of(j * 2 * tq, 2 * tq), [2 * tq] * len(groups), False)
        return carry

    lax.fori_loop(0, i // 2, full_pair, 0)

    @pl.when(i % 2 == 1)
    def _():
        update(pl.multiple_of((i - 1) * tq, tq), [tq] * len(groups), False)
    update(pl.multiple_of(i * tq, tq), [rows.stop for rows in groups], True)
    o_ref[0] = (acc_ref[...] / l_ref[...]).astype(BF16)


def attention(q, k, v):
    bsz, s, _ = q.shape
    tq = min(ATTN_TILE, s)
    return pl.pallas_call(
        _attn_kernel,
        grid=(bsz, MLA_HEADS, s // tq),
        in_specs=[pl.BlockSpec((1, tq, MLA_QK), lambda b, h, i: (b, i, h)),
                  pl.BlockSpec((1, s, MLA_QK), lambda b, h, i: (b, 0, h)),
                  pl.BlockSpec((1, s, V_DIM), lambda b, h, i: (b, 0, h))],
        out_specs=pl.BlockSpec((1, tq, V_DIM), lambda b, h, i: (b, i, h)),
        out_shape=jax.ShapeDtypeStruct((bsz, s, MLA_HEADS * V_DIM), BF16),
        scratch_shapes=[pltpu.VMEM((tq, LANES), F32), pltpu.VMEM((tq, LANES), F32), pltpu.VMEM((tq, V_DIM), F32)],
        compiler_params=_cparams("parallel", "parallel", "arbitrary"),
        name="attention",
    )(q, k, v)


def kernel(x, c, positions, ada_w, ada_b, ln_g, ln_b, w_in_e, gdn_conv_w, gdn_a_log, gdn_dt_bias,
           gdn_norm_w, sc_conv_w, w_out_e, w_in_o, q_a_norm, w_q_b, kv_a_norm, w_kv_b, w_out_o,
           router_w, router_bias, w_gate, w_up, w_down):
    bsz, s, d = x.shape
    depth = ada_w.shape[0]
    mod = ada_mod(c, ada_w, ada_b).reshape(depth, 2, bsz, 1, 3 * d)

    def layers(x, b0, mod, positions):
        for i in range(depth):
            j = i // 2
            moda, modb = mod[i, 0][:, None], mod[i, 1][:, None]
            if i % 2 == 0:
                o1, bg, bgt = even_in_proj(x, moda, w_in_e[j], gdn_a_log[j], gdn_dt_bias[j],
                                           gdn_conv_w[j], sc_conv_w[j], b0)
                qd, kd, u, w, ic, gl = gdn_prep(o1, bg, bgt)
                o = gdn_scan(qd, kd, u, w, ic, gl)
                ymix = even_post(o, o1, gdn_norm_w[j])
                w_out = w_out_e[j]
            else:
                q, k, v = mla_in(x, moda, positions, w_in_o[j], q_a_norm[j], w_q_b[j], kv_a_norm[j],
                                 w_kv_b[j])
                ymix = attention(q, k, v)
                w_out = w_out_o[j]
            x1, hm, seli, selw, cnt = mix_out(ymix, w_out, x, moda, modb, ln_g[i, 0], ln_b[i, 0],
                                              router_w, router_bias, b0)
            x = moe_layer(x1, hm, seli, selw, cnt, modb, ln_g[i, 1], ln_b[i, 1], w_gate, w_up, w_down, i)
            b0 = 0
        return x

    ng = BATCH_GROUPS if bsz % BATCH_GROUPS == 0 else 1
    gb = bsz // ng
    outs = [layers(x, g * gb, mod[:, :, g * gb:(g + 1) * gb], positions[g * gb:(g + 1) * gb])
            for g in range(ng)]
    return outs[0] if ng == 1 else jnp.concatenate(outs, axis=0)
```

```python
import functools

import numpy as np
import jax
import jax.numpy as jnp
from jax import lax
from jax.experimental import pallas as pl
from jax.experimental.pallas import tpu as pltpu

F32 = jnp.float32
BF16 = jnp.bfloat16
HIGHEST = lax.Precision.HIGHEST

D_MODEL = 1024
DEPTH = 2
DN_ALPHA = (2.0 * DEPTH) ** 0.25

GDN_HEADS = 4
GDN_HEAD_DIM = 128
GDN_WIDTH = GDN_HEADS * GDN_HEAD_DIM
GDN_CONV = 4
GDN_CHUNK = 64
SC_WIDTH = 512
SC_CONV = 3

MLA_HEADS = 8
Q_LORA = 384
KV_LORA = 256
QK_NOPE = 128
QK_ROPE = 64
V_DIM = 128
ROPE_THETA = 10000.0

N_EXPERTS = 32
N_GROUPS = 4
EXPERTS_PER_GROUP = N_EXPERTS // N_GROUPS
D_EXPERT = 512

NORM_EPS = 1e-6
LN_EPS = 1e-5

LANES = 128
HALO = 16
TOKEN_TILE = 512
GDN_TILE = 256
ATTN_TILE = 1024
ATTN_ROWS = 256
LOG2_E = 1.4426950408889634
MOE_BLOCK = 512
SCAN_BATCH = 4
BATCH_GROUPS = 1
CLEAR_CHUNK = 8
VMEM_LIMIT = 48 * 1024 * 1024


def _cparams(*sem):
    return pltpu.CompilerParams(dimension_semantics=sem, vmem_limit_bytes=VMEM_LIMIT)


def _sigmoid(x):
    return 1.0 / (1.0 + jnp.exp(-x))


def _silu(x):
    return x * _sigmoid(x)


def _dot(a, b):
    return jnp.dot(a, b, preferred_element_type=F32)


def _dot_nt(a, b, precision=None):
    return lax.dot_general(a, b, (((1,), (1,)), ((), ())), precision=precision,
                           preferred_element_type=F32)


def _dot_tn(a, b):
    return lax.dot_general(a, b, (((0,), (0,)), ((), ())), preferred_element_type=F32)


def _ada_kernel(c_ref, w_ref, b_ref, o_ref):
    cond = _silu(c_ref[...])
    o_ref[0] = jnp.dot(cond, w_ref[0], precision=HIGHEST, preferred_element_type=F32) + b_ref[0]


def ada_mod(c, ada_w, ada_b):
    nl = ada_w.shape[0] * ada_w.shape[1]
    bsz, d = c.shape
    w = ada_w.reshape(nl, d, 3 * d)
    b = ada_b.reshape(nl, 1, 3 * d)
    return pl.pallas_call(
        _ada_kernel,
        grid=(nl, 3),
        in_specs=[pl.BlockSpec((bsz, d), lambda l, j: (0, 0)),
                  pl.BlockSpec((1, d, d), lambda l, j: (l, 0, j)),
                  pl.BlockSpec((1, 1, d), lambda l, j: (l, 0, j))],
        out_specs=pl.BlockSpec((1, bsz, d), lambda l, j: (l, 0, j)),
        out_shape=jax.ShapeDtypeStruct((nl, bsz, 3 * d), F32),
        compiler_params=_cparams("parallel", "parallel"),
        name="ada_mod",
    )(c, w, b)


EVEN_MAIN = 3 * GDN_WIDTH + GDN_WIDTH + 3 * SC_WIDTH
EVEN_OUT = 3 * GDN_WIDTH + GDN_WIDTH + SC_WIDTH


def _causal_conv(x, halo, w, taps):
    rows = x.shape[0]
    nh = halo.shape[0]
    xf = jnp.concatenate([halo, x], axis=0)
    y = w[taps - 1:taps] * x
    for j in range(taps - 1):
        off = nh - (taps - 1) + j
        y = y + w[j:j + 1] * xf[off:off + rows]
    return y


def _even_in_kernel(x_ref, mod_ref, w1_ref, w2_ref, alog_ref, dtb_ref, gcw_ref, scw_ref,
                    o1_ref, o2_ref, o2t_ref, qkv_tail_ref, ch_tail_ref):
    d = D_MODEL
    cw = 512
    first = pl.program_id(1) == 0
    mod = mod_ref[0, 0]
    h = (x_ref[0] * (1.0 + mod[:, d:2 * d]) + mod[:, :d]).astype(BF16)
    tm = h.shape[0]
    gcw = gcw_ref[...]

    def project(j):
        return _dot(h, w1_ref[:, j * cw:(j + 1) * cw])

    def finish_qkv(j, r):
        cols = slice(j * cw, (j + 1) * cw)
        halo = jnp.where(first, 0.0, qkv_tail_ref[:, cols])
        o1_ref[0, :, cols] = _silu(_causal_conv(r, halo, gcw[:, cols], GDN_CONV)).astype(BF16)
        qkv_tail_ref[:, cols] = r[tm - HALO:]

    r0 = project(0)
    r1 = project(1)
    finish_qkv(0, r0)
    r2 = project(2)
    finish_qkv(1, r1)
    z = project(3)
    finish_qkv(2, r2)
    sc_c = project(5)
    o1_ref[0, :, 3 * cw:4 * cw] = _silu(z).astype(BF16)
    sc_h = project(6)
    sc_b = project(4)
    ch = sc_c * sc_h
    halo = jnp.where(first, 0.0, ch_tail_ref[...])
    r = _dot(h, w2_ref[...])
    o1_ref[0, :, 4 * cw:5 * cw] = (sc_b * _causal_conv(ch, halo, scw_ref[...], SC_CONV)).astype(BF16)
    ch_tail_ref[...] = ch[tm - HALO:]
    lane = lax.broadcasted_iota(jnp.int32, r.shape, 1)
    a = r + dtb_ref[...]
    softplus = jnp.maximum(a, 0.0) + jnp.log(1.0 + jnp.exp(-jnp.abs(a)))
    bg = jnp.where(lane < GDN_HEADS, _sigmoid(r), -jnp.exp(alog_ref[...]) * softplus)
    o2_ref[0] = bg
    o2t_ref[0] = bg.T[:8]


def even_in_proj(x, mod, w_in, a_log, dt_bias, gdn_conv_w, sc_conv_w, b0=0):
    _, s, d = x.shape
    bsz = mod.shape[0]
    tm = min(TOKEN_TILE, s)
    q_end = 4 * GDN_WIDTH
    w1 = jnp.concatenate([w_in[:, :q_end], w_in[:, q_end + 2 * GDN_HEADS:]], axis=1).astype(BF16)
    w2 = jnp.pad(w_in[:, q_end:q_end + 2 * GDN_HEADS], ((0, 0), (0, LANES - 2 * GDN_HEADS))).astype(BF16)
    alog = jnp.pad(a_log.astype(F32), (GDN_HEADS, LANES - 2 * GDN_HEADS)).reshape(1, LANES)
    dtb = jnp.pad(dt_bias.astype(F32), (GDN_HEADS, LANES - 2 * GDN_HEADS)).reshape(1, LANES)
    const = lambda b, i: (0, 0)
    return pl.pallas_call(
        _even_in_kernel,
        grid=(bsz, s // tm),
        in_specs=[pl.BlockSpec((1, tm, d), lambda b, i: (b + b0, i, 0)),
                  pl.BlockSpec((1, 1, 1, 3 * d), lambda b, i: (b, 0, 0, 0)),
                  pl.BlockSpec((d, EVEN_MAIN), const),
                  pl.BlockSpec((d, LANES), const),
                  pl.BlockSpec((1, LANES), const),
                  pl.BlockSpec((1, LANES), const),
                  pl.BlockSpec((GDN_CONV, 3 * GDN_WIDTH), const),
                  pl.BlockSpec((SC_CONV, SC_WIDTH), const)],
        out_specs=[pl.BlockSpec((1, tm, EVEN_OUT), lambda b, i: (b, i, 0)),
                   pl.BlockSpec((1, tm, LANES), lambda b, i: (b, i, 0)),
                   pl.BlockSpec((1, 8, tm), lambda b, i: (b, 0, i))],
        out_shape=[jax.ShapeDtypeStruct((bsz, s, EVEN_OUT), BF16),
                   jax.ShapeDtypeStruct((bsz, s, LANES), F32),
                   jax.ShapeDtypeStruct((bsz, 8, s), F32)],
        scratch_shapes=[pltpu.VMEM((HALO, 3 * GDN_WIDTH), F32), pltpu.VMEM((HALO, SC_WIDTH), F32)],
        compiler_params=_cparams("parallel", "arbitrary"),
        name="even_in_proj",
    )(x, mod, w1, w2, alog, dtb, gdn_conv_w.astype(F32), sc_conv_w.astype(F32))


def _inv_unit_lower(lows, xor_ij, block):
    eye = jnp.where(xor_ij == 0, 1.0, 0.0)
    ms = [eye - jnp.where(xor_ij == 1, low, 0.0) for low in lows]
    s = 2
    while s < block:
        level = (xor_ij >> (s.bit_length() - 1)) == 1
        mbs = [m.astype(BF16) for m in ms]
        cms = [_dot(jnp.where(level, low, 0.0).astype(BF16), mb).astype(BF16) for low, mb in zip(lows, mbs)]
        ms = [m - _dot(mb, cm) for m, mb, cm in zip(ms, mbs, cms)]
        s *= 2
    return ms


def _gdn_prep_kernel(qkv_ref, bg_ref, bgt_ref, ltri_ref, utri_ref, same_ref,
                     qd_ref, kd_ref, u_ref, w_ref, ic_ref, gl_ref):
    ts = qkv_ref.shape[1]
    c = GDN_CHUNK
    hd = GDN_HEAD_DIM
    y = qkv_ref[0].astype(F32)

    bg = bg_ref[0]
    gc_col = jnp.dot(ltri_ref[...], bg, precision=HIGHEST, preferred_element_type=F32)
    gc_row = jnp.dot(bgt_ref[0], utri_ref[...], precision=HIGHEST, preferred_element_type=F32)
    gc_end = jnp.dot(same_ref[...], bg, precision=HIGHEST, preferred_element_type=F32)

    ii = lax.broadcasted_iota(jnp.int32, (ts, ts), 0)
    jj = lax.broadcasted_iota(jnp.int32, (ts, ts), 1)
    xor_ij = ii ^ jj
    causal = (same_ref[...] > 0.0) & (ii >= jj)
    diag = xor_ij == 0

    heads = range(GDN_HEADS)
    gcols = [gc_col[:, GDN_HEADS + h:GDN_HEADS + h + 1] for h in heads]
    gends = [gc_end[:, GDN_HEADS + h:GDN_HEADS + h + 1] for h in heads]
    egcs = [jnp.exp(g) for g in gcols]
    lows, intras, rhss, qs, ks = [], [], [], [], []
    for h in heads:
        q = y[:, h * hd:(h + 1) * hd]
        k = y[:, GDN_WIDTH + h * hd:GDN_WIDTH + (h + 1) * hd]
        v = y[:, 2 * GDN_WIDTH + h * hd:2 * GDN_WIDTH + (h + 1) * hd]
        q = q * lax.rsqrt(jnp.sum(q * q, -1, keepdims=True) + NORM_EPS) * (hd ** -0.5)
        k = k * lax.rsqrt(jnp.sum(k * k, -1, keepdims=True) + NORM_EPS)
        beta = bg[:, h:h + 1]
        grow = gc_row[GDN_HEADS + h:GDN_HEADS + h + 1, :]
        decay = jnp.exp(jnp.where(causal, gcols[h] - grow, -jnp.inf))
        kb = k * beta
        kbf = k.astype(BF16)
        lows.append(jnp.where(diag, 0.0, _dot_nt(kb.astype(BF16), kbf) * decay))
        intras.append(_dot_nt(q.astype(BF16), kbf) * decay)
        rhss.append(jnp.concatenate([v * beta, kb * egcs[h]], axis=-1).astype(BF16))
        qs.append(q)
        ks.append(k)

    invs = _inv_unit_lower(lows, xor_ij, c)
    for h in heads:
        sol = _dot(invs[h].astype(BF16), rhss[h])
        cols = slice(h * hd, (h + 1) * hd)
        u_ref[0, :, cols] = sol[:, :hd]
        w_ref[0, :, cols] = sol[:, hd:].astype(BF16)
        qd_ref[0, :, cols] = (qs[h] * egcs[h]).astype(BF16)
        kd_ref[0, :, cols] = (ks[h] * jnp.exp(gends[h] - gcols[h])).astype(BF16)
        packed = intras[h][:, :c]
        for n in range(1, ts // c):
            packed = packed + intras[h][:, n * c:(n + 1) * c]
        ic_ref[0, :, h * c:(h + 1) * c] = packed.astype(BF16)
        for n in range(ts // c):
            gl_ref[0, n, h:h + 1, :] = jnp.broadcast_to(jnp.exp(gends[h][n * c:n * c + 1]), (1, hd))


def gdn_prep(o1, bg, bgt):
    bsz, s, _ = o1.shape
    ts = min(GDN_TILE, s)
    c = GDN_CHUNK
    nc = ts // c
    qkv_w = 3 * GDN_WIDTH
    r = np.arange(ts)
    same = (r[:, None] // c) == (r[None, :] // c)
    ltri = jnp.asarray((same & (r[:, None] >= r[None, :])).astype(np.float32))
    utri = jnp.asarray((same & (r[:, None] <= r[None, :])).astype(np.float32))
    same = jnp.asarray(same.astype(np.float32))
    tok = lambda b, i: (b, i, 0)
    const = lambda b, i: (0, 0)
    wide = jax.ShapeDtypeStruct((bsz, s, GDN_WIDTH), BF16)
    return pl.pallas_call(
        _gdn_prep_kernel,
        grid=(bsz, s // ts),
        in_specs=[pl.BlockSpec((1, ts, qkv_w), tok),
                  pl.BlockSpec((1, ts, LANES), tok),
                  pl.BlockSpec((1, 8, ts), lambda b, i: (b, 0, i)),
                  pl.BlockSpec((ts, ts), const),
                  pl.BlockSpec((ts, ts), const),
                  pl.BlockSpec((ts, ts), const)],
        out_specs=[pl.BlockSpec((1, ts, GDN_WIDTH), tok),
                   pl.BlockSpec((1, ts, GDN_WIDTH), tok),
                   pl.BlockSpec((1, ts, GDN_WIDTH), tok),
                   pl.BlockSpec((1, ts, GDN_WIDTH), tok),
                   pl.BlockSpec((1, ts, GDN_HEADS * c), tok),
                   pl.BlockSpec((1, nc, GDN_HEADS, GDN_HEAD_DIM), lambda b, i: (b, i, 0, 0))],
        out_shape=[wide, wide, jax.ShapeDtypeStruct((bsz, s, GDN_WIDTH), F32), wide,
                   jax.ShapeDtypeStruct((bsz, s, GDN_HEADS * c), BF16),
                   jax.ShapeDtypeStruct((bsz, s // c, GDN_HEADS, GDN_HEAD_DIM), F32)],
        compiler_params=_cparams("parallel", "parallel"),
        name="gdn_prep",
    )(o1, bg, bgt, ltri, utri, same)


def _gdn_scan_kernel(qd_ref, kd_ref, u_ref, w_ref, ic_ref, gl_ref, o_ref, state_ref):
    nb, ts = qd_ref.shape[0], qd_ref.shape[1]
    c = GDN_CHUNK
    hd = GDN_HEAD_DIM

    @pl.when(pl.program_id(1) == 0)
    def _():
        state_ref[...] = jnp.zeros_like(state_ref)

    chains = [(b, h) for b in range(nb) for h in range(GDN_HEADS)]
    cols = [slice(h * hd, (h + 1) * hd) for _, h in chains]
    sts = [state_ref[b * GDN_HEADS + h] for b, h in chains]
    for n in range(ts // c):
        rows = slice(n * c, (n + 1) * c)
        stbs = [st.astype(BF16) for st in sts]
        vbs = [(u_ref[b, rows, cl] - _dot(w_ref[b, rows, cl], stb)).astype(BF16)
               for (b, _), cl, stb in zip(chains, cols, stbs)]
        for (b, h), cl, stb, vb in zip(chains, cols, stbs, vbs):
            o_ref[b, rows, cl] = _dot(qd_ref[b, rows, cl], stb) + _dot(ic_ref[b, rows, h * c:(h + 1) * c], vb)
        sts = [st * gl_ref[b, n, h:h + 1, :] + _dot_tn(kd_ref[b, rows, cl], vb)
               for (b, h), cl, st, vb in zip(chains, cols, sts, vbs)]
    for (b, h), st in zip(chains, sts):
        state_ref[b * GDN_HEADS + h] = st


def gdn_scan(qd, kd, u, w, ic, gl):
    bsz, s, _ = qd.shape
    ts = min(GDN_TILE, s)
    c = GDN_CHUNK
    nb = SCAN_BATCH if bsz % SCAN_BATCH == 0 else 1
    tok = lambda b, i: (b, i, 0)
    return pl.pallas_call(
        _gdn_scan_kernel,
        grid=(bsz // nb, s // ts),
        in_specs=[pl.BlockSpec((nb, ts, GDN_WIDTH), tok)] * 4 + [
            pl.BlockSpec((nb, ts, GDN_HEADS * c), tok),
            pl.BlockSpec((nb, ts // c, GDN_HEADS, GDN_HEAD_DIM), lambda b, i: (b, i, 0, 0))],
        out_specs=pl.BlockSpec((nb, ts, GDN_WIDTH), tok),
        out_shape=jax.ShapeDtypeStruct((bsz, s, GDN_WIDTH), F32),
        scratch_shapes=[pltpu.VMEM((nb * GDN_HEADS, GDN_HEAD_DIM, GDN_HEAD_DIM), F32)],
        compiler_params=_cparams("parallel", "arbitrary"),
        name="gdn_scan",
    )(qd, kd, u, w, ic, gl)


def _even_post_kernel(o_ref, zs_ref, yb_ref, nw_ref, y_ref):
    hd = GDN_HEAD_DIM
    o = o_ref[0]
    zs = zs_ref[0].astype(F32)
    nw = nw_ref[...]
    for h in range(GDN_HEADS):
        cols = slice(h * hd, (h + 1) * hd)
        oh = o[:, cols]
        on = oh * lax.rsqrt(jnp.mean(oh * oh, -1, keepdims=True) + NORM_EPS) * nw
        y_ref[0, :, cols] = (on * zs[:, cols]).astype(BF16)
    y_ref[0, :, GDN_WIDTH:] = yb_ref[0]


def even_post(o, o1, norm_w):
    bsz, s, _ = o.shape
    tm = min(TOKEN_TILE, s)
    wb = 512
    blk = lambda j: pl.BlockSpec((1, tm, wb), lambda b, i, j=j: (b, i, j))
    const = lambda b, i: (0, 0)
    return pl.pallas_call(
        _even_post_kernel,
        grid=(bsz, s // tm),
        in_specs=[pl.BlockSpec((1, tm, GDN_WIDTH), lambda b, i: (b, i, 0)),
                  blk(3), blk(4),
                  pl.BlockSpec((1, GDN_HEAD_DIM), const)],
        out_specs=pl.BlockSpec((1, tm, GDN_WIDTH + SC_WIDTH), lambda b, i: (b, i, 0)),
        out_shape=jax.ShapeDtypeStruct((bsz, s, GDN_WIDTH + SC_WIDTH), BF16),
        compiler_params=_cparams("parallel", "parallel"),
        name="even_post",
    )(o, o1, o1, norm_w.astype(F32).reshape(1, -1))


def _layer_norm(r, g, b):
    mu = jnp.mean(r, -1, keepdims=True)
    rc = r - mu
    var = jnp.mean(rc * rc, -1, keepdims=True)
    return rc * lax.rsqrt(var + LN_EPS) * g + b


def _route(logits_t, bias):
    scores = _sigmoid(logits_t)
    biased = scores + bias
    t = logits_t.shape[1]
    epg = EXPERTS_PER_GROUP
    sub = lax.broadcasted_iota(jnp.int32, (epg, t), 0).astype(F32)
    best = None
    for g in range(N_GROUPS):
        bgp = biased[g * epg:(g + 1) * epg]
        m1 = jnp.max(bgp, axis=0, keepdims=True)
        i1 = jnp.min(jnp.where(bgp == m1, sub, float(epg)), axis=0, keepdims=True)
        rest = jnp.where(sub == i1, -jnp.inf, bgp)
        m2 = jnp.max(rest, axis=0, keepdims=True)
        i2 = jnp.min(jnp.where(rest == m2, sub, float(epg)), axis=0, keepdims=True)
        gs = m1 + m2
        if best is None:
            best, e0, e1 = gs, i1, i2
        else:
            better = gs > best
            best = jnp.where(better, gs, best)
            e0 = jnp.where(better, float(g * epg) + i1, e0)
            e1 = jnp.where(better, float(g * epg) + i2, e1)
    eio = lax.broadcasted_iota(jnp.int32, scores.shape, 0).astype(F32)
    hit0 = eio == e0
    hit1 = eio == e1
    s0 = jnp.sum(jnp.where(hit0, scores, 0.0), axis=0, keepdims=True)
    s1 = jnp.sum(jnp.where(hit1, scores, 0.0), axis=0, keepdims=True)
    tot = s0 + s1
    return e0.astype(jnp.int32), e1.astype(jnp.int32), s0 / tot, s1 / tot, hit0, hit1


def _mix_out_kernel(y_ref, wo_ref, x_ref, moda_ref, modb_ref, lng_ref, lnb_ref, rwt_ref, rb_ref,
                    upper_ref, x1_ref, hm_ref, seli_ref, selw_ref, cnt_ref):
    d = D_MODEL
    gate = moda_ref[0, 0][:, 2 * d:]
    modb = modb_ref[0, 0]
    y = _dot(y_ref[0], wo_ref[...])
    x1 = _layer_norm(DN_ALPHA * x_ref[0] + (1.0 + gate) * y, lng_ref[...], lnb_ref[...])
    x1_ref[0] = x1
    hm = x1 * (1.0 + modb[:, d:2 * d]) + modb[:, :d]
    hm_hi = hm.astype(BF16)
    hm_ref[0] = hm_hi

    hm_lo = (hm - hm_hi.astype(F32)).astype(BF16)
    rw = rwt_ref[...]
    hi_lo = _dot_nt(rw, hm_hi)
    logits_t = hi_lo[:N_EXPERTS] + hi_lo[N_EXPERTS:] + _dot_nt(rw[:N_EXPERTS], hm_lo)
    e0, e1, w0, w1, hit0, hit1 = _route(logits_t, rb_ref[...])
    member = jnp.where(hit0 | hit1, 1.0, 0.0)
    before = _dot(member.astype(BF16), upper_ref[...])
    rank0 = jnp.sum(jnp.where(hit0, before, 0.0), axis=0, keepdims=True).astype(jnp.int32)
    rank1 = jnp.sum(jnp.where(hit1, before, 0.0), axis=0, keepdims=True).astype(jnp.int32)
    zi = jnp.zeros_like(e0)
    seli_ref[0] = jnp.concatenate([e0, e1, rank0, rank1, zi, zi, zi, zi], axis=0)
    wrows = jnp.concatenate([w0, w1, jnp.zeros((LANES - 2, w0.shape[1]), F32)], axis=0)
    selw_ref[0] = wrows.T[:, :8]
    cnt_ref[0, 0] = jnp.broadcast_to(jnp.sum(member, axis=1, keepdims=True), (N_EXPERTS, LANES))


def mix_out(ymix, w_out, x, moda, modb, ln_g, ln_b, router_w, router_bias, b0=0):
    bsz, s, d = ymix.shape
    tm = min(TOKEN_TILE, s)
    nt = s // tm
    r = np.arange(tm)
    upper = jnp.asarray((r[:, None] < r[None, :]).astype(np.float32), dtype=BF16)
    rw_f32 = router_w.T.astype(F32)
    rw_hi = rw_f32.astype(BF16)
    rwt = jnp.concatenate([rw_hi, (rw_f32 - rw_hi.astype(F32)).astype(BF16)], axis=0)
    tok = lambda b, i: (b, i, 0)
    const = lambda b, i: (0, 0)
    modspec = pl.BlockSpec((1, 1, 1, 3 * d), lambda b, i: (b, 0, 0, 0))
    row8 = pl.BlockSpec((1, 8, tm), lambda b, i: (b, 0, i))
    return pl.pallas_call(
        _mix_out_kernel,
        grid=(bsz, nt),
        in_specs=[pl.BlockSpec((1, tm, d), tok),
                  pl.BlockSpec((d, d), const),
                  pl.BlockSpec((1, tm, d), lambda b, i: (b + b0, i, 0)),
                  modspec, modspec,
                  pl.BlockSpec((1, d), const), pl.BlockSpec((1, d), const),
                  pl.BlockSpec((2 * N_EXPERTS, d), const),
                  pl.BlockSpec((N_EXPERTS, 1), const),
                  pl.BlockSpec((tm, tm), const)],
        out_specs=[pl.BlockSpec((1, tm, d), tok),
                   pl.BlockSpec((1, tm, d), tok),
                   row8, pl.BlockSpec((1, tm, 8), tok),
                   pl.BlockSpec((1, 1, N_EXPERTS, LANES), lambda b, i: (b, i, 0, 0))],
        out_shape=[jax.ShapeDtypeStruct((bsz, s, d), F32),
                   jax.ShapeDtypeStruct((bsz, s, d), BF16),
                   jax.ShapeDtypeStruct((bsz, 8, s), jnp.int32),
                   jax.ShapeDtypeStruct((bsz, s, 8), F32),
                   jax.ShapeDtypeStruct((bsz, nt, N_EXPERTS, LANES), F32)],
        compiler_params=_cparams("parallel", "parallel"),
        name="mix_out",
    )(ymix, w_out.astype(BF16), x, moda, modb, ln_g.reshape(1, d), ln_b.reshape(1, d),
      rwt, router_bias.astype(F32).reshape(N_EXPERTS, 1), upper)


def _experts_kernel(be_ref, next_ref, nused_ref, xs_ref, wg_hbm, wu_hbm, wd_hbm, ys_ref,
                    wgf_ref, wuf_ref, wdf_ref, wgb_ref, wub_ref, wdb_ref, sem_ref, slot_ref, *, layer):
    i = pl.program_id(0)

    def weight_copies(e, slot):
        return [pltpu.make_async_copy(src.at[layer, e], dst.at[slot], sem_ref.at[n, slot])
                for n, (src, dst) in enumerate(((wg_hbm, wgf_ref), (wu_hbm, wuf_ref), (wd_hbm, wdf_ref)))]

    @pl.when(i == 0)
    def _():
        slot_ref[0] = 0
        for cp in weight_copies(be_ref[0], 0):
            cp.start()

    @pl.when((i == 0) | (be_ref[i] != be_ref[jnp.maximum(i - 1, 0)]))
    def _():
        slot = slot_ref[0]
        for cp in weight_copies(be_ref[i], slot):
            cp.wait()

        @pl.when(next_ref[i] >= 0)
        def _():
            for cp in weight_copies(next_ref[i], 1 - slot):
                cp.start()

        wgb_ref[...] = wgf_ref[slot].astype(BF16)
        wub_ref[...] = wuf_ref[slot].astype(BF16)
        wdb_ref[...] = wdf_ref[slot].astype(BF16)
        slot_ref[0] = 1 - slot

    @pl.when(i < nused_ref[0])
    def _():
        x = xs_ref[...]
        act =(_silu(_dot(x, wgb_ref[...])) * _dot(x, wub_ref[...])).astype(BF16)
        ys_ref[...] = _dot(act, wdb_ref[...]).astype(BF16)

    @pl.when(i >= nused_ref[0])
    def _():
        ys_ref[...] = jnp.zeros_like(ys_ref)


def moe_experts(xs, block_expert, n_used, w_gate, w_up, w_down, layer):
    rows, d = xs.shape
    nb = rows // MOE_BLOCK
    later = jnp.where(block_expert[None, :] > block_expert[:, None], block_expert[None, :], N_EXPERTS)
    next_expert = jnp.min(later, axis=1)
    next_expert = jnp.where(next_expert == N_EXPERTS, -1, next_expert).astype(jnp.int32)
    hbm = pl.BlockSpec(memory_space=pl.ANY)
    rows_spec = pl.BlockSpec((MOE_BLOCK, d), lambda i, be, nx, nu: (i, 0))
    return pl.pallas_call(
        functools.partial(_experts_kernel, layer=layer),
        grid_spec=pltpu.PrefetchScalarGridSpec(
            num_scalar_prefetch=3,
            grid=(nb,),
            in_specs=[rows_spec, hbm, hbm, hbm],
            out_specs=rows_spec,
            scratch_shapes=[pltpu.VMEM((2, d, D_EXPERT), F32), pltpu.VMEM((2, d, D_EXPERT), F32),
                            pltpu.VMEM((2, D_EXPERT, d), F32),
                            pltpu.VMEM((d, D_EXPERT), BF16), pltpu.VMEM((d, D_EXPERT), BF16),
                            pltpu.VMEM((D_EXPERT, d), BF16),
                            pltpu.SemaphoreType.DMA((3, 2)), pltpu.SMEM((1,), jnp.int32)]),
        out_shape=jax.ShapeDtypeStruct((rows, d), BF16),
        compiler_params=_cparams("arbitrary"),
        name="moe_experts",
    )(block_expert, next_expert, n_used, xs, w_gate, w_up, w_down)


def _plan_kernel(seli_ref, base_ref, dest_ref):
    sel = seli_ref[0]
    base = base_ref[0, 0]
    eio = lax.broadcasted_iota(jnp.int32, (N_EXPERTS, sel.shape[1]), 0)
    out = []
    for kk in range(2):
        first = jnp.sum(jnp.where(eio == sel[kk:kk + 1], base, 0.0), axis=0, keepdims=True)
        out.append(first.astype(jnp.int32) + sel[2 + kk:3 + kk])
    zero = jnp.zeros_like(out[0])
    dest_ref[0] = jnp.concatenate(out + [zero] * 6, axis=0)


def dispatch_plan(seli, base):
    bsz, _, s = seli.shape
    nt = base.shape[1]
    tm = s // nt
    return pl.pallas_call(
        _plan_kernel,
        grid=(bsz, nt),
        in_specs=[pl.BlockSpec((1, 8, tm), lambda b, i: (b, 0, i)),
                  pl.BlockSpec((1, 1, N_EXPERTS, 1), lambda b, i: (b, i, 0, 0))],
        out_specs=pl.BlockSpec((1, 8, tm), lambda b, i: (b, 0, i)),
        out_shape=jax.ShapeDtypeStruct((bsz, 8, s), jnp.int32),
        compiler_params=_cparams("parallel", "parallel"),
        name="dispatch_plan",
    )(seli, base)


def _invert_kernel(dest_ref, pad_lo_ref, pad_hi_ref, slot_tok_ref):
    t = dest_ref.shape[0] // 2

    def clear_range(e, carry):
        hi = pad_hi_ref[e]
        n = (hi - pad_lo_ref[e] + CLEAR_CHUNK - 1) // CLEAR_CHUNK

        def clear(q, c):
            r0 = hi - (q + 1) * CLEAR_CHUNK
            tok0 = lax.rem(r0, t)
            for j in range(CLEAR_CHUNK):
                tok = tok0 + j
                slot_tok_ref[r0 + j] = jnp.where(tok >= t, tok - t, tok)
            return c
        return lax.fori_loop(0, n, clear, carry)

    def put(a, carry):
        slot_tok_ref[dest_ref[a]] = a
        slot_tok_ref[dest_ref[t + a]] = a
        return carry

    lax.fori_loop(0, pad_lo_ref.shape[0], clear_range, 0)
    lax.fori_loop(0, t, put, 0, unroll=16)


def invert_slots(dest_flat, pad_lo, pad_hi, rows):
    smem = pl.BlockSpec(memory_space=pltpu.SMEM)
    return pl.pallas_call(
        _invert_kernel,
        in_specs=[smem, smem, smem],
        out_specs=smem,
        out_shape=jax.ShapeDtypeStruct((rows,), jnp.int32),
        name="invert_slots",
    )(dest_flat, pad_lo, pad_hi)


def _moe_combine_kernel(g0_ref, g1_ref, wt_ref, x_ref, mod_ref, lng_ref, lnb_ref, o_ref):
    d = D_MODEL
    gate = mod_ref[0, 0][:, 2 * d:]
    wt = wt_ref[0]
    y = g0_ref[0, 0].astype(F32) * wt[:, 0:1] + g1_ref[0, 0].astype(F32) * wt[:, 1:2]
    o_ref[0] = _layer_norm(DN_ALPHA * x_ref[0] + (1.0 + gate) * y, lng_ref[...], lnb_ref[...])


def moe_combine(g, wt, x1, modb, ln_g, ln_b):
    bsz, s, d = x1.shape
    tm = min(TOKEN_TILE, s)
    tok = lambda b, i: (b, i, 0)
    const = lambda b, i: (0, 0)
    return pl.pallas_call(
        _moe_combine_kernel,
        grid=(bsz, s // tm),
        in_specs=[pl.BlockSpec((1, 1, tm, d), lambda b, i: (0, b, i, 0)),
                  pl.BlockSpec((1, 1, tm, d), lambda b, i: (1, b, i, 0)),
                  pl.BlockSpec((1, tm, 8), tok), pl.BlockSpec((1, tm, d), tok),
                  pl.BlockSpec((1, 1, 1, 3 * d), lambda b, i: (b, 0, 0, 0)),
                  pl.BlockSpec((1, d), const), pl.BlockSpec((1, d), const)],
        out_specs=pl.BlockSpec((1, tm, d), tok),
        out_shape=jax.ShapeDtypeStruct((bsz, s, d), F32),
        compiler_params=_cparams("parallel", "parallel"),
        name="moe_combine",
    )(g, g, wt, x1, modb, ln_g.reshape(1, d), ln_b.reshape(1, d))


def moe_layer(x1, hm, seli, selw, cnt, modb, ln_g, ln_b, w_gate, w_up, w_down, layer):
    bsz, s, d = x1.shape
    t = bsz * s
    tm = min(TOKEN_TILE, s)
    blk = MOE_BLOCK
    cnt = cnt[..., 0].reshape(-1, N_EXPERTS).astype(jnp.int32)
    tile_off = jnp.cumsum(cnt, axis=0) - cnt
    counts = cnt.sum(0)
    padded = (counts + blk - 1) // blk * blk
    pend = jnp.cumsum(padded)
    pstart = pend - padded
    nb = -(-(2 * t) // blk) + N_EXPERTS
    rows = nb * blk
    base = (pstart[None, :] + tile_off).astype(F32).reshape(bsz, s // tm, N_EXPERTS, 1)
    dest = dispatch_plan(seli, base)
    dest = jnp.swapaxes(dest[:, 0:2, :], 0, 1).reshape(2 * t)
    pad_lo = jnp.concatenate([pstart + counts, pend[-1:]]).astype(jnp.int32)
    pad_hi = jnp.concatenate([pend, jnp.full((1,), rows)]).astype(jnp.int32)
    slot_tok = invert_slots(dest, pad_lo, pad_hi, rows)
    starts = jnp.arange(nb, dtype=jnp.int32) * blk
    block_expert = jnp.minimum(jnp.sum((pend[None, :] <= starts[:, None]).astype(jnp.int32), axis=1),
                               N_EXPERTS - 1)
    n_used = (pend[-1] // blk).astype(jnp.int32).reshape(1)
    xs = hm.reshape(t, d).at[slot_tok].get(mode="promise_in_bounds")
    ys = moe_experts(xs, block_expert, n_used, w_gate, w_up, w_down, layer)
    g = ys.at[dest].get(mode="promise_in_bounds").reshape(2, bsz, s, d)
    return moe_combine(g, selw, x1, modb, ln_g, ln_b)


MLA_IN_COLS = Q_LORA + KV_LORA + 2 * LANES
MLA_QK = 2 * LANES


def _mla_in_kernel(x_ref, mod_ref, pos_ref, invf_ref, win_ref, qn_ref, kvn_ref, wq_ref, wkv_ref,
                   q_ref, k_ref, v_ref):
    d = D_MODEL
    nh = MLA_HEADS
    mod = mod_ref[0, 0]
    h = (x_ref[0] * (1.0 + mod[:, d:2 * d]) + mod[:, :d]).astype(BF16)
    proj = _dot(h, win_ref[...])
    ang = pos_ref[0].astype(F32) * invf_ref[...]
    lane = lax.broadcasted_iota(jnp.int32, ang.shape, 1)
    cos = jnp.where(lane < QK_ROPE, jnp.cos(ang), 0.0)
    sin = jnp.where(lane < QK_ROPE, jnp.sin(ang), 0.0)

    qa = proj[:, :Q_LORA]
    qa = (qa * lax.rsqrt(jnp.mean(qa * qa, -1, keepdims=True) + NORM_EPS) * qn_ref[...]).astype(BF16)
    kva = proj[:, Q_LORA:Q_LORA + KV_LORA]
    kva = (kva * lax.rsqrt(jnp.mean(kva * kva, -1, keepdims=True) + NORM_EPS) * kvn_ref[...]).astype(BF16)
    kr0 = Q_LORA + KV_LORA
    k_rope = (proj[:, kr0:kr0 + LANES] * cos + proj[:, kr0 + LANES:kr0 + 2 * LANES] * sin).astype(BF16)

    hw = nh * LANES
    q_nope = _dot(qa, wq_ref[:, :hw])
    q_rope = _dot(qa, wq_ref[:, hw:2 * hw])
    q_rot = _dot(qa, wq_ref[:, 2 * hw:])
    k_nope = _dot(kva, wkv_ref[:, :hw])
    v_ref[0] = _dot(kva, wkv_ref[:, hw:]).astype(BF16)
    for hh in range(nh):
        cols = slice(hh * LANES, (hh + 1) * LANES)
        q_ref[0, :, hh * MLA_QK:hh * MLA_QK + LANES] = q_nope[:, cols].astype(BF16)
        q_ref[0, :, hh * MLA_QK + LANES:(hh + 1) * MLA_QK] = (
            q_rope[:, cols] * cos + q_rot[:, cols] * sin).astype(BF16)
        k_ref[0, :, hh * MLA_QK:hh * MLA_QK + LANES] = k_nope[:, cols].astype(BF16)
        k_ref[0, :, hh * MLA_QK + LANES:(hh + 1) * MLA_QK] = k_rope


def _rope_cols(w):
    half = QK_ROPE // 2
    pad = [(0, 0)] * (w.ndim - 1) + [(0, LANES - QK_ROPE)]
    rot = jnp.concatenate([-w[..., half:], w[..., :half]], axis=-1)
    return jnp.pad(w, pad), jnp.pad(rot, pad)


def mla_in(x, mod, positions, w_in, q_a_norm, w_q_b, kv_a_norm, w_kv_b):
    bsz, s, d = x.shape
    tm = min(TOKEN_TILE, s)
    nh = MLA_HEADS
    kr, kr_rot = _rope_cols(w_in[:, Q_LORA + KV_LORA:])
    win = jnp.concatenate([w_in[:, :Q_LORA + KV_LORA], kr, kr_rot], axis=1).astype(BF16)
    wq = w_q_b.reshape(Q_LORA, nh, QK_NOPE + QK_ROPE)
    qr, qr_rot = _rope_cols(wq[..., QK_NOPE:])
    scale = (QK_NOPE + QK_ROPE) ** -0.5 * LOG2_E
    wq = (jnp.concatenate([wq[..., :QK_NOPE].reshape(Q_LORA, -1), qr.reshape(Q_LORA, -1),
                           qr_rot.reshape(Q_LORA, -1)], axis=1) * scale).astype(BF16)
    wkv = w_kv_b.reshape(KV_LORA, nh, QK_NOPE + V_DIM)
    wkv = jnp.concatenate([wkv[..., :QK_NOPE].reshape(KV_LORA, -1),
                           wkv[..., QK_NOPE:].reshape(KV_LORA, -1)], axis=1).astype(BF16)
    inv = ROPE_THETA ** (-np.arange(0, QK_ROPE, 2, dtype=np.float32) / QK_ROPE)
    invf = np.zeros((1, LANES), np.float32)
    invf[0, :QK_ROPE] = np.concatenate([inv, inv])
    tok = lambda b, i: (b, i, 0)
    const = lambda b, i: (0, 0)
    return pl.pallas_call(
        _mla_in_kernel,
        grid=(bsz, s // tm),
        in_specs=[pl.BlockSpec((1, tm, d), tok),
                  pl.BlockSpec((1, 1, 1, 3 * d), lambda b, i: (b, 0, 0, 0)),
                  pl.BlockSpec((1, tm, 1), tok),
                  pl.BlockSpec((1, LANES), const),
                  pl.BlockSpec((d, MLA_IN_COLS), const),
                  pl.BlockSpec((1, Q_LORA), const),
                  pl.BlockSpec((1, KV_LORA), const),
                  pl.BlockSpec((Q_LORA, 3 * nh * LANES), const),
                  pl.BlockSpec((KV_LORA, 2 * nh * LANES), const)],
        out_specs=[pl.BlockSpec((1, tm, nh * MLA_QK), tok),
                   pl.BlockSpec((1, tm, nh * MLA_QK), tok),
                   pl.BlockSpec((1, tm, nh * V_DIM), tok)],
        out_shape=[jax.ShapeDtypeStruct((bsz, s, nh * MLA_QK), BF16),
                   jax.ShapeDtypeStruct((bsz, s, nh * MLA_QK), BF16),
                   jax.ShapeDtypeStruct((bsz, s, nh * V_DIM), BF16)],
        compiler_params=_cparams("parallel", "parallel"),
        name="mla_in",
    )(x, mod, positions.reshape(bsz, s, 1), jnp.asarray(invf), win,
      q_a_norm.astype(F32).reshape(1, -1), kv_a_norm.astype(F32).reshape(1, -1), wq, wkv)


def _attn_kernel(q_ref, k_ref, v_ref, o_ref, m_ref, l_ref, acc_ref):
    tq = q_ref.shape[1]
    i = pl.program_id(2)
    m_ref[...] = jnp.full(m_ref.shape, -jnp.inf, F32)
    l_ref[...] = jnp.zeros(l_ref.shape, F32)
    acc_ref[...] = jnp.zeros(acc_ref.shape, F32)

    groups = [slice(g * ATTN_ROWS, (g + 1) * ATTN_ROWS) for g in range(tq // ATTN_ROWS)]

    def update(off, widths, masked):
        scs = []
        for rows, width in zip(groups, widths):
            sc = _dot_nt(q_ref[0, rows, :], k_ref[0, pl.ds(off, width), :])
            if masked:
                own = width - ATTN_ROWS
                qi = lax.broadcasted_iota(jnp.int32, (ATTN_ROWS, ATTN_ROWS), 0)
                ki = lax.broadcasted_iota(jnp.int32, (ATTN_ROWS, ATTN_ROWS), 1)
                tail = jnp.where(ki <= qi, sc[:, own:], -jnp.inf)
                sc = tail if own == 0 else jnp.concatenate([sc[:, :own], tail], axis=1)
            scs.append(sc)
        m_olds = [m_ref[rows] for rows in groups]
        m_news = [jnp.maximum(m_old, jnp.max(sc, -1, keepdims=True)) for m_old, sc in zip(m_olds, scs)]
        ps = [jnp.exp2(sc - jnp.concatenate([m_new] * (width // LANES), axis=1))
              for sc, m_new, width in zip(scs, m_news, widths)]
        alphas = [jnp.exp2(m_old - m_new) for m_old, m_new in zip(m_olds, m_news)]
        pvs = [_dot(p.astype(BF16), v_ref[0, pl.ds(off, width), :]) for p, width in zip(ps, widths)]
        for rows, m_new, alpha, p, pv in zip(groups, m_news, alphas, ps, pvs):
            l_ref[rows] = alpha * l_ref[rows] + jnp.sum(p, -1, keepdims=True)
            acc_ref[rows] = alpha * acc_ref[rows] + pv
            m_ref[rows] = m_new

    def full_pair(j, carry):
        update(pl.multiple_of(j * 2 * tq, 2 * tq), [2 * tq] * len(groups), False)
        return carry

    lax.fori_loop(0, i // 2, full_pair, 0)

    @pl.when(i % 2 == 1)
    def _():
        update(pl.multiple_of((i - 1) * tq, tq), [tq + rows.stop for rows in groups], True)

    @pl.when(i % 2 == 0)
    def _():
        update(pl.multiple_of(i * tq, tq), [rows.stop for rows in groups], True)

    o_ref[0] = (acc_ref[...] / l_ref[...]).astype(BF16)


def attention(q, k, v):
    bsz, s, _ = q.shape
    tq = min(ATTN_TILE, s)
    return pl.pallas_call(
        _attn_kernel,
        grid=(bsz, MLA_HEADS, s // tq),
        in_specs=[pl.BlockSpec((1, tq, MLA_QK), lambda b, h, i: (b, i, h)),
                  pl.BlockSpec((1, s, MLA_QK), lambda b, h, i: (b, 0, h)),
                  pl.BlockSpec((1, s, V_DIM), lambda b, h, i: (b, 0, h))],
        out_specs=pl.BlockSpec((1, tq, V_DIM), lambda b, h, i: (b, i, h)),
        out_shape=jax.ShapeDtypeStruct((bsz, s, MLA_HEADS * V_DIM), BF16),
        scratch_shapes=[pltpu.VMEM((tq, LANES), F32), pltpu.VMEM((tq, LANES), F32), pltpu.VMEM((tq, V_DIM), F32)],
        compiler_params=_cparams("parallel", "parallel", "arbitrary"),
        name="attention",
    )(q, k, v)


def kernel(x, c, positions, ada_w, ada_b, ln_g, ln_b, w_in_e, gdn_conv_w, gdn_a_log, gdn_dt_bias,
           gdn_norm_w, sc_conv_w, w_out_e, w_in_o, q_a_norm, w_q_b, kv_a_norm, w_kv_b, w_out_o,
           router_w, router_bias, w_gate, w_up, w_down):
    bsz, s, d = x.shape
    depth = ada_w.shape[0]
    mod = ada_mod(c, ada_w, ada_b).reshape(depth, 2, bsz, 1, 3 * d)

    def layers(x, b0, mod, positions):
        for i in range(depth):
            j = i // 2
            moda, modb = mod[i, 0][:, None], mod[i, 1][:, None]
            if i % 2 == 0:
                o1, bg, bgt = even_in_proj(x, moda, w_in_e[j], gdn_a_log[j], gdn_dt_bias[j],
                                           gdn_conv_w[j], sc_conv_w[j], b0)
                qd, kd, u, w, ic, gl = gdn_prep(o1, bg, bgt)
                o = gdn_scan(qd, kd, u, w, ic, gl)
                ymix = even_post(o, o1, gdn_norm_w[j])
                w_out = w_out_e[j]
            else:
                q, k, v = mla_in(x, moda, positions, w_in_o[j], q_a_norm[j], w_q_b[j], kv_a_norm[j],
                                 w_kv_b[j])
                ymix = attention(q, k, v)
                w_out = w_out_o[j]
            x1, hm, seli, selw, cnt = mix_out(ymix, w_out, x, moda, modb, ln_g[i, 0], ln_b[i, 0],
                                              router_w, router_bias, b0)
            x = moe_layer(x1, hm, seli, selw, cnt, modb, ln_g[i, 1], ln_b[i, 1], w_gate, w_up, w_down, i)
            b0 = 0
        return x

    ng = BATCH_GROUPS if bsz % BATCH_GROUPS == 0 else 1
    gb = bsz // ng
    outs = [layers(x, g * gb, mod[:, :, g * gb:(g + 1) * gb], positions[g * gb:(g + 1) * gb])
            for g in range(ng)]
    return outs[0] if ng == 1 else jnp.concatenate(outs, axis=0)
```

```python
import functools

import numpy as np
import jax
import jax.numpy as jnp
from jax import lax
from jax.experimental import pallas as pl
from jax.experimental.pallas import tpu as pltpu

F32 = jnp.float32
BF16 = jnp.bfloat16
HIGHEST = lax.Precision.HIGHEST

D_MODEL = 1024
DEPTH = 2
DN_ALPHA = (2.0 * DEPTH) ** 0.25

GDN_HEADS = 4
GDN_HEAD_DIM = 128
GDN_WIDTH = GDN_HEADS * GDN_HEAD_DIM
GDN_CONV = 4
GDN_CHUNK = 64
SC_WIDTH = 512
SC_CONV = 3

MLA_HEADS = 8
Q_LORA = 384
KV_LORA = 256
QK_NOPE = 128
QK_ROPE = 64
V_DIM = 128
ROPE_THETA = 10000.0

N_EXPERTS = 32
N_GROUPS = 4
EXPERTS_PER_GROUP = N_EXPERTS // N_GROUPS
D_EXPERT = 512

NORM_EPS = 1e-6
LN_EPS = 1e-5

LANES = 128
HALO = 16
TOKEN_TILE = 1024
GDN_TILE = 256
ATTN_TILE = 1024
ATTN_ROWS = 256
LOG2_E = 1.4426950408889634
MOE_BLOCK = 512
SCAN_BATCH = 4
BATCH_GROUPS = 1
CLEAR_CHUNK = 8
VMEM_LIMIT = 48 * 1024 * 1024


def _cparams(*sem):
    return pltpu.CompilerParams(dimension_semantics=sem, vmem_limit_bytes=VMEM_LIMIT)


def _sigmoid(x):
    return 1.0 / (1.0 + jnp.exp(-x))


def _silu(x):
    return x * _sigmoid(x)


def _dot(a, b):
    return jnp.dot(a, b, preferred_element_type=F32)


def _dot_nt(a, b, precision=None):
    return lax.dot_general(a, b, (((1,), (1,)), ((), ())), precision=precision,
                           preferred_element_type=F32)


def _dot_tn(a, b):
    return lax.dot_general(a, b, (((0,), (0,)), ((), ())), preferred_element_type=F32)


def _ada_kernel(c_ref, w_ref, b_ref, o_ref):
    cond = _silu(c_ref[...])
    o_ref[0] = jnp.dot(cond, w_ref[0], precision=HIGHEST, preferred_element_type=F32) + b_ref[0]


def ada_mod(c, ada_w, ada_b):
    nl = ada_w.shape[0] * ada_w.shape[1]
    bsz, d = c.shape
    w = ada_w.reshape(nl, d, 3 * d)
    b = ada_b.reshape(nl, 1, 3 * d)
    return pl.pallas_call(
        _ada_kernel,
        grid=(nl, 3),
        in_specs=[pl.BlockSpec((bsz, d), lambda l, j: (0, 0)),
                  pl.BlockSpec((1, d, d), lambda l, j: (l, 0, j)),
                  pl.BlockSpec((1, 1, d), lambda l, j: (l, 0, j))],
        out_specs=pl.BlockSpec((1, bsz, d), lambda l, j: (l, 0, j)),
        out_shape=jax.ShapeDtypeStruct((nl, bsz, 3 * d), F32),
        compiler_params=_cparams("parallel", "parallel"),
        name="ada_mod",
    )(c, w, b)


EVEN_MAIN = 3 * GDN_WIDTH + GDN_WIDTH + 3 * SC_WIDTH
EVEN_OUT = 3 * GDN_WIDTH + GDN_WIDTH + SC_WIDTH


def _causal_conv(x, halo, w, taps):
    rows = x.shape[0]
    nh = halo.shape[0]
    xf = jnp.concatenate([halo, x], axis=0)
    y = w[taps - 1:taps] * x
    for j in range(taps - 1):
        off = nh - (taps - 1) + j
        y = y + w[j:j + 1] * xf[off:off + rows]
    return y


def _even_in_kernel(x_ref, mod_ref, w1_ref, w2_ref, alog_ref, dtb_ref, gcw_ref, scw_ref,
                    o1_ref, o2_ref, o2t_ref, qkv_tail_ref, ch_tail_ref):
    d = D_MODEL
    cw = 512
    first = pl.program_id(1) == 0
    mod = mod_ref[0, 0]
    h = (x_ref[0] * (1.0 + mod[:, d:2 * d]) + mod[:, :d]).astype(BF16)
    tm = h.shape[0]
    gcw = gcw_ref[...]

    def project(j):
        return _dot(h, w1_ref[:, j * cw:(j + 1) * cw])

    def finish_qkv(j, r):
        cols = slice(j * cw, (j + 1) * cw)
        halo = jnp.where(first, 0.0, qkv_tail_ref[:, cols])
        o1_ref[0, :, cols] = _silu(_causal_conv(r, halo, gcw[:, cols], GDN_CONV)).astype(BF16)
        qkv_tail_ref[:, cols] = r[tm - HALO:]

    r0 = project(0)
    r1 = project(1)
    finish_qkv(0, r0)
    r2 = project(2)
    finish_qkv(1, r1)
    z = project(3)
    finish_qkv(2, r2)
    sc_c = project(5)
    o1_ref[0, :, 3 * cw:4 * cw] = _silu(z).astype(BF16)
    sc_h = project(6)
    sc_b = project(4)
    ch = sc_c * sc_h
    halo = jnp.where(first, 0.0, ch_tail_ref[...])
    r = _dot(h, w2_ref[...])
    o1_ref[0, :, 4 * cw:5 * cw] = (sc_b * _causal_conv(ch, halo, scw_ref[...], SC_CONV)).astype(BF16)
    ch_tail_ref[...] = ch[tm - HALO:]
    lane = lax.broadcasted_iota(jnp.int32, r.shape, 1)
    a = r + dtb_ref[...]
    softplus = jnp.maximum(a, 0.0) + jnp.log(1.0 + jnp.exp(-jnp.abs(a)))
    bg = jnp.where(lane < GDN_HEADS, _sigmoid(r), -jnp.exp(alog_ref[...]) * softplus)
    o2_ref[0] = bg
    o2t_ref[0] = bg.T[:8]


def even_in_proj(x, mod, w_in, a_log, dt_bias, gdn_conv_w, sc_conv_w, b0=0):
    _, s, d = x.shape
    bsz = mod.shape[0]
    tm = min(TOKEN_TILE, s)
    q_end = 4 * GDN_WIDTH
    w1 = jnp.concatenate([w_in[:, :q_end], w_in[:, q_end + 2 * GDN_HEADS:]], axis=1).astype(BF16)
    w2 = jnp.pad(w_in[:, q_end:q_end + 2 * GDN_HEADS], ((0, 0), (0, LANES - 2 * GDN_HEADS))).astype(BF16)
    alog = jnp.pad(a_log.astype(F32), (GDN_HEADS, LANES - 2 * GDN_HEADS)).reshape(1, LANES)
    dtb = jnp.pad(dt_bias.astype(F32), (GDN_HEADS, LANES - 2 * GDN_HEADS)).reshape(1, LANES)
    const = lambda b, i: (0, 0)
    return pl.pallas_call(
        _even_in_kernel,
        grid=(bsz, s // tm),
        in_specs=[pl.BlockSpec((1, tm, d), lambda b, i: (b + b0, i, 0)),
                  pl.BlockSpec((1, 1, 1, 3 * d), lambda b, i: (b, 0, 0, 0)),
                  pl.BlockSpec((d, EVEN_MAIN), const),
                  pl.BlockSpec((d, LANES), const),
                  pl.BlockSpec((1, LANES), const),
                  pl.BlockSpec((1, LANES), const),
                  pl.BlockSpec((GDN_CONV, 3 * GDN_WIDTH), const),
                  pl.BlockSpec((SC_CONV, SC_WIDTH), const)],
        out_specs=[pl.BlockSpec((1, tm, EVEN_OUT), lambda b, i: (b, i, 0)),
                   pl.BlockSpec((1, tm, LANES), lambda b, i: (b, i, 0)),
                   pl.BlockSpec((1, 8, tm), lambda b, i: (b, 0, i))],
        out_shape=[jax.ShapeDtypeStruct((bsz, s, EVEN_OUT), BF16),
                   jax.ShapeDtypeStruct((bsz, s, LANES), F32),
                   jax.ShapeDtypeStruct((bsz, 8, s), F32)],
        scratch_shapes=[pltpu.VMEM((HALO, 3 * GDN_WIDTH), F32), pltpu.VMEM((HALO, SC_WIDTH), F32)],
        compiler_params=_cparams("parallel", "arbitrary"),
        name="even_in_proj",
    )(x, mod, w1, w2, alog, dtb, gdn_conv_w.astype(F32), sc_conv_w.astype(F32))


def _inv_unit_lower(lows, xor_ij, block):
    eye = jnp.where(xor_ij == 0, 1.0, 0.0)
    ms = [eye - jnp.where(xor_ij == 1, low, 0.0) for low in lows]
    s = 2
    while s < block:
        level = (xor_ij >> (s.bit_length() - 1)) == 1
        mbs = [m.astype(BF16) for m in ms]
        cms = [_dot(jnp.where(level, low, 0.0).astype(BF16), mb).astype(BF16) for low, mb in zip(lows, mbs)]
        ms = [m - _dot(mb, cm) for m, mb, cm in zip(ms, mbs, cms)]
        s *= 2
    return ms


def _gdn_prep_kernel(qkv_ref, bg_ref, bgt_ref, ltri_ref, utri_ref, same_ref,
                     qd_ref, kd_ref, u_ref, w_ref, ic_ref, gl_ref):
    ts = qkv_ref.shape[1]
    c = GDN_CHUNK
    hd = GDN_HEAD_DIM
    y = qkv_ref[0].astype(F32)

    bg = bg_ref[0]
    gc_col = jnp.dot(ltri_ref[...], bg, precision=HIGHEST, preferred_element_type=F32)
    gc_row = jnp.dot(bgt_ref[0], utri_ref[...], precision=HIGHEST, preferred_element_type=F32)
    gc_end = jnp.dot(same_ref[...], bg, precision=HIGHEST, preferred_element_type=F32)

    ii = lax.broadcasted_iota(jnp.int32, (ts, ts), 0)
    jj = lax.broadcasted_iota(jnp.int32, (ts, ts), 1)
    xor_ij = ii ^ jj
    causal = (same_ref[...] > 0.0) & (ii >= jj)
    diag = xor_ij == 0

    heads = range(GDN_HEADS)
    gcols = [gc_col[:, GDN_HEADS + h:GDN_HEADS + h + 1] for h in heads]
    gends = [gc_end[:, GDN_HEADS + h:GDN_HEADS + h + 1] for h in heads]
    egcs = [jnp.exp(g) for g in gcols]
    lows, intras, rhss, qs, ks = [], [], [], [], []
    for h in heads:
        q = y[:, h * hd:(h + 1) * hd]
        k = y[:, GDN_WIDTH + h * hd:GDN_WIDTH + (h + 1) * hd]
        v = y[:, 2 * GDN_WIDTH + h * hd:2 * GDN_WIDTH + (h + 1) * hd]
        q = q * lax.rsqrt(jnp.sum(q * q, -1, keepdims=True) + NORM_EPS) * (hd ** -0.5)
        k = k * lax.rsqrt(jnp.sum(k * k, -1, keepdims=True) + NORM_EPS)
        beta = bg[:, h:h + 1]
        grow = gc_row[GDN_HEADS + h:GDN_HEADS + h + 1, :]
        decay = jnp.exp(jnp.where(causal, gcols[h] - grow, -jnp.inf))
        kb = k * beta
        kbf = k.astype(BF16)
        lows.append(jnp.where(diag, 0.0, _dot_nt(kb.astype(BF16), kbf) * decay))
        intras.append(_dot_nt(q.astype(BF16), kbf) * decay)
        rhss.append(jnp.concatenate([v * beta, kb * egcs[h]], axis=-1).astype(BF16))
        qs.append(q)
        ks.append(k)

    invs = _inv_unit_lower(lows, xor_ij, c)
    for h in heads:
        sol = _dot(invs[h].astype(BF16), rhss[h])
        cols = slice(h * hd, (h + 1) * hd)
        u_ref[0, :, cols] = sol[:, :hd]
        w_ref[0, :, cols] = sol[:, hd:].astype(BF16)
        qd_ref[0, :, cols] = (qs[h] * egcs[h]).astype(BF16)
        kd_ref[0, :, cols] = (ks[h] * jnp.exp(gends[h] - gcols[h])).astype(BF16)
        packed = intras[h][:, :c]
        for n in range(1, ts // c):
            packed = packed + intras[h][:, n * c:(n + 1) * c]
        ic_ref[0, :, h * c:(h + 1) * c] = packed.astype(BF16)
        for n in range(ts // c):
            gl_ref[0, n, h:h + 1, :] = jnp.broadcast_to(jnp.exp(gends[h][n * c:n * c + 1]), (1, hd))


def gdn_prep(o1, bg, bgt):
    bsz, s, _ = o1.shape
    ts = min(GDN_TILE, s)
    c = GDN_CHUNK
    nc = ts // c
    qkv_w = 3 * GDN_WIDTH
    r = np.arange(ts)
    same = (r[:, None] // c) == (r[None, :] // c)
    ltri = jnp.asarray((same & (r[:, None] >= r[None, :])).astype(np.float32))
    utri = jnp.asarray((same & (r[:, None] <= r[None, :])).astype(np.float32))
    same = jnp.asarray(same.astype(np.float32))
    tok = lambda b, i: (b, i, 0)
    const = lambda b, i: (0, 0)
    wide = jax.ShapeDtypeStruct((bsz, s, GDN_WIDTH), BF16)
    return pl.pallas_call(
        _gdn_prep_kernel,
        grid=(bsz, s // ts),
        in_specs=[pl.BlockSpec((1, ts, qkv_w), tok),
                  pl.BlockSpec((1, ts, LANES), tok),
                  pl.BlockSpec((1, 8, ts), lambda b, i: (b, 0, i)),
                  pl.BlockSpec((ts, ts), const),
                  pl.BlockSpec((ts, ts), const),
                  pl.BlockSpec((ts, ts), const)],
        out_specs=[pl.BlockSpec((1, ts, GDN_WIDTH), tok),
                   pl.BlockSpec((1, ts, GDN_WIDTH), tok),
                   pl.BlockSpec((1, ts, GDN_WIDTH), tok),
                   pl.BlockSpec((1, ts, GDN_WIDTH), tok),
                   pl.BlockSpec((1, ts, GDN_HEADS * c), tok),
                   pl.BlockSpec((1, nc, GDN_HEADS, GDN_HEAD_DIM), lambda b, i: (b, i, 0, 0))],
        out_shape=[wide, wide, jax.ShapeDtypeStruct((bsz, s, GDN_WIDTH), F32), wide,
                   jax.ShapeDtypeStruct((bsz, s, GDN_HEADS * c), BF16),
                   jax.ShapeDtypeStruct((bsz, s // c, GDN_HEADS, GDN_HEAD_DIM), F32)],
        compiler_params=_cparams("parallel", "parallel"),
        name="gdn_prep",
    )(o1, bg, bgt, ltri, utri, same)


def _gdn_scan_kernel(qd_ref, kd_ref, u_ref, w_ref, ic_ref, gl_ref, o_ref, state_ref):
    nb, ts = qd_ref.shape[0], qd_ref.shape[1]
    c = GDN_CHUNK
    hd = GDN_HEAD_DIM

    @pl.when(pl.program_id(1) == 0)
    def _():
        state_ref[...] = jnp.zeros_like(state_ref)

    chains = [(b, h) for b in range(nb) for h in range(GDN_HEADS)]
    cols = [slice(h * hd, (h + 1) * hd) for _, h in chains]
    sts = [state_ref[b * GDN_HEADS + h] for b, h in chains]
    for n in range(ts // c):
        rows = slice(n * c, (n + 1) * c)
        stbs = [st.astype(BF16) for st in sts]
        vbs = [(u_ref[b, rows, cl] - _dot(w_ref[b, rows, cl], stb)).astype(BF16)
               for (b, _), cl, stb in zip(chains, cols, stbs)]
        for (b, h), cl, stb, vb in zip(chains, cols, stbs, vbs):
            o_ref[b, rows, cl] = _dot(qd_ref[b, rows, cl], stb) + _dot(ic_ref[b, rows, h * c:(h + 1) * c], vb)
        sts = [st * gl_ref[b, n, h:h + 1, :] + _dot_tn(kd_ref[b, rows, cl], vb)
               for (b, h), cl, st, vb in zip(chains, cols, sts, vbs)]
    for (b, h), st in zip(chains, sts):
        state_ref[b * GDN_HEADS + h] = st


def gdn_scan(qd, kd, u, w, ic, gl):
    bsz, s, _ = qd.shape
    ts = min(GDN_TILE, s)
    c = GDN_CHUNK
    nb = SCAN_BATCH if bsz % SCAN_BATCH == 0 else 1
    tok = lambda b, i: (b, i, 0)
    return pl.pallas_call(
        _gdn_scan_kernel,
        grid=(bsz // nb, s // ts),
        in_specs=[pl.BlockSpec((nb, ts, GDN_WIDTH), tok)] * 4 + [
            pl.BlockSpec((nb, ts, GDN_HEADS * c), tok),
            pl.BlockSpec((nb, ts // c, GDN_HEADS, GDN_HEAD_DIM), lambda b, i: (b, i, 0, 0))],
        out_specs=pl.BlockSpec((nb, ts, GDN_WIDTH), tok),
        out_shape=jax.ShapeDtypeStruct((bsz, s, GDN_WIDTH), F32),
        scratch_shapes=[pltpu.VMEM((nb * GDN_HEADS, GDN_HEAD_DIM, GDN_HEAD_DIM), F32)],
        compiler_params=_cparams("parallel", "arbitrary"),
        name="gdn_scan",
    )(qd, kd, u, w, ic, gl)


def _even_post_kernel(o_ref, zs_ref, yb_ref, nw_ref, y_ref):
    hd = GDN_HEAD_DIM
    o = o_ref[0]
    zs = zs_ref[0].astype(F32)
    nw = nw_ref[...]
    for h in range(GDN_HEADS):
        cols = slice(h * hd, (h + 1) * hd)
        oh = o[:, cols]
        on = oh * lax.rsqrt(jnp.mean(oh * oh, -1, keepdims=True) + NORM_EPS) * nw
        y_ref[0, :, cols] = (on * zs[:, cols]).astype(BF16)
    y_ref[0, :, GDN_WIDTH:] = yb_ref[0]


def even_post(o, o1, norm_w):
    bsz, s, _ = o.shape
    tm = min(TOKEN_TILE, s)
    wb = 512
    blk = lambda j: pl.BlockSpec((1, tm, wb), lambda b, i, j=j: (b, i, j))
    const = lambda b, i: (0, 0)
    return pl.pallas_call(
        _even_post_kernel,
        grid=(bsz, s // tm),
        in_specs=[pl.BlockSpec((1, tm, GDN_WIDTH), lambda b, i: (b, i, 0)),
                  blk(3), blk(4),
                  pl.BlockSpec((1, GDN_HEAD_DIM), const)],
        out_specs=pl.BlockSpec((1, tm, GDN_WIDTH + SC_WIDTH), lambda b, i: (b, i, 0)),
        out_shape=jax.ShapeDtypeStruct((bsz, s, GDN_WIDTH + SC_WIDTH), BF16),
        compiler_params=_cparams("parallel", "parallel"),
        name="even_post",
    )(o, o1, o1, norm_w.astype(F32).reshape(1, -1))


def _layer_norm(r, g, b):
    mu = jnp.mean(r, -1, keepdims=True)
    rc = r - mu
    var = jnp.mean(rc * rc, -1, keepdims=True)
    return rc * lax.rsqrt(var + LN_EPS) * g + b


def _route(logits_t, bias):
    scores = _sigmoid(logits_t)
    biased = scores + bias
    t = logits_t.shape[1]
    epg = EXPERTS_PER_GROUP
    sub = lax.broadcasted_iota(jnp.int32, (epg, t), 0).astype(F32)
    best = None
    for g in range(N_GROUPS):
        bgp = biased[g * epg:(g + 1) * epg]
        m1 = jnp.max(bgp, axis=0, keepdims=True)
        i1 = jnp.min(jnp.where(bgp == m1, sub, float(epg)), axis=0, keepdims=True)
        rest = jnp.where(sub == i1, -jnp.inf, bgp)
        m2 = jnp.max(rest, axis=0, keepdims=True)
        i2 = jnp.min(jnp.where(rest == m2, sub, float(epg)), axis=0, keepdims=True)
        gs = m1 + m2
        if best is None:
            best, e0, e1 = gs, i1, i2
        else:
            better = gs > best
            best = jnp.where(better, gs, best)
            e0 = jnp.where(better, float(g * epg) + i1, e0)
            e1 = jnp.where(better, float(g * epg) + i2, e1)
    eio = lax.broadcasted_iota(jnp.int32, scores.shape, 0).astype(F32)
    hit0 = eio == e0
    hit1 = eio == e1
    s0 = jnp.sum(jnp.where(hit0, scores, 0.0), axis=0, keepdims=True)
    s1 = jnp.sum(jnp.where(hit1, scores, 0.0), axis=0, keepdims=True)
    tot = s0 + s1
    return e0.astype(jnp.int32), e1.astype(jnp.int32), s0 / tot, s1 / tot, hit0, hit1


def _mix_out_kernel(y_ref, wo_ref, x_ref, moda_ref, modb_ref, lng_ref, lnb_ref, rwt_ref, rb_ref,
                    upper_ref, x1_ref, hm_ref, seli_ref, selw_ref, cnt_ref):
    d = D_MODEL
    gate = moda_ref[0, 0][:, 2 * d:]
    modb = modb_ref[0, 0]
    y = _dot(y_ref[0], wo_ref[...])
    x1 = _layer_norm(DN_ALPHA * x_ref[0] + (1.0 + gate) * y, lng_ref[...], lnb_ref[...])
    x1_ref[0] = x1
    hm = x1 * (1.0 + modb[:, d:2 * d]) + modb[:, :d]
    hm_hi = hm.astype(BF16)
    hm_ref[0] = hm_hi

    hm_lo = (hm - hm_hi.astype(F32)).astype(BF16)
    rw = rwt_ref[...]
    hi_lo = _dot_nt(rw, hm_hi)
    logits_t = hi_lo[:N_EXPERTS] + hi_lo[N_EXPERTS:] + _dot_nt(rw[:N_EXPERTS], hm_lo)
    e0, e1, w0, w1, hit0, hit1 = _route(logits_t, rb_ref[...])
    member = jnp.where(hit0 | hit1, 1.0, 0.0)
    before = _dot(member.astype(BF16), upper_ref[...])
    rank0 = jnp.sum(jnp.where(hit0, before, 0.0), axis=0, keepdims=True).astype(jnp.int32)
    rank1 = jnp.sum(jnp.where(hit1, before, 0.0), axis=0, keepdims=True).astype(jnp.int32)
    zi = jnp.zeros_like(e0)
    seli_ref[0] = jnp.concatenate([e0, e1, rank0, rank1, zi, zi, zi, zi], axis=0)
    wrows = jnp.concatenate([w0, w1, jnp.zeros((LANES - 2, w0.shape[1]), F32)], axis=0)
    selw_ref[0] = wrows.T[:, :8]
    cnt_ref[0, 0] = jnp.broadcast_to(jnp.sum(member, axis=1, keepdims=True), (N_EXPERTS, LANES))


def mix_out(ymix, w_out, x, moda, modb, ln_g, ln_b, router_w, router_bias, b0=0):
    bsz, s, d = ymix.shape
    tm = min(TOKEN_TILE, s)
    nt = s // tm
    r = np.arange(tm)
    upper = jnp.asarray((r[:, None] < r[None, :]).astype(np.float32), dtype=BF16)
    rw_f32 = router_w.T.astype(F32)
    rw_hi = rw_f32.astype(BF16)
    rwt = jnp.concatenate([rw_hi, (rw_f32 - rw_hi.astype(F32)).astype(BF16)], axis=0)
    tok = lambda b, i: (b, i, 0)
    const = lambda b, i: (0, 0)
    modspec = pl.BlockSpec((1, 1, 1, 3 * d), lambda b, i: (b, 0, 0, 0))
    row8 = pl.BlockSpec((1, 8, tm), lambda b, i: (b, 0, i))
    return pl.pallas_call(
        _mix_out_kernel,
        grid=(bsz, nt),
        in_specs=[pl.BlockSpec((1, tm, d), tok),
                  pl.BlockSpec((d, d), const),
                  pl.BlockSpec((1, tm, d), lambda b, i: (b + b0, i, 0)),
                  modspec, modspec,
                  pl.BlockSpec((1, d), const), pl.BlockSpec((1, d), const),
                  pl.BlockSpec((2 * N_EXPERTS, d), const),
                  pl.BlockSpec((N_EXPERTS, 1), const),
                  pl.BlockSpec((tm, tm), const)],
        out_specs=[pl.BlockSpec((1, tm, d), tok),
                   pl.BlockSpec((1, tm, d), tok),
                   row8, pl.BlockSpec((1, tm, 8), tok),
                   pl.BlockSpec((1, 1, N_EXPERTS, LANES), lambda b, i: (b, i, 0, 0))],
        out_shape=[jax.ShapeDtypeStruct((bsz, s, d), F32),
                   jax.ShapeDtypeStruct((bsz, s, d), BF16),
                   jax.ShapeDtypeStruct((bsz, 8, s), jnp.int32),
                   jax.ShapeDtypeStruct((bsz, s, 8), F32),
                   jax.ShapeDtypeStruct((bsz, nt, N_EXPERTS, LANES), F32)],
        compiler_params=_cparams("parallel", "parallel"),
        name="mix_out",
    )(ymix, w_out.astype(BF16), x, moda, modb, ln_g.reshape(1, d), ln_b.reshape(1, d),
      rwt, router_bias.astype(F32).reshape(N_EXPERTS, 1), upper)


def _experts_kernel(be_ref, next_ref, nused_ref, xs_ref, wg_hbm, wu_hbm, wd_hbm, ys_ref,
                    wgf_ref, wuf_ref, wdf_ref, wgb_ref, wub_ref, wdb_ref, sem_ref, slot_ref, *, layer):
    i = pl.program_id(0)

    def weight_copies(e, slot):
        return [pltpu.make_async_copy(src.at[layer, e], dst.at[slot], sem_ref.at[n, slot])
                for n, (src, dst) in enumerate(((wg_hbm, wgf_ref), (wu_hbm, wuf_ref), (wd_hbm, wdf_ref)))]

    @pl.when(i == 0)
    def _():
        slot_ref[0] = 0
        for cp in weight_copies(be_ref[0], 0):
            cp.start()

    @pl.when((i == 0) | (be_ref[i] != be_ref[jnp.maximum(i - 1, 0)]))
    def _():
        slot = slot_ref[0]
        for cp in weight_copies(be_ref[i], slot):
            cp.wait()

        @pl.when(next_ref[i] >= 0)
        def _():
            for cp in weight_copies(next_ref[i], 1 - slot):
                cp.start()

        wgb_ref[...] = wgf_ref[slot].astype(BF16)
        wub_ref[...] = wuf_ref[slot].astype(BF16)
        wdb_ref[...] = wdf_ref[slot].astype(BF16)
        slot_ref[0] = 1 - slot

    @pl.when(i < nused_ref[0])
    def _():
        x = xs_ref[...]
        act =(_silu(_dot(x, wgb_ref[...])) * _dot(x, wub_ref[...])).astype(BF16)
        ys_ref[...] = _dot(act, wdb_ref[...]).astype(BF16)

    @pl.when(i >= nused_ref[0])
    def _():
        ys_ref[...] = jnp.zeros_like(ys_ref)


def moe_experts(xs, block_expert, n_used, w_gate, w_up, w_down, layer):
    rows, d = xs.shape
    nb = rows // MOE_BLOCK
    later = jnp.where(block_expert[None, :] > block_expert[:, None], block_expert[None, :], N_EXPERTS)
    next_expert = jnp.min(later, axis=1)
    next_expert = jnp.where(next_expert == N_EXPERTS, -1, next_expert).astype(jnp.int32)
    hbm = pl.BlockSpec(memory_space=pl.ANY)
    rows_spec = pl.BlockSpec((MOE_BLOCK, d), lambda i, be, nx, nu: (i, 0))
    return pl.pallas_call(
        functools.partial(_experts_kernel, layer=layer),
        grid_spec=pltpu.PrefetchScalarGridSpec(
            num_scalar_prefetch=3,
            grid=(nb,),
            in_specs=[rows_spec, hbm, hbm, hbm],
            out_specs=rows_spec,
            scratch_shapes=[pltpu.VMEM((2, d, D_EXPERT), F32), pltpu.VMEM((2, d, D_EXPERT), F32),
                            pltpu.VMEM((2, D_EXPERT, d), F32),
                            pltpu.VMEM((d, D_EXPERT), BF16), pltpu.VMEM((d, D_EXPERT), BF16),
                            pltpu.VMEM((D_EXPERT, d), BF16),
                            pltpu.SemaphoreType.DMA((3, 2)), pltpu.SMEM((1,), jnp.int32)]),
        out_shape=jax.ShapeDtypeStruct((rows, d), BF16),
        compiler_params=_cparams("arbitrary"),
        name="moe_experts",
    )(block_expert, next_expert, n_used, xs, w_gate, w_up, w_down)


def _plan_kernel(seli_ref, base_ref, dest_ref):
    sel = seli_ref[0]
    base = base_ref[0, 0]
    eio = lax.broadcasted_iota(jnp.int32, (N_EXPERTS, sel.shape[1]), 0)
    out = []
    for kk in range(2):
        first = jnp.sum(jnp.where(eio == sel[kk:kk + 1], base, 0.0), axis=0, keepdims=True)
        out.append(first.astype(jnp.int32) + sel[2 + kk:3 + kk])
    zero = jnp.zeros_like(out[0])
    dest_ref[0] = jnp.concatenate(out + [zero] * 6, axis=0)


def dispatch_plan(seli, base):
    bsz, _, s = seli.shape
    nt = base.shape[1]
    tm = s // nt
    return pl.pallas_call(
        _plan_kernel,
        grid=(bsz, nt),
        in_specs=[pl.BlockSpec((1, 8, tm), lambda b, i: (b, 0, i)),
                  pl.BlockSpec((1, 1, N_EXPERTS, 1), lambda b, i: (b, i, 0, 0))],
        out_specs=pl.BlockSpec((1, 8, tm), lambda b, i: (b, 0, i)),
        out_shape=jax.ShapeDtypeStruct((bsz, 8, s), jnp.int32),
        compiler_params=_cparams("parallel", "parallel"),
        name="dispatch_plan",
    )(seli, base)


def _invert_kernel(dest_ref, pad_lo_ref, pad_hi_ref, slot_tok_ref):
    t = dest_ref.shape[0] // 2

    def clear_range(e, carry):
        hi = pad_hi_ref[e]
        n = (hi - pad_lo_ref[e] + CLEAR_CHUNK - 1) // CLEAR_CHUNK

        def clear(q, c):
            r0 = hi - (q + 1) * CLEAR_CHUNK
            tok0 = lax.rem(r0, t)
            for j in range(CLEAR_CHUNK):
                tok = tok0 + j
                slot_tok_ref[r0 + j] = jnp.where(tok >= t, tok - t, tok)
            return c
        return lax.fori_loop(0, n, clear, carry)

    def put(a, carry):
        slot_tok_ref[dest_ref[a]] = a
        slot_tok_ref[dest_ref[t + a]] = a
        return carry

    lax.fori_loop(0, pad_lo_ref.shape[0], clear_range, 0)
    lax.fori_loop(0, t, put, 0, unroll=16)


def invert_slots(dest_flat, pad_lo, pad_hi, rows):
    smem = pl.BlockSpec(memory_space=pltpu.SMEM)
    return pl.pallas_call(
        _invert_kernel,
        in_specs=[smem, smem, smem],
        out_specs=smem,
        out_shape=jax.ShapeDtypeStruct((rows,), jnp.int32),
        name="invert_slots",
    )(dest_flat, pad_lo, pad_hi)


def _moe_combine_kernel(g0_ref, g1_ref, wt_ref, x_ref, mod_ref, lng_ref, lnb_ref, o_ref):
    d = D_MODEL
    gate = mod_ref[0, 0][:, 2 * d:]
    wt = wt_ref[0]
    y = g0_ref[0, 0].astype(F32) * wt[:, 0:1] + g1_ref[0, 0].astype(F32) * wt[:, 1:2]
    o_ref[0] = _layer_norm(DN_ALPHA * x_ref[0] + (1.0 + gate) * y, lng_ref[...], lnb_ref[...])


def moe_combine(g, wt, x1, modb, ln_g, ln_b):
    bsz, s, d = x1.shape
    tm = min(TOKEN_TILE, s)
    tok = lambda b, i: (b, i, 0)
    const = lambda b, i: (0, 0)
    return pl.pallas_call(
        _moe_combine_kernel,
        grid=(bsz, s // tm),
        in_specs=[pl.BlockSpec((1, 1, tm, d), lambda b, i: (0, b, i, 0)),
                  pl.BlockSpec((1, 1, tm, d), lambda b, i: (1, b, i, 0)),
                  pl.BlockSpec((1, tm, 8), tok), pl.BlockSpec((1, tm, d), tok),
                  pl.BlockSpec((1, 1, 1, 3 * d), lambda b, i: (b, 0, 0, 0)),
                  pl.BlockSpec((1, d), const), pl.BlockSpec((1, d), const)],
        out_specs=pl.BlockSpec((1, tm, d), tok),
        out_shape=jax.ShapeDtypeStruct((bsz, s, d), F32),
        compiler_params=_cparams("parallel", "parallel"),
        name="moe_combine",
    )(g, g, wt, x1, modb, ln_g.reshape(1, d), ln_b.reshape(1, d))


def moe_layer(x1, hm, seli, selw, cnt, modb, ln_g, ln_b, w_gate, w_up, w_down, layer):
    bsz, s, d = x1.shape
    t = bsz * s
    tm = min(TOKEN_TILE, s)
    blk = MOE_BLOCK
    cnt = cnt[..., 0].reshape(-1, N_EXPERTS).astype(jnp.int32)
    tile_off = jnp.cumsum(cnt, axis=0) - cnt
    counts = cnt.sum(0)
    padded = (counts + blk - 1) // blk * blk
    pend = jnp.cumsum(padded)
    pstart = pend - padded
    nb = -(-(2 * t) // blk) + N_EXPERTS
    rows = nb * blk
    base = (pstart[None, :] + tile_off).astype(F32).reshape(bsz, s // tm, N_EXPERTS, 1)
    dest = dispatch_plan(seli, base)
    dest = jnp.swapaxes(dest[:, 0:2, :], 0, 1).reshape(2 * t)
    pad_lo = jnp.concatenate([pstart + counts, pend[-1:]]).astype(jnp.int32)
    pad_hi = jnp.concatenate([pend, jnp.full((1,), rows)]).astype(jnp.int32)
    slot_tok = invert_slots(dest, pad_lo, pad_hi, rows)
    starts = jnp.arange(nb, dtype=jnp.int32) * blk
    block_expert = jnp.minimum(jnp.sum((pend[None, :] <= starts[:, None]).astype(jnp.int32), axis=1),
                               N_EXPERTS - 1)
    n_used = (pend[-1] // blk).astype(jnp.int32).reshape(1)
    xs = hm.reshape(t, d).at[slot_tok].get(mode="promise_in_bounds")
    ys = moe_experts(xs, block_expert, n_used, w_gate, w_up, w_down, layer)
    g = ys.at[dest].get(mode="promise_in_bounds").reshape(2, bsz, s, d)
    return moe_combine(g, selw, x1, modb, ln_g, ln_b)


MLA_IN_COLS = Q_LORA + KV_LORA + 2 * LANES
MLA_QK = 2 * LANES


def _mla_in_kernel(x_ref, mod_ref, pos_ref, invf_ref, win_ref, qn_ref, kvn_ref, wq_ref, wkv_ref,
                   q_ref, k_ref, v_ref):
    d = D_MODEL
    nh = MLA_HEADS
    mod = mod_ref[0, 0]
    h = (x_ref[0] * (1.0 + mod[:, d:2 * d]) + mod[:, :d]).astype(BF16)
    proj = _dot(h, win_ref[...])
    ang = pos_ref[0].astype(F32) * invf_ref[...]
    lane = lax.broadcasted_iota(jnp.int32, ang.shape, 1)
    cos = jnp.where(lane < QK_ROPE, jnp.cos(ang), 0.0)
    sin = jnp.where(lane < QK_ROPE, jnp.sin(ang), 0.0)

    qa = proj[:, :Q_LORA]
    qa = (qa * lax.rsqrt(jnp.mean(qa * qa, -1, keepdims=True) + NORM_EPS) * qn_ref[...]).astype(BF16)
    kva = proj[:, Q_LORA:Q_LORA + KV_LORA]
    kva = (kva * lax.rsqrt(jnp.mean(kva * kva, -1, keepdims=True) + NORM_EPS) * kvn_ref[...]).astype(BF16)
    kr0 = Q_LORA + KV_LORA
    k_rope = (proj[:, kr0:kr0 + LANES] * cos + proj[:, kr0 + LANES:kr0 + 2 * LANES] * sin).astype(BF16)

    hw = nh * LANES
    q_nope = _dot(qa, wq_ref[:, :hw])
    q_rope = _dot(qa, wq_ref[:, hw:2 * hw])
    q_rot = _dot(qa, wq_ref[:, 2 * hw:])
    k_nope = _dot(kva, wkv_ref[:, :hw])
    v_ref[0] = _dot(kva, wkv_ref[:, hw:]).astype(BF16)
    for hh in range(nh):
        cols = slice(hh * LANES, (hh + 1) * LANES)
        q_ref[0, :, hh * MLA_QK:hh * MLA_QK + LANES] = q_nope[:, cols].astype(BF16)
        q_ref[0, :, hh * MLA_QK + LANES:(hh + 1) * MLA_QK] = (
            q_rope[:, cols] * cos + q_rot[:, cols] * sin).astype(BF16)
        k_ref[0, :, hh * MLA_QK:hh * MLA_QK + LANES] = k_nope[:, cols].astype(BF16)
        k_ref[0, :, hh * MLA_QK + LANES:(hh + 1) * MLA_QK] = k_rope


def _rope_cols(w):
    half = QK_ROPE // 2
    pad = [(0, 0)] * (w.ndim - 1) + [(0, LANES - QK_ROPE)]
    rot = jnp.concatenate([-w[..., half:], w[..., :half]], axis=-1)
    return jnp.pad(w, pad), jnp.pad(rot, pad)


def mla_in(x, mod, positions, w_in, q_a_norm, w_q_b, kv_a_norm, w_kv_b):
    bsz, s, d = x.shape
    tm = min(TOKEN_TILE, s)
    nh = MLA_HEADS
    kr, kr_rot = _rope_cols(w_in[:, Q_LORA + KV_LORA:])
    win = jnp.concatenate([w_in[:, :Q_LORA + KV_LORA], kr, kr_rot], axis=1).astype(BF16)
    wq = w_q_b.reshape(Q_LORA, nh, QK_NOPE + QK_ROPE)
    qr, qr_rot = _rope_cols(wq[..., QK_NOPE:])
    scale = (QK_NOPE + QK_ROPE) ** -0.5 * LOG2_E
    wq = (jnp.concatenate([wq[..., :QK_NOPE].reshape(Q_LORA, -1), qr.reshape(Q_LORA, -1),
                           qr_rot.reshape(Q_LORA, -1)], axis=1) * scale).astype(BF16)
    wkv = w_kv_b.reshape(KV_LORA, nh, QK_NOPE + V_DIM)
    wkv = jnp.concatenate([wkv[..., :QK_NOPE].reshape(KV_LORA, -1),
                           wkv[..., QK_NOPE:].reshape(KV_LORA, -1)], axis=1).astype(BF16)
    inv = ROPE_THETA ** (-np.arange(0, QK_ROPE, 2, dtype=np.float32) / QK_ROPE)
    invf = np.zeros((1, LANES), np.float32)
    invf[0, :QK_ROPE] = np.concatenate([inv, inv])
    tok = lambda b, i: (b, i, 0)
    const = lambda b, i: (0, 0)
    return pl.pallas_call(
        _mla_in_kernel,
        grid=(bsz, s // tm),
        in_specs=[pl.BlockSpec((1, tm, d), tok),
                  pl.BlockSpec((1, 1, 1, 3 * d), lambda b, i: (b, 0, 0, 0)),
                  pl.BlockSpec((1, tm, 1), tok),
                  pl.BlockSpec((1, LANES), const),
                  pl.BlockSpec((d, MLA_IN_COLS), const),
                  pl.BlockSpec((1, Q_LORA), const),
                  pl.BlockSpec((1, KV_LORA), const),
                  pl.BlockSpec((Q_LORA, 3 * nh * LANES), const),
                  pl.BlockSpec((KV_LORA, 2 * nh * LANES), const)],
        out_specs=[pl.BlockSpec((1, tm, nh * MLA_QK), tok),
                   pl.BlockSpec((1, tm, nh * MLA_QK), tok),
                   pl.BlockSpec((1, tm, nh * V_DIM), tok)],
        out_shape=[jax.ShapeDtypeStruct((bsz, s, nh * MLA_QK), BF16),
                   jax.ShapeDtypeStruct((bsz, s, nh * MLA_QK), BF16),
                   jax.ShapeDtypeStruct((bsz, s, nh * V_DIM), BF16)],
        compiler_params=_cparams("parallel", "parallel"),
        name="mla_in",
    )(x, mod, positions.reshape(bsz, s, 1), jnp.asarray(invf), win,
      q_a_norm.astype(F32).reshape(1, -1), kv_a_norm.astype(F32).reshape(1, -1), wq, wkv)


def _attn_kernel(q_ref, k_ref, v_ref, o_ref, m_ref, l_ref, acc_ref):
    tq = q_ref.shape[1]
    i = pl.program_id(2)
    m_ref[...] = jnp.full(m_ref.shape, -jnp.inf, F32)
    l_ref[...] = jnp.zeros(l_ref.shape, F32)
    acc_ref[...] = jnp.zeros(acc_ref.shape, F32)

    groups = [slice(g * ATTN_ROWS, (g + 1) * ATTN_ROWS) for g in range(tq // ATTN_ROWS)]

    def update(off, widths, masked):
        scs = []
        for rows, width in zip(groups, widths):
            sc = _dot_nt(q_ref[0, rows, :], k_ref[0, pl.ds(off, width), :])
            if masked:
                own = width - ATTN_ROWS
                qi = lax.broadcasted_iota(jnp.int32, (ATTN_ROWS, ATTN_ROWS), 0)
                ki = lax.broadcasted_iota(jnp.int32, (ATTN_ROWS, ATTN_ROWS), 1)
                tail = jnp.where(ki <= qi, sc[:, own:], -jnp.inf)
                sc = tail if own == 0 else jnp.concatenate([sc[:, :own], tail], axis=1)
            scs.append(sc)
        m_olds = [m_ref[rows] for rows in groups]
        m_news = [jnp.maximum(m_old, jnp.max(sc, -1, keepdims=True)) for m_old, sc in zip(m_olds, scs)]
        ps = [jnp.exp2(sc - jnp.concatenate([m_new] * (width // LANES), axis=1))
              for sc, m_new, width in zip(scs, m_news, widths)]
        alphas = [jnp.exp2(m_old - m_new) for m_old, m_new in zip(m_olds, m_news)]
        pvs = [_dot(p.astype(BF16), v_ref[0, pl.ds(off, width), :]) for p, width in zip(ps, widths)]
        for rows, m_new, alpha, p, pv in zip(groups, m_news, alphas, ps, pvs):
            l_ref[rows] = alpha * l_ref[rows] + jnp.sum(p, -1, keepdims=True)
            acc_ref[rows] = alpha * acc_ref[rows] + pv
            m_ref[rows] = m_new

    def full_pair(j, carry):
        update(pl.multiple_of(j * 2 * tq, 2 * tq), [2 * tq] * len(groups), False)
        return carry

    lax.fori_loop(0, i // 2, full_pair, 0)

    @pl.when(i % 2 == 1)
    def _():
        update(pl.multiple_of((i - 1) * tq, tq), [tq + rows.stop for rows in groups], True)

    @pl.when(i % 2 == 0)
    def _():
        update(pl.multiple_of(i * tq, tq), [rows.stop for rows in groups], True)

    o_ref[0] = (acc_ref[...] / l_ref[...]).astype(BF16)


def attention(q, k, v):
    bsz, s, _ = q.shape
    tq = min(ATTN_TILE, s)
    return pl.pallas_call(
        _attn_kernel,
        grid=(bsz, MLA_HEADS, s // tq),
        in_specs=[pl.BlockSpec((1, tq, MLA_QK), lambda b, h, i: (b, i, h)),
                  pl.BlockSpec((1, s, MLA_QK), lambda b, h, i: (b, 0, h)),
                  pl.BlockSpec((1, s, V_DIM), lambda b, h, i: (b, 0, h))],
        out_specs=pl.BlockSpec((1, tq, V_DIM), lambda b, h, i: (b, i, h)),
        out_shape=jax.ShapeDtypeStruct((bsz, s, MLA_HEADS * V_DIM), BF16),
        scratch_shapes=[pltpu.VMEM((tq, LANES), F32), pltpu.VMEM((tq, LANES), F32), pltpu.VMEM((tq, V_DIM), F32)],
        compiler_params=_cparams("parallel", "parallel", "arbitrary"),
        name="attention",
    )(q, k, v)


def kernel(x, c, positions, ada_w, ada_b, ln_g, ln_b, w_in_e, gdn_conv_w, gdn_a_log, gdn_dt_bias,
           gdn_norm_w, sc_conv_w, w_out_e, w_in_o, q_a_norm, w_q_b, kv_a_norm, w_kv_b, w_out_o,
           router_w, router_bias, w_gate, w_up, w_down):
    bsz, s, d = x.shape
    depth = ada_w.shape[0]
    mod = ada_mod(c, ada_w, ada_b).reshape(depth, 2, bsz, 1, 3 * d)

    def layers(x, b0, mod, positions):
        for i in range(depth):
            j = i // 2
            moda, modb = mod[i, 0][:, None], mod[i, 1][:, None]
            if i % 2 == 0:
                o1, bg, bgt = even_in_proj(x, moda, w_in_e[j], gdn_a_log[j], gdn_dt_bias[j],
                                           gdn_conv_w[j], sc_conv_w[j], b0)
                qd, kd, u, w, ic, gl = gdn_prep(o1, bg, bgt)
                o = gdn_scan(qd, kd, u, w, ic, gl)
                ymix = even_post(o, o1, gdn_norm_w[j])
                w_out = w_out_e[j]
            else:
                q, k, v = mla_in(x, moda, positions, w_in_o[j], q_a_norm[j], w_q_b[j], kv_a_norm[j],
                                 w_kv_b[j])
                ymix = attention(q, k, v)
                w_out = w_out_o[j]
            x1, hm, seli, selw, cnt = mix_out(ymix, w_out, x, moda, modb, ln_g[i, 0], ln_b[i, 0],
                                              router_w, router_bias, b0)
            x = moe_layer(x1, hm, seli, selw, cnt, modb, ln_g[i, 1], ln_b[i, 1], w_gate, w_up, w_down, i)
            b0 = 0
        return x

    ng = BATCH_GROUPS if bsz % BATCH_GROUPS == 0 else 1
    gb = bsz // ng
    outs = [layers(x, g * gb, mod[:, :, g * gb:(g + 1) * gb], positions[g * gb:(g + 1) * gb])
            for g in range(ng)]
    return outs[0] if ng == 1 else jnp.concatenate(outs, axis=0)
```
